```python
import jax, jax.numpy as jnp
from jax import lax
import numpy as np

D_MODEL = 1024
BATCH = 8
SEQ = 2048
DEPTH = 1
DEC_BATCH = 128
DEC_SEQ = 4
PAST_LEN = 16384
PAGE_SIZE = 128

D_CONV = D_MODEL // 2
D_POOL = D_MODEL - D_CONV
D_MIX = D_CONV + D_POOL
CONV_WIDTH = 31
CONV_CTX = CONV_WIDTH - 1
CONV_HEADS = 8
POOL_WINDOWS = (2, 4, 8, 16)
N_POOL_GROUPS = len(POOL_WINDOWS)
POOL_GROUP = D_POOL // N_POOL_GROUPS
POOL_CTX = max(POOL_WINDOWS) - 1
N_MEM = 256
X_HEADS = 4
X_HEAD_DIM = D_MODEL // X_HEADS
D_FF = 4 * D_MODEL
LN_EPS = 1e-5
DN_ALPHA = (2.0 * DEPTH) ** 0.25
DN_BETA = (8.0 * DEPTH) ** -0.25

kernel_name = "hybrid_conv_pool_xattn_decoder_step"


def layer_norm(x, g, b):
    xf = x.astype(jnp.float32)
    mu = jnp.mean(xf, axis=-1, keepdims=True)
    var = jnp.mean(jnp.square(xf - mu), axis=-1, keepdims=True)
    y = (xf - mu) * lax.rsqrt(var + LN_EPS)
    return (y * g.astype(jnp.float32) + b.astype(jnp.float32)).astype(x.dtype)


def group_norm_tokens(x, g, b):
    B, T, C = x.shape
    xf = x.astype(jnp.float32).reshape(B, T, CONV_HEADS, C // CONV_HEADS)
    mu = jnp.mean(xf, axis=-1, keepdims=True)
    var = jnp.mean(jnp.square(xf - mu), axis=-1, keepdims=True)
    y = ((xf - mu) * lax.rsqrt(var + LN_EPS)).reshape(B, T, C)
    return (y * g.astype(jnp.float32) + b.astype(jnp.float32)).astype(x.dtype)


def conv_branch(u, ctx, conv_w, conv_b, gn_g, gn_b):
    u_ext = jnp.concatenate([ctx.astype(u.dtype), u], axis=1)
    y = lax.conv_general_dilated(
        u_ext, conv_w[:, None, :].astype(u.dtype), window_strides=(1,), padding="VALID",
        dimension_numbers=("NWC", "WIO", "NWC"), feature_group_count=D_CONV)
    y = group_norm_tokens(y + conv_b, gn_g, gn_b)
    return jax.nn.silu(y), u_ext[:, -CONV_CTX:]


def pool_branch(p, ctx, pos0, pool_w, pool_scale):
    B, T, _ = p.shape
    p_ext = jnp.concatenate([ctx.astype(p.dtype), p], axis=1)
    L = p_ext.shape[1]
    pf = p_ext.astype(jnp.float32).reshape(B, L, N_POOL_GROUPS, POOL_GROUP)
    csum = jnp.cumsum(pf, axis=1)
    pos = pos0 + jnp.arange(T)
    means = []
    for gi, w in enumerate(POOL_WINDOWS):
        c = csum[:, :, gi]
        c_shift = jnp.pad(c, ((0, 0), (w, 0), (0, 0)))[:, :L]
        win_sum = (c - c_shift)[:, POOL_CTX:]
        count = jnp.minimum(pos + 1, w).astype(jnp.float32)[None, :, None]
        means.append(win_sum / count)
    mean = jnp.stack(means, axis=2)
    delta = (mean - pf[:, POOL_CTX:]).astype(p.dtype)
    mixed = jnp.einsum("btgc,gcd->btgd", delta, pool_w).reshape(B, T, D_POOL)
    return mixed * pool_scale, p_ext[:, -POOL_CTX:]


def mem_kv(mem, xk_w, xv_w):
    B, M, _ = mem.shape
    k = (mem @ xk_w).reshape(B, M, X_HEADS, X_HEAD_DIM)
    v = (mem @ xv_w).reshape(B, M, X_HEADS, X_HEAD_DIM)
    return k, v


def cross_attn(x, k, v, xq_w, xo_w):
    B, T, _ = x.shape
    q = (x @ xq_w).reshape(B, T, X_HEADS, X_HEAD_DIM)
    s = jnp.einsum("bthd,bmhd->bhtm", q, k.astype(q.dtype)).astype(jnp.float32) * (X_HEAD_DIM ** -0.5)
    a = jax.nn.softmax(s, axis=-1).astype(x.dtype)
    o = jnp.einsum("bhtm,bmhd->bthd", a, v.astype(x.dtype)).reshape(B, T, D_MODEL)
    return o @ xo_w


def decoder_layer(x, conv_ctx, pool_ctx, mem_k, mem_v, pos0,
                  w_in, b_in, conv_w, conv_b, gn_g, gn_b, pool_w, pool_scale, w_out,
                  ln1_g, ln1_b, xq_w, xo_w, ln2_g, ln2_b, w1, b1, w2, b2, ln3_g, ln3_b):
    h = x @ w_in + b_in
    a, g, p = jnp.split(h, [D_CONV, 2 * D_CONV], axis=-1)
    u = a * jax.nn.sigmoid(g)
    o_conv, conv_new = conv_branch(u, conv_ctx, conv_w, conv_b, gn_g, gn_b)
    o_pool, pool_new = pool_branch(p, pool_ctx, pos0, pool_w, pool_scale)
    mix = jnp.concatenate([o_conv, o_pool], axis=-1) @ w_out
    x = layer_norm(DN_ALPHA * x + mix, ln1_g, ln1_b)
    x = layer_norm(DN_ALPHA * x + cross_attn(x, mem_k, mem_v, xq_w, xo_w), ln2_g, ln2_b)
    f = jnp.square(jax.nn.relu(x @ w1 + b1)) @ w2 + b2
    x = layer_norm(DN_ALPHA * x + f, ln3_g, ln3_b)
    return x, conv_new, pool_new


def setup_inputs(seed: int = 0) -> dict:
    key = jax.random.key(seed)
    ks = iter(jax.random.split(key, 40))
    nrm = lambda shape, scale: jax.random.normal(next(ks), shape, jnp.float32) * scale
    d_in = 2 * D_CONV + D_POOL
    return {
        "x_prompt": nrm((BATCH, SEQ, D_MODEL), 1.0),
        "x_sample": nrm((DEC_BATCH, DEC_SEQ, D_MODEL), 1.0),
        "mem_prompt": nrm((BATCH, N_MEM, D_MODEL), 1.0),
        "cache_mem_k": nrm((DEPTH, DEC_BATCH, N_MEM, X_HEADS, X_HEAD_DIM), 1.0),
        "cache_mem_v": nrm((DEPTH, DEC_BATCH, N_MEM, X_HEADS, X_HEAD_DIM), DN_BETA),
        "state_conv": nrm((DEPTH, DEC_BATCH, CONV_CTX, D_CONV), 0.5),
        "state_pool": nrm((DEPTH, DEC_BATCH, POOL_CTX, D_POOL), 1.0),
        "w_in": nrm((DEPTH, D_MODEL, d_in), D_MODEL ** -0.5),
        "b_in": nrm((DEPTH, d_in), 0.02),
        "conv_w": nrm((DEPTH, CONV_WIDTH, D_CONV), CONV_WIDTH ** -0.5),
        "conv_b": nrm((DEPTH, D_CONV), 0.02),
        "gn_g": 1.0 + nrm((DEPTH, D_CONV), 0.02),
        "gn_b": nrm((DEPTH, D_CONV), 0.02),
        "pool_w": nrm((DEPTH, N_POOL_GROUPS, POOL_GROUP, POOL_GROUP), POOL_GROUP ** -0.5),
        "pool_scale": 1.0 + nrm((DEPTH, D_POOL), 0.1),
        "w_out": nrm((DEPTH, D_MIX, D_MODEL), D_MIX ** -0.5 * DN_BETA),
        "ln1_g": 1.0 + nrm((DEPTH, D_MODEL), 0.02),
        "ln1_b": nrm((DEPTH, D_MODEL), 0.02),
        "xq_w": nrm((DEPTH, D_MODEL, D_MODEL), D_MODEL ** -0.5),
        "xk_w": nrm((DEPTH, D_MODEL, D_MODEL), D_MODEL ** -0.5),
        "xv_w": nrm((DEPTH, D_MODEL, D_MODEL), D_MODEL ** -0.5 * DN_BETA),
        "xo_w": nrm((DEPTH, D_MODEL, D_MODEL), D_MODEL ** -0.5 * DN_BETA),
        "ln2_g": 1.0 + nrm((DEPTH, D_MODEL), 0.02),
        "ln2_b": nrm((DEPTH, D_MODEL), 0.02),
        "w1": nrm((DEPTH, D_MODEL, D_FF), D_MODEL ** -0.5),
        "b1": nrm((DEPTH, D_FF), 0.02),
        "w2": nrm((DEPTH, D_FF, D_MODEL), D_FF ** -0.5 * DN_BETA),
        "b2": nrm((DEPTH, D_MODEL), 0.02),
        "ln3_g": 1.0 + nrm((DEPTH, D_MODEL), 0.02),
        "ln3_b": nrm((DEPTH, D_MODEL), 0.02),
    }


def reference(x_prompt, x_sample, mem_prompt, cache_mem_k, cache_mem_v, state_conv, state_pool,
              w_in, b_in, conv_w, conv_b, gn_g, gn_b, pool_w, pool_scale, w_out,
              ln1_g, ln1_b, xq_w, xk_w, xv_w, xo_w, ln2_g, ln2_b, w1, b1, w2, b2, ln3_g, ln3_b):
    n_prompt = x_prompt.shape[0]
    hp, hs = x_prompt, x_sample
    mk_p, mv_p, conv_p, conv_s, pool_p, pool_s = [], [], [], [], [], []
    for l in range(DEPTH):
        lw = (w_in[l], b_in[l], conv_w[l], conv_b[l], gn_g[l], gn_b[l], pool_w[l], pool_scale[l],
              w_out[l], ln1_g[l], ln1_b[l], xq_w[l], xo_w[l], ln2_g[l], ln2_b[l],
              w1[l], b1[l], w2[l], b2[l], ln3_g[l], ln3_b[l])
        k_p, v_p = mem_kv(mem_prompt, xk_w[l], xv_w[l])
        hp, c_p, q_p = decoder_layer(
            hp, jnp.zeros((n_prompt, CONV_CTX, D_CONV), hp.dtype),
            jnp.zeros((n_prompt, POOL_CTX, D_POOL), hp.dtype), k_p, v_p, 0, *lw)
        hs, c_s, q_s = decoder_layer(
            hs, state_conv[l], state_pool[l], cache_mem_k[l], cache_mem_v[l], PAST_LEN, *lw)
        mk_p.append(k_p)
        mv_p.append(v_p)
        conv_p.append(c_p)
        conv_s.append(c_s)
        pool_p.append(q_p)
        pool_s.append(q_s)
    return (hp, hs, jnp.stack(mk_p), jnp.stack(mv_p), jnp.stack(conv_p), jnp.stack(conv_s),
            jnp.stack(pool_p), jnp.stack(pool_s))
```

```python
import functools

import jax
import jax.numpy as jnp
from jax import lax
from jax.experimental import pallas as pl
from jax.experimental.pallas import tpu as pltpu

F32 = jnp.float32
BF16 = jnp.bfloat16

D_MODEL = 1024
D_CONV = 512
D_POOL = 512
D_IN = 2 * D_CONV + D_POOL
CONV_WIDTH = 31
CONV_CTX = CONV_WIDTH - 1
CONV_HEADS = 8
CONV_HEAD_DIM = D_CONV // CONV_HEADS
POOL_WINDOWS = (2, 4, 8, 16)
POOL_GROUP = D_POOL // len(POOL_WINDOWS)
POOL_CTX = max(POOL_WINDOWS) - 1
N_MEM = 256
X_HEADS = 4
X_HEAD_DIM = D_MODEL // X_HEADS
D_FF = 4 * D_MODEL
LN_EPS = 1e-5
DEPTH = 1
PAST_LEN = 16384
DN_ALPHA = (2.0 * DEPTH) ** 0.25
ATTN_SCALE = X_HEAD_DIM ** -0.5

CONV_PAD = 32
POOL_PAD = 16
GN_HALF = 256

PROMPT_BLOCK = 256
CONV_ROWS = 32
SAMPLE_MIX_BATCH = 16
SAMPLE_ATTN_BATCH = 4
SAMPLE_TAIL_ROWS = 256
VMEM_LIMIT = 56 * 1024 * 1024


def _bdot(a, w):
    return jnp.dot(a.astype(BF16), w, preferred_element_type=F32)


def _layer_norm(x, g, b):
    mu = jnp.mean(x, axis=-1, keepdims=True)
    d = x - mu
    var = jnp.mean(d * d, axis=-1, keepdims=True)
    return d * lax.rsqrt(var + LN_EPS) * g + b


def _group_norm_swish(y, gmat, gn_g, gn_b):
    inv = 1.0 / CONV_HEAD_DIM
    mu = _bdot(y, gmat) * inv
    d = y - mu
    var = _bdot(d * d, gmat) * inv
    n = d * lax.rsqrt(var + LN_EPS) * gn_g + gn_b
    return n * jax.nn.sigmoid(n)


def _tail(x1, o_bf16, xo_w, ln2_g, ln2_b, w1, b1, w2, b2, ln3_g, ln3_b):
    attn = jnp.dot(o_bf16, xo_w, preferred_element_type=F32)
    x2 = _layer_norm(DN_ALPHA * x1 + attn, ln2_g, ln2_b)
    hdn = jnp.maximum(_bdot(x2, w1) + b1, 0.0)
    f = _bdot(hdn * hdn, w2) + b2
    return _layer_norm(DN_ALPHA * x2 + f, ln3_g, ln3_b)


def _kv_proj_kernel(mem_ref, xk_ref, xv_ref, k_ref, v_ref, kt_ref, vb_ref):
    m = mem_ref[0].astype(BF16)
    k = jnp.dot(m, xk_ref[...], preferred_element_type=F32)
    v = jnp.dot(m, xv_ref[...], preferred_element_type=F32)
    k_ref[0] = k
    v_ref[0] = v
    kt_ref[0] = k.T.astype(BF16)
    vb_ref[0] = v.astype(BF16)


def _const_spec(shape):
    nd = len(shape)
    return pl.BlockSpec(shape, lambda *_: (0,) * nd, pipeline_mode=pl.Buffered(1))


def _kv_proj_call(mem, xk_w, xv_w):
    nb = mem.shape[0]
    blk = lambda s: pl.BlockSpec((1,) + s, lambda b: (b, 0, 0))
    return pl.pallas_call(
        _kv_proj_kernel,
        grid=(nb,),
        in_specs=[blk((N_MEM, D_MODEL)), _const_spec((D_MODEL, D_MODEL)), _const_spec((D_MODEL, D_MODEL))],
        out_specs=[blk((N_MEM, D_MODEL)), blk((N_MEM, D_MODEL)), blk((D_MODEL, N_MEM)), blk((N_MEM, D_MODEL))],
        out_shape=[
            jax.ShapeDtypeStruct((nb, N_MEM, D_MODEL), F32),
            jax.ShapeDtypeStruct((nb, N_MEM, D_MODEL), F32),
            jax.ShapeDtypeStruct((nb, D_MODEL, N_MEM), BF16),
            jax.ShapeDtypeStruct((nb, N_MEM, D_MODEL), BF16),
        ],
        compiler_params=pltpu.CompilerParams(dimension_semantics=("arbitrary",), vmem_limit_bytes=VMEM_LIMIT),
        name="kv_proj",
    )(mem, xk_w, xv_w)


def _prompt_kernel(x_ref, kt_ref, v_ref,
                   w_in_ref, b_in_ref, conv_w_ref, conv_b_ref, gn_g_ref, gn_b_ref, gmat_ref,
                   pool_w_ref, pool_scale_ref, w_out_ref, ln1_g_ref, ln1_b_ref,
                   xq_ref, xo_ref, ln2_g_ref, ln2_b_ref, w1_ref, b1_ref, w2_ref, b2_ref,
                   ln3_g_ref, ln3_b_ref,
                   y_ref, conv_new_ref, pool_new_ref,
                   ubuf, pbuf, ybuf, mixbuf, obuf):
    tm = PROMPT_BLOCK
    t = pl.program_id(1)
    nt = pl.num_programs(1)

    @pl.when(t == 0)
    def _():
        ubuf[0:CONV_PAD, :] = jnp.zeros((CONV_PAD, D_CONV), F32)
        pbuf[0:POOL_PAD, :] = jnp.zeros((POOL_PAD, D_POOL), F32)

    @pl.when(t > 0)
    def _():
        ubuf[0:CONV_PAD, :] = ubuf[tm:tm + CONV_PAD, :]
        pbuf[0:POOL_PAD, :] = pbuf[tm:tm + POOL_PAD, :]

    x = x_ref[0]
    h = _bdot(x, w_in_ref[...]) + b_in_ref[...]
    ubuf[CONV_PAD:CONV_PAD + tm, :] = h[:, 0:D_CONV] * jax.nn.sigmoid(h[:, D_CONV:2 * D_CONV])
    pbuf[POOL_PAD:POOL_PAD + tm, :] = h[:, 2 * D_CONV:]

    @pl.when(t == nt - 1)
    def _():
        conv_new_ref[0] = ubuf[CONV_PAD + tm - CONV_CTX:CONV_PAD + tm, :]
        pool_new_ref[0] = pbuf[POOL_PAD + tm - POOL_CTX:POOL_PAD + tm, :]

    base = CONV_PAD - CONV_CTX
    for c in range(tm // CONV_ROWS):
        r0 = c * CONV_ROWS
        acc = jnp.broadcast_to(conv_b_ref[...], (CONV_ROWS, D_CONV))
        for k in range(CONV_WIDTH):
            acc = acc + ubuf[base + r0 + k:base + r0 + k + CONV_ROWS, :] * conv_w_ref[k:k + 1, :]
        ybuf[r0:r0 + CONV_ROWS, :] = acc

    for j in range(D_CONV // GN_HALF):
        sl = slice(j * GN_HALF, (j + 1) * GN_HALF)
        o = _group_norm_swish(ybuf[:, sl], gmat_ref[...], gn_g_ref[:, sl], gn_b_ref[:, sl])
        mixbuf[:, sl] = o.astype(BF16)

    pos = t * tm + lax.broadcasted_iota(jnp.int32, (tm, POOL_GROUP), 0)
    for g, w in enumerate(POOL_WINDOWS):
        sl = slice(g * POOL_GROUP, (g + 1) * POOL_GROUP)
        cur = pbuf[POOL_PAD:POOL_PAD + tm, sl]
        ws = cur
        for j in range(1, w):
            ws = ws + pbuf[POOL_PAD - j:POOL_PAD - j + tm, sl]
        cnt = jnp.minimum(pos + 1, w).astype(F32)
        delta = ws / cnt - cur
        mixed = _bdot(delta, pool_w_ref[g]) * pool_scale_ref[:, sl]
        mixbuf[:, D_CONV + g * POOL_GROUP:D_CONV + (g + 1) * POOL_GROUP] = mixed.astype(BF16)

    mix = jnp.dot(mixbuf[...], w_out_ref[...], preferred_element_type=F32)
    x1 = _layer_norm(DN_ALPHA * x + mix, ln1_g_ref[...], ln1_b_ref[...])

    q = (_bdot(x1, xq_ref[...]) * ATTN_SCALE).astype(BF16)
    for hd in range(X_HEADS):
        sl = slice(hd * X_HEAD_DIM, (hd + 1) * X_HEAD_DIM)
        s = jnp.dot(q[:, sl], kt_ref[0, sl, :], preferred_element_type=F32)
        e = jnp.exp(s - jnp.max(s, axis=-1, keepdims=True))
        a = e / jnp.sum(e, axis=-1, keepdims=True)
        obuf[:, sl] = jnp.dot(a.astype(BF16), v_ref[0, :, sl], preferred_element_type=F32).astype(BF16)

    y_ref[0] = _tail(x1, obuf[...], xo_ref[...], ln2_g_ref[...], ln2_b_ref[...],
                     w1_ref[...], b1_ref[...], w2_ref[...], b2_ref[...],
                     ln3_g_ref[...], ln3_b_ref[...])


def _prompt_call(x, kt, vb, weights):
    nb, seq, _ = x.shape
    tm = PROMPT_BLOCK
    in_specs = [
        pl.BlockSpec((1, tm, D_MODEL), lambda b, t: (b, t, 0)),
        pl.BlockSpec((1, D_MODEL, N_MEM), lambda b, t: (b, 0, 0)),
        pl.BlockSpec((1, N_MEM, D_MODEL), lambda b, t: (b, 0, 0)),
    ] + [_const_spec(w.shape) for w in weights]
    out_specs = [
        pl.BlockSpec((1, tm, D_MODEL), lambda b, t: (b, t, 0)),
        pl.BlockSpec((1, CONV_CTX, D_CONV), lambda b, t: (b, 0, 0)),
        pl.BlockSpec((1, POOL_CTX, D_POOL), lambda b, t: (b, 0, 0)),
    ]
    out_shape = [
        jax.ShapeDtypeStruct((nb, seq, D_MODEL), F32),
        jax.ShapeDtypeStruct((nb, CONV_CTX, D_CONV), F32),
        jax.ShapeDtypeStruct((nb, POOL_CTX, D_POOL), F32),
    ]
    scratch = [
        pltpu.VMEM((CONV_PAD + tm, D_CONV), F32),
        pltpu.VMEM((POOL_PAD + tm, D_POOL), F32),
        pltpu.VMEM((tm, D_CONV), F32),
        pltpu.VMEM((tm, D_MODEL), BF16),
        pltpu.VMEM((tm, D_MODEL), BF16),
    ]
    return pl.pallas_call(
        _prompt_kernel,
        grid=(nb, seq // tm),
        in_specs=in_specs,
        out_specs=out_specs,
        out_shape=out_shape,
        scratch_shapes=scratch,
        compiler_params=pltpu.CompilerParams(
            dimension_semantics=("arbitrary", "arbitrary"), vmem_limit_bytes=VMEM_LIMIT),
        name="prompt_layer",
    )(x, kt, vb, *weights)


def _sample_mixer_kernel(x_ref, sconv_ref, spool_ref,
                         w_in_ref, b_in_ref, conv_w_ref, conv_b_ref, gn_g_ref, gn_b_ref, gmat_ref,
                         pool_w_ref, pool_scale_ref, w_out_ref, ln1_g_ref, ln1_b_ref, xq_ref,
                         x1_ref, q8_ref, conv_new_ref, pool_new_ref,
                         ebuf, pbuf, hbuf, ybuf, dbuf, mixbuf, *, dec_seq):
    gb = SAMPLE_MIX_BATCH
    ts = dec_seq
    x = x_ref[...]
    h = _bdot(x, w_in_ref[...]) + b_in_ref[...]
    hbuf[:, 0:D_CONV] = h[:, 0:D_CONV] * jax.nn.sigmoid(h[:, D_CONV:2 * D_CONV])
    hbuf[:, D_CONV:] = h[:, 2 * D_CONV:]

    pos = PAST_LEN + lax.broadcasted_iota(jnp.int32, (ts, POOL_GROUP), 0)
    for b in range(gb):
        rows = slice(b * ts, (b + 1) * ts)
        ebuf[0:CONV_CTX, :] = sconv_ref[b]
        ebuf[CONV_CTX:CONV_CTX + ts, :] = hbuf[rows, 0:D_CONV]
        pbuf[0:POOL_CTX, :] = spool_ref[b]
        pbuf[POOL_CTX:POOL_CTX + ts, :] = hbuf[rows, D_CONV:]
        conv_new_ref[b] = ebuf[ts:ts + CONV_CTX, :]
        pool_new_ref[b] = pbuf[ts:ts + POOL_CTX, :]

        acc = jnp.broadcast_to(conv_b_ref[...], (ts, D_CONV))
        for k in range(CONV_WIDTH):
            acc = acc + ebuf[k:k + ts, :] * conv_w_ref[k:k + 1, :]
        ybuf[rows, :] = acc

        for g, w in enumerate(POOL_WINDOWS):
            sl = slice(g * POOL_GROUP, (g + 1) * POOL_GROUP)
            cur = pbuf[POOL_CTX:POOL_CTX + ts, sl]
            ws = cur
            for j in range(1, w):
                ws = ws + pbuf[POOL_CTX - j:POOL_CTX - j + ts, sl]
            cnt = jnp.minimum(pos + 1, w).astype(F32)
            dbuf[rows, sl] = ws / cnt - cur

    for j in range(D_CONV // GN_HALF):
        sl = slice(j * GN_HALF, (j + 1) * GN_HALF)
        o = _group_norm_swish(ybuf[:, sl], gmat_ref[...], gn_g_ref[:, sl], gn_b_ref[:, sl])
        mixbuf[:, sl] = o.astype(BF16)
    for g in range(len(POOL_WINDOWS)):
        sl = slice(g * POOL_GROUP, (g + 1) * POOL_GROUP)
        mixed = _bdot(dbuf[:, sl], pool_w_ref[g]) * pool_scale_ref[:, sl]
        mixbuf[:, D_CONV + g * POOL_GROUP:D_CONV + (g + 1) * POOL_GROUP] = mixed.astype(BF16)

    mix = jnp.dot(mixbuf[...], w_out_ref[...], preferred_element_type=F32)
    x1 = _layer_norm(DN_ALPHA * x + mix, ln1_g_ref[...], ln1_b_ref[...])
    x1_ref[...] = x1
    q = _bdot(x1, xq_ref[...]) * ATTN_SCALE
    for b in range(gb):
        q8_ref[b, 0:ts, :] = q[b * ts:(b + 1) * ts, :]
        q8_ref[b, ts:8, :] = jnp.zeros((8 - ts, D_MODEL), F32)


def _sample_mixer_call(xs, sconv, spool, weights, dec_seq):
    n_tok = xs.shape[0]
    nb = n_tok // dec_seq
    gb = SAMPLE_MIX_BATCH
    rows = gb * dec_seq
    in_specs = [
        pl.BlockSpec((rows, D_MODEL), lambda i: (i, 0)),
        pl.BlockSpec((gb, CONV_CTX, D_CONV), lambda i: (i, 0, 0)),
        pl.BlockSpec((gb, POOL_CTX, D_POOL), lambda i: (i, 0, 0)),
    ] + [_const_spec(w.shape) for w in weights]
    out_specs = [
        pl.BlockSpec((rows, D_MODEL), lambda i: (i, 0)),
        pl.BlockSpec((gb, 8, D_MODEL), lambda i: (i, 0, 0)),
        pl.BlockSpec((gb, CONV_CTX, D_CONV), lambda i: (i, 0, 0)),
        pl.BlockSpec((gb, POOL_CTX, D_POOL), lambda i: (i, 0, 0)),
    ]
    out_shape = [
        jax.ShapeDtypeStruct((n_tok, D_MODEL), F32),
        jax.ShapeDtypeStruct((nb, 8, D_MODEL), F32),
        jax.ShapeDtypeStruct((nb, CONV_CTX, D_CONV), F32),
        jax.ShapeDtypeStruct((nb, POOL_CTX, D_POOL), F32),
    ]
    scratch = [
        pltpu.VMEM((CONV_CTX + 8 + 2, D_CONV), F32),
        pltpu.VMEM((POOL_CTX + 8 + 1, D_POOL), F32),
        pltpu.VMEM((rows, D_CONV + D_POOL), F32),
        pltpu.VMEM((rows, D_CONV), F32),
        pltpu.VMEM((rows, D_POOL), F32),
        pltpu.VMEM((rows, D_MODEL), BF16),
    ]
    return pl.pallas_call(
        functools.partial(_sample_mixer_kernel, dec_seq=dec_seq),
        grid=(nb // gb,),
        in_specs=in_specs,
        out_specs=out_specs,
        out_shape=out_shape,
        scratch_shapes=scratch,
        compiler_params=pltpu.CompilerParams(dimension_semantics=("arbitrary",), vmem_limit_bytes=VMEM_LIMIT),
        name="sample_mixer",
    )(xs, sconv, spool, *weights)


def _sample_attn_kernel(q8_ref, k_ref, v_ref, o_ref, *, dec_seq):
    gb = SAMPLE_ATTN_BATCH
    ts = dec_seq
    col_head = lax.broadcasted_iota(jnp.int32, (8, D_MODEL), 1) // X_HEAD_DIM
    for b in range(gb):
        q8 = q8_ref[b]
        qh = jnp.concatenate([jnp.where(col_head == hd, q8, 0.0) for hd in range(X_HEADS)], axis=0)
        s = lax.dot_general(qh.astype(BF16), k_ref[b].astype(BF16), (((1,), (1,)), ((), ())),
                            preferred_element_type=F32)
        e = jnp.exp(s - jnp.max(s, axis=-1, keepdims=True))
        a = e / jnp.sum(e, axis=-1, keepdims=True)
        of = jnp.dot(a.astype(BF16), v_ref[b].astype(BF16), preferred_element_type=F32)
        o8 = jnp.zeros((8, D_MODEL), F32)
        for hd in range(X_HEADS):
            o8 = o8 + jnp.where(col_head == hd, of[hd * 8:(hd + 1) * 8, :], 0.0)
        o_ref[b * ts:(b + 1) * ts, :] = o8[0:ts, :]


def _sample_attn_call(q8, k, v, dec_seq):
    nb = q8.shape[0]
    gb = SAMPLE_ATTN_BATCH
    return pl.pallas_call(
        functools.partial(_sample_attn_kernel, dec_seq=dec_seq),
        grid=(nb // gb,),
        in_specs=[
            pl.BlockSpec((gb, 8, D_MODEL), lambda i: (i, 0, 0)),
            pl.BlockSpec((gb, N_MEM, D_MODEL), lambda i: (i, 0, 0)),
            pl.BlockSpec((gb, N_MEM, D_MODEL), lambda i: (i, 0, 0)),
        ],
        out_specs=pl.BlockSpec((gb * dec_seq, D_MODEL), lambda i: (i, 0)),
        out_shape=jax.ShapeDtypeStruct((nb * dec_seq, D_MODEL), F32),
        compiler_params=pltpu.CompilerParams(dimension_semantics=("arbitrary",), vmem_limit_bytes=VMEM_LIMIT),
        name="sample_attn",
    )(q8, k, v)


def _sample_tail_kernel(x1_ref, o_ref, xo_ref, ln2_g_ref, ln2_b_ref, w1_ref, b1_ref, w2_ref, b2_ref,
                        ln3_g_ref, ln3_b_ref, y_ref):
    y_ref[...] = _tail(x1_ref[...], o_ref[...].astype(BF16), xo_ref[...], ln2_g_ref[...], ln2_b_ref[...],
                       w1_ref[...], b1_ref[...], w2_ref[...], b2_ref[...],
                       ln3_g_ref[...], ln3_b_ref[...])


def _sample_tail_call(x1, o, weights):
    n_tok = x1.shape[0]
    rows = SAMPLE_TAIL_ROWS
    row_spec = pl.BlockSpec((rows, D_MODEL), lambda i: (i, 0))
    return pl.pallas_call(
        _sample_tail_kernel,
        grid=(n_tok // rows,),
        in_specs=[row_spec, row_spec] + [_const_spec(w.shape) for w in weights],
        out_specs=row_spec,
        out_shape=jax.ShapeDtypeStruct((n_tok, D_MODEL), F32),
        compiler_params=pltpu.CompilerParams(dimension_semantics=("arbitrary",), vmem_limit_bytes=VMEM_LIMIT),
        name="sample_tail",
    )(x1, o, *weights)


def _group_indicator():
    gid = jnp.arange(GN_HALF) // CONV_HEAD_DIM
    return (gid[:, None] == gid[None, :]).astype(BF16)


def kernel(x_prompt, x_sample, mem_prompt, cache_mem_k, cache_mem_v, state_conv, state_pool, w_in, b_in, conv_w, conv_b, gn_g, gn_b, pool_w, pool_scale, w_out, ln1_g, ln1_b, xq_w, xk_w, xv_w, xo_w, ln2_g, ln2_b, w1, b1, w2, b2, ln3_g, ln3_b):
    assert w_in.shape[0] == DEPTH == 1
    n_prompt, seq, _ = x_prompt.shape
    n_dec, dec_seq, _ = x_sample.shape
    assert seq % PROMPT_BLOCK == 0 and n_dec % SAMPLE_MIX_BATCH == 0 and n_dec % SAMPLE_ATTN_BATCH == 0
    assert dec_seq <= 8 and (n_dec * dec_seq) % SAMPLE_TAIL_ROWS == 0

    row = lambda a: a[0].reshape(1, -1)
    mixer_w = (w_in[0].astype(BF16), row(b_in), conv_w[0], row(conv_b), row(gn_g), row(gn_b),
               _group_indicator(), pool_w[0].astype(BF16), row(pool_scale), w_out[0].astype(BF16),
               row(ln1_g), row(ln1_b), xq_w[0].astype(BF16))
    tail_w = (xo_w[0].astype(BF16), row(ln2_g), row(ln2_b), w1[0].astype(BF16), row(b1),
              w2[0].astype(BF16), row(b2), row(ln3_g), row(ln3_b))

    k_p, v_p, kt_p, vb_p = _kv_proj_call(mem_prompt, xk_w[0].astype(BF16), xv_w[0].astype(BF16))
    y_p, conv_p, pool_p = _prompt_call(x_prompt, kt_p, vb_p, mixer_w + tail_w)

    xs = x_sample.reshape(n_dec * dec_seq, D_MODEL)
    x1_s, q8_s, conv_s, pool_s = _sample_mixer_call(xs, state_conv[0], state_pool[0], mixer_w, dec_seq)
    o_s = _sample_attn_call(q8_s, cache_mem_k[0].reshape(n_dec, N_MEM, D_MODEL),
                            cache_mem_v[0].reshape(n_dec, N_MEM, D_MODEL), dec_seq)
    y_s = _sample_tail_call(x1_s, o_s, tail_w)

    kv_shape = (DEPTH, n_prompt, N_MEM, X_HEADS, X_HEAD_DIM)
    return (y_p, y_s.reshape(n_dec, dec_seq, D_MODEL), k_p.reshape(kv_shape), v_p.reshape(kv_shape),
            conv_p[None], conv_s[None], pool_p[None], pool_s[None])
```

```python
import functools

import jax
import jax.numpy as jnp
from jax import lax
from jax.experimental import pallas as pl
from jax.experimental.pallas import tpu as pltpu

F32 = jnp.float32
BF16 = jnp.bfloat16
U32 = jnp.uint32

LANES = 128
D_MODEL = 1024
D_CONV = 512
D_POOL = 512
D_IN = 2 * D_CONV + D_POOL
CONV_WIDTH = 31
CONV_CTX = CONV_WIDTH - 1
CONV_HEADS = 8
CONV_HEAD_DIM = D_CONV // CONV_HEADS
POOL_WINDOWS = (2, 4, 8, 16)
POOL_GROUP = D_POOL // len(POOL_WINDOWS)
POOL_CTX = max(POOL_WINDOWS) - 1
N_MEM = 256
X_HEADS = 4
X_HEAD_DIM = D_MODEL // X_HEADS
HEAD_CHUNKS = X_HEAD_DIM // LANES
KV_ROWS = N_MEM * X_HEADS * HEAD_CHUNKS
KV_ROW_STRIDE = X_HEADS * HEAD_CHUNKS
D_FF = 4 * D_MODEL
LN_EPS = 1e-5
DEPTH = 1
PAST_LEN = 16384
DN_ALPHA = (2.0 * DEPTH) ** 0.25
ATTN_SCALE = X_HEAD_DIM ** -0.5

assert POOL_GROUP == LANES and X_HEAD_DIM % LANES == 0

CONV_SLABS = D_CONV // LANES
POOL_SLABS = D_POOL // LANES
CONV_PAD = 32
POOL_PAD = 16
GN_HALF = 256

PROMPT_BLOCK = 256
CONV_ROWS = 32
SAMPLE_MIX_BATCH = 32
SAMPLE_ATTN_BATCH = 4
SAMPLE_TAIL_ROWS = 256
CAST_ROWS = 256
VMEM_LIMIT = 56 * 1024 * 1024


def _unpack(w_u32):
    return pltpu.bitcast(w_u32, BF16)


def _pack(w_bf16):
    return pltpu.bitcast(w_bf16, U32)


def _bdot(a, w):
    return jnp.dot(a.astype(BF16), w, preferred_element_type=F32)


def _layer_norm(x, g, b):
    mu = jnp.mean(x, axis=-1, keepdims=True)
    d = x - mu
    var = jnp.mean(d * d, axis=-1, keepdims=True)
    return d * lax.rsqrt(var + LN_EPS) * g + b


def _group_norm_swish(y, gmat, gn_g, gn_b):
    inv = 1.0 / CONV_HEAD_DIM
    mu = _bdot(y, gmat) * inv
    d = y - mu
    var = _bdot(d * d, gmat) * inv
    n = d * lax.rsqrt(var + LN_EPS) * gn_g + gn_b
    return n * jax.nn.sigmoid(n)


def _mixer_out(ybuf, dbuf, mixbuf, x, gmat_ref, gn_g_ref, gn_b_ref, pool_w_ref, pool_scale_ref,
               w_out_ref, ln1_g_ref, ln1_b_ref):
    gmat = _unpack(gmat_ref[...])
    for j in range(D_CONV // GN_HALF):
        sl = slice(j * GN_HALF, (j + 1) * GN_HALF)
        o = _group_norm_swish(ybuf[:, sl], gmat, gn_g_ref[:, sl], gn_b_ref[:, sl])
        mixbuf[:, sl] = o.astype(BF16)
    half = POOL_GROUP // 2
    for g in range(len(POOL_WINDOWS)):
        sl = slice(g * POOL_GROUP, (g + 1) * POOL_GROUP)
        pw = _unpack(pool_w_ref[g * half:(g + 1) * half, :])
        mixed = _bdot(dbuf[:, sl], pw) * pool_scale_ref[:, sl]
        mixbuf[:, D_CONV + g * POOL_GROUP:D_CONV + (g + 1) * POOL_GROUP] = mixed.astype(BF16)
    mix = jnp.dot(mixbuf[...], _unpack(w_out_ref[...]), preferred_element_type=F32)
    return _layer_norm(DN_ALPHA * x + mix, ln1_g_ref[...], ln1_b_ref[...])


def _tail(x1, o_bf16, xo_ref, ln2_g_ref, ln2_b_ref, w1_ref, b1_ref, w2_ref, b2_ref, ln3_g_ref, ln3_b_ref):
    attn = jnp.dot(o_bf16, _unpack(xo_ref[...]), preferred_element_type=F32)
    x2 = _layer_norm(DN_ALPHA * x1 + attn, ln2_g_ref[...], ln2_b_ref[...])
    hdn = jnp.maximum(_bdot(x2, _unpack(w1_ref[...])) + b1_ref[...], 0.0)
    f = _bdot(hdn * hdn, _unpack(w2_ref[...])) + b2_ref[...]
    return _layer_norm(DN_ALPHA * x2 + f, ln3_g_ref[...], ln3_b_ref[...])


def _const_spec(shape):
    nd = len(shape)
    return pl.BlockSpec(shape, lambda *_: (0,) * nd, pipeline_mode=pl.Buffered(1))


def _params(n_axes):
    return pltpu.CompilerParams(dimension_semantics=("arbitrary",) * n_axes, vmem_limit_bytes=VMEM_LIMIT)


def _cast_kernel(*refs):
    n = len(refs) // 2
    for src, dst in zip(refs[:n], refs[n:]):
        dst[...] = _pack(src[...].astype(BF16))


def _cast_call(ws, steps):
    in_specs, out_specs, out_shape = [], [], []
    for w in ws:
        k, n = w.shape
        rows = k // steps
        assert rows * steps == k and rows % 16 == 0
        in_specs.append(pl.BlockSpec((rows, n), lambda i: (i, 0)))
        out_specs.append(pl.BlockSpec((rows // 2, n), lambda i: (i, 0)))
        out_shape.append(jax.ShapeDtypeStruct((k // 2, n), U32))
    return pl.pallas_call(
        _cast_kernel, grid=(steps,), in_specs=in_specs, out_specs=out_specs, out_shape=out_shape,
        compiler_params=_params(1), name="cast_weights",
    )(*ws)


def _store_head_split(dst_ref, val):
    for hd in range(X_HEADS):
        for c in range(HEAD_CHUNKS):
            col = hd * X_HEAD_DIM + c * LANES
            dst_ref[0, pl.ds(c * X_HEADS + hd, N_MEM, stride=KV_ROW_STRIDE), :] = val[:, col:col + LANES]


def _kv_proj_kernel(mem_ref, xk_ref, xv_ref, k_ref, v_ref, kt_ref, vb_ref):
    m = mem_ref[0].astype(BF16)
    k = jnp.dot(m, _unpack(xk_ref[...]), preferred_element_type=F32)
    v = jnp.dot(m, _unpack(xv_ref[...]), preferred_element_type=F32)
    _store_head_split(k_ref, k)
    _store_head_split(v_ref, v)
    kt_ref[0] = _pack(k.T.astype(BF16))
    vb_ref[0] = _pack(v.astype(BF16))


def _kv_proj_call(mem, xk_w, xv_w):
    nb = mem.shape[0]
    blk = lambda s: pl.BlockSpec((1,) + s, lambda b: (b, 0, 0))
    return pl.pallas_call(
        _kv_proj_kernel,
        grid=(nb,),
        in_specs=[blk((N_MEM, D_MODEL)), _const_spec(xk_w.shape), _const_spec(xv_w.shape)],
        out_specs=[blk((KV_ROWS, LANES)), blk((KV_ROWS, LANES)),
                   blk((D_MODEL // 2, N_MEM)), blk((N_MEM // 2, D_MODEL))],
        out_shape=[
            jax.ShapeDtypeStruct((nb, KV_ROWS, LANES), F32),
            jax.ShapeDtypeStruct((nb, KV_ROWS, LANES), F32),
            jax.ShapeDtypeStruct((nb, D_MODEL // 2, N_MEM), U32),
            jax.ShapeDtypeStruct((nb, N_MEM // 2, D_MODEL), U32),
        ],
        compiler_params=_params(1),
        name="kv_proj",
    )(mem, xk_w, xv_w)


def _prompt_kernel(x_ref, kt_ref, v_ref,
                   w_in_ref, b_in_ref, conv_w_ref, conv_b_ref, gn_g_ref, gn_b_ref, gmat_ref,
                   pool_w_ref, pool_scale_ref, w_out_ref, ln1_g_ref, ln1_b_ref, xq_ref,
                   xo_ref, ln2_g_ref, ln2_b_ref, w1_ref, b1_ref, w2_ref, b2_ref, ln3_g_ref, ln3_b_ref,
                   y_ref, conv_new_ref, pool_new_ref,
                   ubuf, pbuf, ybuf, dbuf, mixbuf, obuf):
    tm = PROMPT_BLOCK
    t = pl.program_id(1)
    nt = pl.num_programs(1)

    @pl.when(t == 0)
    def _():
        ubuf[:, 0:CONV_PAD, :] = jnp.zeros((CONV_SLABS, CONV_PAD, LANES), F32)
        pbuf[:, 0:POOL_PAD, :] = jnp.zeros((POOL_SLABS, POOL_PAD, LANES), F32)

    @pl.when(t > 0)
    def _():
        ubuf[:, 0:CONV_PAD, :] = ubuf[:, tm:tm + CONV_PAD, :]
        pbuf[:, 0:POOL_PAD, :] = pbuf[:, tm:tm + POOL_PAD, :]

    x = x_ref[0]
    h = _bdot(x, _unpack(w_in_ref[...])) + b_in_ref[...]
    u = h[:, 0:D_CONV] * jax.nn.sigmoid(h[:, D_CONV:2 * D_CONV])
    for s in range(CONV_SLABS):
        ubuf[s, CONV_PAD:CONV_PAD + tm, :] = u[:, s * LANES:(s + 1) * LANES]
    for s in range(POOL_SLABS):
        pbuf[s, POOL_PAD:POOL_PAD + tm, :] = h[:, 2 * D_CONV + s * LANES:2 * D_CONV + (s + 1) * LANES]

    @pl.when(t == nt - 1)
    def _():
        for s in range(CONV_SLABS):
            conv_new_ref[0, :, s * LANES:(s + 1) * LANES] = ubuf[s, CONV_PAD + tm - CONV_CTX:CONV_PAD + tm, :]
        for s in range(POOL_SLABS):
            pool_new_ref[0, :, s * LANES:(s + 1) * LANES] = pbuf[s, POOL_PAD + tm - POOL_CTX:POOL_PAD + tm, :]

    base = CONV_PAD - CONV_CTX
    for s in range(CONV_SLABS):
        sl = slice(s * LANES, (s + 1) * LANES)
        for c in range(tm // CONV_ROWS):
            r0 = c * CONV_ROWS
            acc = jnp.broadcast_to(conv_b_ref[:, sl], (CONV_ROWS, LANES))
            for k in range(CONV_WIDTH):
                acc = acc + ubuf[s, base + r0 + k:base + r0 + k + CONV_ROWS, :] * conv_w_ref[k:k + 1, sl]
            ybuf[r0:r0 + CONV_ROWS, sl] = acc

    pos = t * tm + lax.broadcasted_iota(jnp.int32, (tm, POOL_GROUP), 0)
    for g, w in enumerate(POOL_WINDOWS):
        cur = pbuf[g, POOL_PAD:POOL_PAD + tm, :]
        ws = cur
        for j in range(1, w):
            ws = ws + pbuf[g, POOL_PAD - j:POOL_PAD - j + tm, :]
        cnt = jnp.minimum(pos + 1, w).astype(F32)
        dbuf[:, g * POOL_GROUP:(g + 1) * POOL_GROUP] = ws / cnt - cur

    x1 = _mixer_out(ybuf, dbuf, mixbuf, x, gmat_ref, gn_g_ref, gn_b_ref, pool_w_ref, pool_scale_ref,
                    w_out_ref, ln1_g_ref, ln1_b_ref)

    q = (_bdot(x1, _unpack(xq_ref[...])) * ATTN_SCALE).astype(BF16)
    for hd in range(X_HEADS):
        sl = slice(hd * X_HEAD_DIM, (hd + 1) * X_HEAD_DIM)
        kt_h = _unpack(kt_ref[0, hd * X_HEAD_DIM // 2:(hd + 1) * X_HEAD_DIM // 2, :])
        s = jnp.dot(q[:, sl], kt_h, preferred_element_type=F32)
        e = jnp.exp(s - jnp.max(s, axis=-1, keepdims=True))
        a = e / jnp.sum(e, axis=-1, keepdims=True)
        v_h = _unpack(v_ref[0, :, sl])
        obuf[:, sl] = jnp.dot(a.astype(BF16), v_h, preferred_element_type=F32).astype(BF16)

    y_ref[0] = _tail(x1, obuf[...], xo_ref, ln2_g_ref, ln2_b_ref, w1_ref, b1_ref, w2_ref, b2_ref,
                     ln3_g_ref, ln3_b_ref)


def _prompt_call(x, kt, vb, weights):
    nb, seq, _ = x.shape
    tm = PROMPT_BLOCK
    in_specs = [
        pl.BlockSpec((1, tm, D_MODEL), lambda b, t: (b, t, 0)),
        pl.BlockSpec((1, D_MODEL // 2, N_MEM), lambda b, t: (b, 0, 0)),
        pl.BlockSpec((1, N_MEM // 2, D_MODEL), lambda b, t: (b, 0, 0)),
    ] + [_const_spec(w.shape) for w in weights]
    out_specs = [
        pl.BlockSpec((1, tm, D_MODEL), lambda b, t: (b, t, 0)),
        pl.BlockSpec((1, CONV_CTX, D_CONV), lambda b, t: (b, 0, 0)),
        pl.BlockSpec((1, POOL_CTX, D_POOL), lambda b, t: (b, 0, 0)),
    ]
    out_shape = [
        jax.ShapeDtypeStruct((nb, seq, D_MODEL), F32),
        jax.ShapeDtypeStruct((nb, CONV_CTX, D_CONV), F32),
        jax.ShapeDtypeStruct((nb, POOL_CTX, D_POOL), F32),
    ]
    scratch = [
        pltpu.VMEM((CONV_SLABS, CONV_PAD + tm, LANES), F32),
        pltpu.VMEM((POOL_SLABS, POOL_PAD + tm, LANES), F32),
        pltpu.VMEM((tm, D_CONV), F32),
        pltpu.VMEM((tm, D_POOL), F32),
        pltpu.VMEM((tm, D_MODEL), BF16),
        pltpu.VMEM((tm, D_MODEL), BF16),
    ]
    return pl.pallas_call(
        _prompt_kernel,
        grid=(nb, seq // tm),
        in_specs=in_specs,
        out_specs=out_specs,
        out_shape=out_shape,
        scratch_shapes=scratch,
        compiler_params=_params(2),
        name="prompt_layer",
    )(x, kt, vb, *weights)


def _sample_mixer_kernel(x_ref, sconv_ref, spool_ref,
                         w_in_ref, b_in_ref, conv_w_ref, conv_b_ref, gn_g_ref, gn_b_ref, gmat_ref,
                         pool_w_ref, pool_scale_ref, w_out_ref, ln1_g_ref, ln1_b_ref, xq_ref,
                         x1_ref, q_ref, conv_new_ref, pool_new_ref,
                         hbuf, yslab, dslab, ybuf, dbuf, mixbuf, *, dec_seq):
    gb = SAMPLE_MIX_BATCH
    ts = dec_seq
    x = x_ref[...]
    h = _bdot(x, _unpack(w_in_ref[...])) + b_in_ref[...]
    u = h[:, 0:D_CONV] * jax.nn.sigmoid(h[:, D_CONV:2 * D_CONV])
    for s in range(CONV_SLABS):
        hbuf[s] = u[:, s * LANES:(s + 1) * LANES]
    for s in range(POOL_SLABS):
        hbuf[CONV_SLABS + s] = h[:, 2 * D_CONV + s * LANES:2 * D_CONV + (s + 1) * LANES]

    def step_rows(slab, t):
        return hbuf[slab, pl.ds(t, gb, stride=ts), :]

    for s in range(CONV_SLABS):
        sl = slice(s * LANES, (s + 1) * LANES)
        new = [step_rows(s, t) for t in range(ts)]
        ext = lambda j: sconv_ref[j, :, sl] if j < CONV_CTX else new[j - CONV_CTX]
        for j in range(CONV_CTX - ts):
            conv_new_ref[j, :, sl] = sconv_ref[j + ts, :, sl]
        for t in range(ts):
            conv_new_ref[CONV_CTX - ts + t, :, sl] = new[t]
            acc = jnp.broadcast_to(conv_b_ref[:, sl], (gb, LANES))
            for k in range(CONV_WIDTH):
                acc = acc + ext(t + k) * conv_w_ref[k:k + 1, sl]
            yslab[s, pl.ds(t, gb, stride=ts), :] = acc

    for g, w in enumerate(POOL_WINDOWS):
        sl = slice(g * LANES, (g + 1) * LANES)
        new = [step_rows(CONV_SLABS + g, t) for t in range(ts)]
        ext = lambda j: spool_ref[j, :, sl] if j < POOL_CTX else new[j - POOL_CTX]
        for j in range(POOL_CTX - ts):
            pool_new_ref[j, :, sl] = spool_ref[j + ts, :, sl]
        for t in range(ts):
            pool_new_ref[POOL_CTX - ts + t, :, sl] = new[t]
            ws = new[t]
            for j in range(1, w):
                ws = ws + ext(POOL_CTX + t - j)
            cnt = float(min(PAST_LEN + t + 1, w))
            dslab[g, pl.ds(t, gb, stride=ts), :] = ws / cnt - new[t]

    for s in range(CONV_SLABS):
        ybuf[:, s * LANES:(s + 1) * LANES] = yslab[s]
    for s in range(POOL_SLABS):
        dbuf[:, s * LANES:(s + 1) * LANES] = dslab[s]

    x1 = _mixer_out(ybuf, dbuf, mixbuf, x, gmat_ref, gn_g_ref, gn_b_ref, pool_w_ref, pool_scale_ref,
                    w_out_ref, ln1_g_ref, ln1_b_ref)
    x1_ref[...] = x1
    q_ref[...] = _bdot(x1, _unpack(xq_ref[...])) * ATTN_SCALE


def _sample_mixer_call(xs, sconv_tm, spool_tm, weights, dec_seq):
    n_tok = xs.shape[0]
    nb = n_tok // dec_seq
    gb = SAMPLE_MIX_BATCH
    rows = gb * dec_seq
    row_spec = pl.BlockSpec((rows, D_MODEL), lambda i: (i, 0))
    conv_spec = pl.BlockSpec((CONV_CTX, gb, D_CONV), lambda i: (0, i, 0))
    pool_spec = pl.BlockSpec((POOL_CTX, gb, D_POOL), lambda i: (0, i, 0))
    out_shape = [
        jax.ShapeDtypeStruct((n_tok, D_MODEL), F32),
        jax.ShapeDtypeStruct((n_tok, D_MODEL), F32),
        jax.ShapeDtypeStruct((CONV_CTX, nb, D_CONV), F32),
        jax.ShapeDtypeStruct((POOL_CTX, nb, D_POOL), F32),
    ]
    scratch = [
        pltpu.VMEM((CONV_SLABS + POOL_SLABS, rows, LANES), F32),
        pltpu.VMEM((CONV_SLABS, rows, LANES), F32),
        pltpu.VMEM((POOL_SLABS, rows, LANES), F32),
        pltpu.VMEM((rows, D_CONV), F32),
        pltpu.VMEM((rows, D_POOL), F32),
        pltpu.VMEM((rows, D_MODEL), BF16),
    ]
    return pl.pallas_call(
        functools.partial(_sample_mixer_kernel, dec_seq=dec_seq),
        grid=(nb // gb,),
        in_specs=[row_spec, conv_spec, pool_spec] + [_const_spec(w.shape) for w in weights],
        out_specs=[row_spec, row_spec, conv_spec, pool_spec],
        out_shape=out_shape,
        scratch_shapes=scratch,
        compiler_params=_params(1),
        name="sample_mixer",
    )(xs, sconv_tm, spool_tm, *weights)


def _load_head(ref, b, hd):
    parts = [ref[b, pl.ds(c * X_HEADS + hd, N_MEM, stride=KV_ROW_STRIDE), :] for c in range(HEAD_CHUNKS)]
    return jnp.concatenate(parts, axis=1).astype(BF16)


def _sample_attn_kernel(q_ref, k_ref, v_ref, o_ref, *, dec_seq):
    gb = SAMPLE_ATTN_BATCH
    ts = dec_seq
    q = q_ref[...].astype(BF16)
    nq = gb * ts
    heads = [(b, hd) for b in range(gb) for hd in range(X_HEADS)]
    s = jnp.concatenate(
        [lax.dot_general(q[:, hd * X_HEAD_DIM:(hd + 1) * X_HEAD_DIM], _load_head(k_ref, b, hd),
                         (((1,), (1,)), ((), ())), preferred_element_type=F32) for b, hd in heads], axis=0)
    e = jnp.exp(s - jnp.max(s, axis=-1, keepdims=True))
    a = (e / jnp.sum(e, axis=-1, keepdims=True)).astype(BF16)
    for i, (b, hd) in enumerate(heads):
        o = jnp.dot(a[i * nq:(i + 1) * nq, :], _load_head(v_ref, b, hd), preferred_element_type=F32)
        o_ref[b * ts:(b + 1) * ts, hd * X_HEAD_DIM:(hd + 1) * X_HEAD_DIM] = o[b * ts:(b + 1) * ts, :]


def _sample_attn_call(q, k, v, dec_seq):
    nb = k.shape[0]
    gb = SAMPLE_ATTN_BATCH
    row_spec = pl.BlockSpec((gb * dec_seq, D_MODEL), lambda i: (i, 0))
    kv_spec = pl.BlockSpec((gb, KV_ROWS, LANES), lambda i: (i, 0, 0))
    return pl.pallas_call(
        functools.partial(_sample_attn_kernel, dec_seq=dec_seq),
        grid=(nb // gb,),
        in_specs=[row_spec, kv_spec, kv_spec],
        out_specs=row_spec,
        out_shape=jax.ShapeDtypeStruct((nb * dec_seq, D_MODEL), F32),
        compiler_params=_params(1),
        name="sample_attn",
    )(q, k, v)


def _sample_tail_kernel(x1_ref, o_ref, xo_ref, ln2_g_ref, ln2_b_ref, w1_ref, b1_ref, w2_ref, b2_ref,
                        ln3_g_ref, ln3_b_ref, y_ref):
    y_ref[...] = _tail(x1_ref[...], o_ref[...].astype(BF16), xo_ref, ln2_g_ref, ln2_b_ref,
                       w1_ref, b1_ref, w2_ref, b2_ref, ln3_g_ref, ln3_b_ref)


def _sample_tail_call(x1, o, weights):
    n_tok = x1.shape[0]
    rows = SAMPLE_TAIL_ROWS
    row_spec = pl.BlockSpec((rows, D_MODEL), lambda i: (i, 0))
    return pl.pallas_call(
        _sample_tail_kernel,
        grid=(n_tok // rows,),
        in_specs=[row_spec, row_spec] + [_const_spec(w.shape) for w in weights],
        out_specs=row_spec,
        out_shape=jax.ShapeDtypeStruct((n_tok, D_MODEL), F32),
        compiler_params=_params(1),
        name="sample_tail",
    )(x1, o, *weights)


def _group_indicator():
    gid = jnp.arange(GN_HALF) // CONV_HEAD_DIM
    return (gid[:, None] == gid[None, :]).astype(F32)


def _to_head_split_view(a):
    lead = a.shape[:-3]
    n = len(lead)
    a = a.reshape(lead + (N_MEM, X_HEADS, HEAD_CHUNKS, LANES))
    a = a.transpose(tuple(range(n)) + (n, n + 2, n + 1, n + 3))
    return a.reshape(lead + (KV_ROWS, LANES))


def _from_head_split_view(a):
    lead = a.shape[:-2]
    n = len(lead)
    a = a.reshape(lead + (N_MEM, HEAD_CHUNKS, X_HEADS, LANES))
    a = a.transpose(tuple(range(n)) + (n, n + 2, n + 1, n + 3))
    return a.reshape(lead + (N_MEM, X_HEADS, X_HEAD_DIM))


def kernel(x_prompt, x_sample, mem_prompt, cache_mem_k, cache_mem_v, state_conv, state_pool, w_in, b_in, conv_w, conv_b, gn_g, gn_b, pool_w, pool_scale, w_out, ln1_g, ln1_b, xq_w, xk_w, xv_w, xo_w, ln2_g, ln2_b, w1, b1, w2, b2, ln3_g, ln3_b):
    assert w_in.shape[0] == DEPTH == 1
    n_prompt, seq, _ = x_prompt.shape
    n_dec, dec_seq, _ = x_sample.shape
    assert seq % PROMPT_BLOCK == 0 and n_dec % SAMPLE_MIX_BATCH == 0 and n_dec % SAMPLE_ATTN_BATCH == 0
    assert (n_dec * dec_seq) % SAMPLE_TAIL_ROWS == 0

    w_in_p, w_out_p, xq_p, xo_p, xk_p, xv_p, pool_w_p, gmat_p = _cast_call(
        [w_in[0], w_out[0], xq_w[0], xo_w[0], xk_w[0], xv_w[0],
         pool_w[0].reshape(len(POOL_WINDOWS) * POOL_GROUP, POOL_GROUP), _group_indicator()], steps=4)
    w1_p, w2_p = _cast_call([w1[0], w2[0]], steps=16)

    row = lambda a: a[0].reshape(1, -1)
    mixer_w = (w_in_p, row(b_in), conv_w[0], row(conv_b), row(gn_g), row(gn_b), gmat_p, pool_w_p,
               row(pool_scale), w_out_p, row(ln1_g), row(ln1_b), xq_p)
    tail_w = (xo_p, row(ln2_g), row(ln2_b), w1_p, row(b1), w2_p, row(b2), row(ln3_g), row(ln3_b))

    k_p, v_p, kt_p, vb_p = _kv_proj_call(mem_prompt, xk_p, xv_p)
    y_p, conv_p, pool_p = _prompt_call(x_prompt, kt_p, vb_p, mixer_w + tail_w)

    xs = x_sample.reshape(n_dec * dec_seq, D_MODEL)
    x1_s, q_s, conv_s, pool_s = _sample_mixer_call(
        xs, state_conv[0].transpose(1, 0, 2), state_pool[0].transpose(1, 0, 2), mixer_w, dec_seq)
    o_s = _sample_attn_call(q_s, _to_head_split_view(cache_mem_k[0]), _to_head_split_view(cache_mem_v[0]), dec_seq)
    y_s = _sample_tail_call(x1_s, o_s, tail_w)

    return (y_p, y_s.reshape(n_dec, dec_seq, D_MODEL),
            _from_head_split_view(k_p)[None], _from_head_split_view(v_p)[None],
            conv_p[None], conv_s.transpose(1, 0, 2)[None], pool_p[None], pool_s.transpose(1, 0, 2)[None])
```

```python
import functools

import jax
import jax.numpy as jnp
from jax import lax
from jax.experimental import pallas as pl
from jax.experimental.pallas import tpu as pltpu

F32 = jnp.float32
BF16 = jnp.bfloat16
U32 = jnp.uint32

LANES = 128
D_MODEL = 1024
D_CONV = 512
D_POOL = 512
D_IN = 2 * D_CONV + D_POOL
CONV_WIDTH = 31
CONV_CTX = CONV_WIDTH - 1
CONV_HEADS = 8
CONV_HEAD_DIM = D_CONV // CONV_HEADS
POOL_WINDOWS = (2, 4, 8, 16)
POOL_GROUP = D_POOL // len(POOL_WINDOWS)
POOL_CTX = max(POOL_WINDOWS) - 1
N_MEM = 256
X_HEADS = 4
X_HEAD_DIM = D_MODEL // X_HEADS
HEAD_CHUNKS = X_HEAD_DIM // LANES
KV_ROWS = N_MEM * X_HEADS * HEAD_CHUNKS
KV_ROW_STRIDE = X_HEADS * HEAD_CHUNKS
D_FF = 4 * D_MODEL
LN_EPS = 1e-5
DEPTH = 1
PAST_LEN = 16384
DN_ALPHA = (2.0 * DEPTH) ** 0.25
ATTN_SCALE = X_HEAD_DIM ** -0.5

assert POOL_GROUP == LANES and X_HEAD_DIM % LANES == 0

CONV_SLABS = D_CONV // LANES
POOL_SLABS = D_POOL // LANES
CONV_PAD = 32
POOL_PAD = 16
GN_HALF = 256

PROMPT_BLOCK = 256
CONV_ROWS = 32
SAMPLE_MIX_BATCH = 32
SAMPLE_ATTN_BATCH = 4
SAMPLE_TAIL_ROWS = 256
CAST_ROWS = 256
VMEM_LIMIT = 56 * 1024 * 1024


def _unpack(w_u32):
    return pltpu.bitcast(w_u32, BF16)


def _pack(w_bf16):
    return pltpu.bitcast(w_bf16, U32)


def _bdot(a, w):
    return jnp.dot(a.astype(BF16), w, preferred_element_type=F32)


def _layer_norm(x, g, b):
    mu = jnp.mean(x, axis=-1, keepdims=True)
    d = x - mu
    var = jnp.mean(d * d, axis=-1, keepdims=True)
    return d * lax.rsqrt(var + LN_EPS) * g + b


def _group_norm_swish(y, gmat, gn_g, gn_b):
    inv = 1.0 / CONV_HEAD_DIM
    mu = _bdot(y, gmat) * inv
    d = y - mu
    var = _bdot(d * d, gmat) * inv
    n = d * lax.rsqrt(var + LN_EPS) * gn_g + gn_b
    return n * jax.nn.sigmoid(n)


def _mixer_out(ybuf, dbuf, mixbuf, x, gmat_ref, gn_g_ref, gn_b_ref, pool_w_ref, pool_scale_ref, w_out_ref):
    gmat = _unpack(gmat_ref[...])
    for j in range(D_CONV // GN_HALF):
        sl = slice(j * GN_HALF, (j + 1) * GN_HALF)
        o = _group_norm_swish(ybuf[:, sl], gmat, gn_g_ref[:, sl], gn_b_ref[:, sl])
        mixbuf[:, sl] = o.astype(BF16)
    half = POOL_GROUP // 2
    for g in range(len(POOL_WINDOWS)):
        sl = slice(g * POOL_GROUP, (g + 1) * POOL_GROUP)
        pw = _unpack(pool_w_ref[g * half:(g + 1) * half, :])
        mixed = _bdot(dbuf[:, sl], pw) * pool_scale_ref[:, sl]
        mixbuf[:, D_CONV + g * POOL_GROUP:D_CONV + (g + 1) * POOL_GROUP] = mixed.astype(BF16)
    mix = jnp.dot(mixbuf[...], _unpack(w_out_ref[...]), preferred_element_type=F32)
    return DN_ALPHA * x + mix


def _tail(x1, o_bf16, xo_ref, ln2_g_ref, ln2_b_ref, w1_ref, b1_ref, w2_ref, b2_ref, ln3_g_ref, ln3_b_ref):
    attn = jnp.dot(o_bf16, _unpack(xo_ref[...]), preferred_element_type=F32)
    x2 = _layer_norm(DN_ALPHA * x1 + attn, ln2_g_ref[...], ln2_b_ref[...])
    hdn = jnp.maximum(_bdot(x2, _unpack(w1_ref[...])) + b1_ref[...], 0.0)
    f = _bdot(hdn * hdn, _unpack(w2_ref[...])) + b2_ref[...]
    return _layer_norm(DN_ALPHA * x2 + f, ln3_g_ref[...], ln3_b_ref[...])


def _const_spec(shape):
    nd = len(shape)
    return pl.BlockSpec(shape, lambda *_: (0,) * nd, pipeline_mode=pl.Buffered(1))


def _params(n_axes):
    return pltpu.CompilerParams(dimension_semantics=("arbitrary",) * n_axes, vmem_limit_bytes=VMEM_LIMIT)


def _cast_kernel(*refs):
    n = len(refs) // 2
    for src, dst in zip(refs[:n], refs[n:]):
        dst[...] = _pack(src[...].astype(BF16))


def _cast_call(ws, steps):
    in_specs, out_specs, out_shape = [], [], []
    for w in ws:
        k, n = w.shape
        rows = k // steps
        assert rows * steps == k and rows % 16 == 0
        in_specs.append(pl.BlockSpec((rows, n), lambda i: (i, 0)))
        out_specs.append(pl.BlockSpec((rows // 2, n), lambda i: (i, 0)))
        out_shape.append(jax.ShapeDtypeStruct((k // 2, n), U32))
    return pl.pallas_call(
        _cast_kernel, grid=(steps,), in_specs=in_specs, out_specs=out_specs, out_shape=out_shape,
        compiler_params=_params(1), name="cast_weights",
    )(*ws)


def _store_head_split(dst_ref, val):
    for hd in range(X_HEADS):
        for c in range(HEAD_CHUNKS):
            col = hd * X_HEAD_DIM + c * LANES
            dst_ref[0, pl.ds(c * X_HEADS + hd, N_MEM, stride=KV_ROW_STRIDE), :] = val[:, col:col + LANES]


def _kv_proj_kernel(mem_ref, xk_ref, xv_ref, k_ref, v_ref, kt_ref, vb_ref):
    m = mem_ref[0].astype(BF16)
    k = jnp.dot(m, _unpack(xk_ref[...]), preferred_element_type=F32)
    v = jnp.dot(m, _unpack(xv_ref[...]), preferred_element_type=F32)
    _store_head_split(k_ref, k)
    _store_head_split(v_ref, v)
    kt_ref[0] = _pack(k.T.astype(BF16))
    vb_ref[0] = _pack(v.astype(BF16))


def _kv_proj_call(mem, xk_w, xv_w):
    nb = mem.shape[0]
    blk = lambda s: pl.BlockSpec((1,) + s, lambda b: (b, 0, 0))
    return pl.pallas_call(
        _kv_proj_kernel,
        grid=(nb,),
        in_specs=[blk((N_MEM, D_MODEL)), _const_spec(xk_w.shape), _const_spec(xv_w.shape)],
        out_specs=[blk((KV_ROWS, LANES)), blk((KV_ROWS, LANES)),
                   blk((D_MODEL // 2, N_MEM)), blk((N_MEM // 2, D_MODEL))],
        out_shape=[
            jax.ShapeDtypeStruct((nb, KV_ROWS, LANES), F32),
            jax.ShapeDtypeStruct((nb, KV_ROWS, LANES), F32),
            jax.ShapeDtypeStruct((nb, D_MODEL // 2, N_MEM), U32),
            jax.ShapeDtypeStruct((nb, N_MEM // 2, D_MODEL), U32),
        ],
        compiler_params=_params(1),
        name="kv_proj",
    )(mem, xk_w, xv_w)


def _prompt_kernel(x_ref, kt_ref, v_ref,
                   w_in_ref, b_in_ref, conv_w_ref, conv_b_ref, gn_g_ref, gn_b_ref, gmat_ref,
                   pool_w_ref, pool_scale_ref, w_out_ref, ln1_g_ref, ln1_b_ref, xq_ref,
                   xo_ref, ln2_g_ref, ln2_b_ref, w1_ref, b1_ref, w2_ref, b2_ref, ln3_g_ref, ln3_b_ref,
                   y_ref, conv_new_ref, pool_new_ref,
                   ubuf, pbuf, ybuf, dbuf, mixbuf, obuf, x1buf, x1prev, x2buf, *, nt):
    tm = PROMPT_BLOCK
    g = pl.program_id(0)
    n_blocks = pl.num_programs(0) - 1
    t = jnp.minimum(g, n_blocks - 1) % nt

    @pl.when(t == 0)
    def _():
        ubuf[:, 0:CONV_PAD, :] = jnp.zeros((CONV_SLABS, CONV_PAD, LANES), F32)
        pbuf[:, 0:POOL_PAD, :] = jnp.zeros((POOL_SLABS, POOL_PAD, LANES), F32)

    @pl.when(t > 0)
    def _():
        ubuf[:, 0:CONV_PAD, :] = ubuf[:, tm:tm + CONV_PAD, :]
        pbuf[:, 0:POOL_PAD, :] = pbuf[:, tm:tm + POOL_PAD, :]

    @pl.when(g == 0)
    def _():
        x1buf[...] = jnp.zeros((tm, D_MODEL), F32)

    x = x_ref[0]
    h = _bdot(x, _unpack(w_in_ref[...])) + b_in_ref[...]
    u = h[:, 0:D_CONV] * jax.nn.sigmoid(h[:, D_CONV:2 * D_CONV])
    for s in range(CONV_SLABS):
        ubuf[s, CONV_PAD:CONV_PAD + tm, :] = u[:, s * LANES:(s + 1) * LANES]
    for s in range(POOL_SLABS):
        pbuf[s, POOL_PAD:POOL_PAD + tm, :] = h[:, 2 * D_CONV + s * LANES:2 * D_CONV + (s + 1) * LANES]

    x1prev[...] = _layer_norm(x1buf[...], ln1_g_ref[...], ln1_b_ref[...])
    q = (_bdot(x1prev[...], _unpack(xq_ref[...])) * ATTN_SCALE).astype(BF16)
    for hd in range(X_HEADS):
        sl = slice(hd * X_HEAD_DIM, (hd + 1) * X_HEAD_DIM)
        kt_h = _unpack(kt_ref[0, hd * X_HEAD_DIM // 2:(hd + 1) * X_HEAD_DIM // 2, :])
        s = jnp.dot(q[:, sl], kt_h, preferred_element_type=F32)
        e = jnp.exp(s - jnp.max(s, axis=-1, keepdims=True))
        a = e / jnp.sum(e, axis=-1, keepdims=True)
        v_h = _unpack(v_ref[0, :, sl])
        obuf[:, sl] = jnp.dot(a.astype(BF16), v_h, preferred_element_type=F32).astype(BF16)
    attn = jnp.dot(obuf[...], _unpack(xo_ref[...]), preferred_element_type=F32)
    x2buf[...] = _layer_norm(DN_ALPHA * x1prev[...] + attn, ln2_g_ref[...], ln2_b_ref[...])

    base = CONV_PAD - CONV_CTX
    for s in range(CONV_SLABS):
        sl = slice(s * LANES, (s + 1) * LANES)
        for c in range(tm // CONV_ROWS):
            r0 = c * CONV_ROWS
            acc = jnp.broadcast_to(conv_b_ref[:, sl], (CONV_ROWS, LANES))
            for k in range(CONV_WIDTH):
                acc = acc + ubuf[s, base + r0 + k:base + r0 + k + CONV_ROWS, :] * conv_w_ref[k:k + 1, sl]
            ybuf[r0:r0 + CONV_ROWS, sl] = acc

    pos = t * tm + lax.broadcasted_iota(jnp.int32, (tm, POOL_GROUP), 0)
    for g, w in enumerate(POOL_WINDOWS):
        cur = pbuf[g, POOL_PAD:POOL_PAD + tm, :]
        ws = cur
        for j in range(1, w):
            ws = ws + pbuf[g, POOL_PAD - j:POOL_PAD - j + tm, :]
        cnt = jnp.minimum(pos + 1, w).astype(F32)
        dbuf[:, g * POOL_GROUP:(g + 1) * POOL_GROUP] = ws / cnt - cur

    x2 = x2buf[...]
    hdn = jnp.maximum(_bdot(x2, _unpack(w1_ref[...])) + b1_ref[...], 0.0)
    f = _bdot(hdn * hdn, _unpack(w2_ref[...])) + b2_ref[...]
    y_ref[0] = _layer_norm(DN_ALPHA * x2 + f, ln3_g_ref[...], ln3_b_ref[...])

    x1buf[...] = _mixer_out(ybuf, dbuf, mixbuf, x, gmat_ref, gn_g_ref, gn_b_ref, pool_w_ref, pool_scale_ref,
                            w_out_ref)

    @pl.when(jnp.logical_and(t == nt - 1, g < n_blocks))
    def _():
        for s in range(CONV_SLABS):
            conv_new_ref[0, :, s * LANES:(s + 1) * LANES] = ubuf[s, CONV_PAD + tm - CONV_CTX:CONV_PAD + tm, :]
        for s in range(POOL_SLABS):
            pool_new_ref[0, :, s * LANES:(s + 1) * LANES] = pbuf[s, POOL_PAD + tm - POOL_CTX:POOL_PAD + tm, :]


def _prompt_call(x, kt, vb, weights):
    nb, seq, _ = x.shape
    tm = PROMPT_BLOCK
    nt = seq // tm
    n_blocks = nb * nt
    cur = lambda g: jnp.minimum(g, n_blocks - 1)
    prev = lambda g: jnp.maximum(g - 1, 0)
    in_specs = [
        pl.BlockSpec((1, tm, D_MODEL), lambda g: (cur(g) // nt, cur(g) % nt, 0)),
        pl.BlockSpec((1, D_MODEL // 2, N_MEM), lambda g: (prev(g) // nt, 0, 0)),
        pl.BlockSpec((1, N_MEM // 2, D_MODEL), lambda g: (prev(g) // nt, 0, 0)),
    ] + [_const_spec(w.shape) for w in weights]
    out_specs = [
        pl.BlockSpec((1, tm, D_MODEL), lambda g: (prev(g) // nt, prev(g) % nt, 0)),
        pl.BlockSpec((1, CONV_CTX, D_CONV), lambda g: (cur(g) // nt, 0, 0)),
        pl.BlockSpec((1, POOL_CTX, D_POOL), lambda g: (cur(g) // nt, 0, 0)),
    ]
    out_shape = [
        jax.ShapeDtypeStruct((nb, seq, D_MODEL), F32),
        jax.ShapeDtypeStruct((nb, CONV_CTX, D_CONV), F32),
        jax.ShapeDtypeStruct((nb, POOL_CTX, D_POOL), F32),
    ]
    scratch = [
        pltpu.VMEM((CONV_SLABS, CONV_PAD + tm, LANES), F32),
        pltpu.VMEM((POOL_SLABS, POOL_PAD + tm, LANES), F32),
        pltpu.VMEM((tm, D_CONV), F32),
        pltpu.VMEM((tm, D_POOL), F32),
        pltpu.VMEM((tm, D_MODEL), BF16),
        pltpu.VMEM((tm, D_MODEL), BF16),
        pltpu.VMEM((tm, D_MODEL), F32),
        pltpu.VMEM((tm, D_MODEL), F32),
        pltpu.VMEM((tm, D_MODEL), F32),
    ]
    return pl.pallas_call(
        functools.partial(_prompt_kernel, nt=nt),
        grid=(n_blocks + 1,),
        in_specs=in_specs,
        out_specs=out_specs,
        out_shape=out_shape,
        scratch_shapes=scratch,
        compiler_params=_params(1),
        name="prompt_layer",
    )(x, kt, vb, *weights)


def _sample_mixer_kernel(x_ref, sconv_ref, spool_ref,
                         w_in_ref, b_in_ref, conv_w_ref, conv_b_ref, gn_g_ref, gn_b_ref, gmat_ref,
                         pool_w_ref, pool_scale_ref, w_out_ref, ln1_g_ref, ln1_b_ref, xq_ref,
                         x1_ref, q_ref, conv_new_ref, pool_new_ref,
                         hbuf, yslab, dslab, ybuf, dbuf, mixbuf, *, dec_seq):
    gb = SAMPLE_MIX_BATCH
    ts = dec_seq
    x = x_ref[...]
    h = _bdot(x, _unpack(w_in_ref[...])) + b_in_ref[...]
    u = h[:, 0:D_CONV] * jax.nn.sigmoid(h[:, D_CONV:2 * D_CONV])
    for s in range(CONV_SLABS):
        hbuf[s] = u[:, s * LANES:(s + 1) * LANES]
    for s in range(POOL_SLABS):
        hbuf[CONV_SLABS + s] = h[:, 2 * D_CONV + s * LANES:2 * D_CONV + (s + 1) * LANES]

    def step_rows(slab, t):
        return hbuf[slab, pl.ds(t, gb, stride=ts), :]

    for s in range(CONV_SLABS):
        sl = slice(s * LANES, (s + 1) * LANES)
        new = [step_rows(s, t) for t in range(ts)]
        ext = lambda j: sconv_ref[j, :, sl] if j < CONV_CTX else new[j - CONV_CTX]
        for j in range(CONV_CTX - ts):
            conv_new_ref[j, :, sl] = sconv_ref[j + ts, :, sl]
        for t in range(ts):
            conv_new_ref[CONV_CTX - ts + t, :, sl] = new[t]
            acc = jnp.broadcast_to(conv_b_ref[:, sl], (gb, LANES))
            for k in range(CONV_WIDTH):
                acc = acc + ext(t + k) * conv_w_ref[k:k + 1, sl]
            yslab[s, pl.ds(t, gb, stride=ts), :] = acc

    for g, w in enumerate(POOL_WINDOWS):
        sl = slice(g * LANES, (g + 1) * LANES)
        new = [step_rows(CONV_SLABS + g, t) for t in range(ts)]
        ext = lambda j: spool_ref[j, :, sl] if j < POOL_CTX else new[j - POOL_CTX]
        for j in range(POOL_CTX - ts):
            pool_new_ref[j, :, sl] = spool_ref[j + ts, :, sl]
        for t in range(ts):
            pool_new_ref[POOL_CTX - ts + t, :, sl] = new[t]
            ws = new[t]
            for j in range(1, w):
                ws = ws + ext(POOL_CTX + t - j)
            cnt = float(min(PAST_LEN + t + 1, w))
            dslab[g, pl.ds(t, gb, stride=ts), :] = ws / cnt - new[t]

    for s in range(CONV_SLABS):
        ybuf[:, s * LANES:(s + 1) * LANES] = yslab[s]
    for s in range(POOL_SLABS):
        dbuf[:, s * LANES:(s + 1) * LANES] = dslab[s]

    z1 = _mixer_out(ybuf, dbuf, mixbuf, x, gmat_ref, gn_g_ref, gn_b_ref, pool_w_ref, pool_scale_ref, w_out_ref)
    x1 = _layer_norm(z1, ln1_g_ref[...], ln1_b_ref[...])
    x1_ref[...] = x1
    q_ref[...] = _bdot(x1, _unpack(xq_ref[...])) * ATTN_SCALE


def _sample_mixer_call(xs, sconv_tm, spool_tm, weights, dec_seq):
    n_tok = xs.shape[0]
    nb = n_tok // dec_seq
    gb = SAMPLE_MIX_BATCH
    rows = gb * dec_seq
    row_spec = pl.BlockSpec((rows, D_MODEL), lambda i: (i, 0))
    conv_spec = pl.BlockSpec((CONV_CTX, gb, D_CONV), lambda i: (0, i, 0))
    pool_spec = pl.BlockSpec((POOL_CTX, gb, D_POOL), lambda i: (0, i, 0))
    out_shape = [
        jax.ShapeDtypeStruct((n_tok, D_MODEL), F32),
        jax.ShapeDtypeStruct((n_tok, D_MODEL), F32),
        jax.ShapeDtypeStruct((CONV_CTX, nb, D_CONV), F32),
        jax.ShapeDtypeStruct((POOL_CTX, nb, D_POOL), F32),
    ]
    scratch = [
        pltpu.VMEM((CONV_SLABS + POOL_SLABS, rows, LANES), F32),
        pltpu.VMEM((CONV_SLABS, rows, LANES), F32),
        pltpu.VMEM((POOL_SLABS, rows, LANES), F32),
        pltpu.VMEM((rows, D_CONV), F32),
        pltpu.VMEM((rows, D_POOL), F32),
        pltpu.VMEM((rows, D_MODEL), BF16),
    ]
    return pl.pallas_call(
        functools.partial(_sample_mixer_kernel, dec_seq=dec_seq),
        grid=(nb // gb,),
        in_specs=[row_spec, conv_spec, pool_spec] + [_const_spec(w.shape) for w in weights],
        out_specs=[row_spec, row_spec, conv_spec, pool_spec],
        out_shape=out_shape,
        scratch_shapes=scratch,
        compiler_params=_params(1),
        name="sample_mixer",
    )(xs, sconv_tm, spool_tm, *weights)


def _load_head(ref, b, hd):
    parts = [ref[b, pl.ds(c * X_HEADS + hd, N_MEM, stride=KV_ROW_STRIDE), :] for c in range(HEAD_CHUNKS)]
    return jnp.concatenate(parts, axis=1).astype(BF16)


def _sample_attn_kernel(q_ref, k_ref, v_ref, o_ref, *, dec_seq):
    gb = SAMPLE_ATTN_BATCH
    ts = dec_seq
    q = q_ref[...].astype(BF16)
    nq = gb * ts
    heads = [(b, hd) for b in range(gb) for hd in range(X_HEADS)]
    s = jnp.concatenate(
        [lax.dot_general(q[:, hd * X_HEAD_DIM:(hd + 1) * X_HEAD_DIM], _load_head(k_ref, b, hd),
                         (((1,), (1,)), ((), ())), preferred_element_type=F32) for b, hd in heads], axis=0)
    e = jnp.exp(s - jnp.max(s, axis=-1, keepdims=True))
    a = (e / jnp.sum(e, axis=-1, keepdims=True)).astype(BF16)
    for i, (b, hd) in enumerate(heads):
        o = jnp.dot(a[i * nq:(i + 1) * nq, :], _load_head(v_ref, b, hd), preferred_element_type=F32)
        o_ref[b * ts:(b + 1) * ts, hd * X_HEAD_DIM:(hd + 1) * X_HEAD_DIM] = o[b * ts:(b + 1) * ts, :]


def _sample_attn_call(q, k, v, dec_seq):
    nb = k.shape[0]
    gb = SAMPLE_ATTN_BATCH
    row_spec = pl.BlockSpec((gb * dec_seq, D_MODEL), lambda i: (i, 0))
    kv_spec = pl.BlockSpec((gb, KV_ROWS, LANES), lambda i: (i, 0, 0))
    return pl.pallas_call(
        functools.partial(_sample_attn_kernel, dec_seq=dec_seq),
        grid=(nb // gb,),
        in_specs=[row_spec, kv_spec, kv_spec],
        out_specs=row_spec,
        out_shape=jax.ShapeDtypeStruct((nb * dec_seq, D_MODEL), F32),
        compiler_params=_params(1),
        name="sample_attn",
    )(q, k, v)


def _sample_tail_kernel(x1_ref, o_ref, xo_ref, ln2_g_ref, ln2_b_ref, w1_ref, b1_ref, w2_ref, b2_ref,
                        ln3_g_ref, ln3_b_ref, y_ref):
    y_ref[...] = _tail(x1_ref[...], o_ref[...].astype(BF16), xo_ref, ln2_g_ref, ln2_b_ref,
                       w1_ref, b1_ref, w2_ref, b2_ref, ln3_g_ref, ln3_b_ref)


def _sample_tail_call(x1, o, weights):
    n_tok = x1.shape[0]
    rows = SAMPLE_TAIL_ROWS
    row_spec = pl.BlockSpec((rows, D_MODEL), lambda i: (i, 0))
    return pl.pallas_call(
        _sample_tail_kernel,
        grid=(n_tok // rows,),
        in_specs=[row_spec, row_spec] + [_const_spec(w.shape) for w in weights],
        out_specs=row_spec,
        out_shape=jax.ShapeDtypeStruct((n_tok, D_MODEL), F32),
        compiler_params=_params(1),
        name="sample_tail",
    )(x1, o, *weights)


def _group_indicator():
    gid = jnp.arange(GN_HALF) // CONV_HEAD_DIM
    return (gid[:, None] == gid[None, :]).astype(F32)


def _to_head_split_view(a):
    lead = a.shape[:-3]
    n = len(lead)
    a = a.reshape(lead + (N_MEM, X_HEADS, HEAD_CHUNKS, LANES))
    a = a.transpose(tuple(range(n)) + (n, n + 2, n + 1, n + 3))
    return a.reshape(lead + (KV_ROWS, LANES))


def _from_head_split_view(a):
    lead = a.shape[:-2]
    n = len(lead)
    a = a.reshape(lead + (N_MEM, HEAD_CHUNKS, X_HEADS, LANES))
    a = a.transpose(tuple(range(n)) + (n, n + 2, n + 1, n + 3))
    return a.reshape(lead + (N_MEM, X_HEADS, X_HEAD_DIM))


def kernel(x_prompt, x_sample, mem_prompt, cache_mem_k, cache_mem_v, state_conv, state_pool, w_in, b_in, conv_w, conv_b, gn_g, gn_b, pool_w, pool_scale, w_out, ln1_g, ln1_b, xq_w, xk_w, xv_w, xo_w, ln2_g, ln2_b, w1, b1, w2, b2, ln3_g, ln3_b):
    assert w_in.shape[0] == DEPTH == 1
    n_prompt, seq, _ = x_prompt.shape
    n_dec, dec_seq, _ = x_sample.shape
    assert seq % PROMPT_BLOCK == 0 and n_dec % SAMPLE_MIX_BATCH == 0 and n_dec % SAMPLE_ATTN_BATCH == 0
    assert (n_dec * dec_seq) % SAMPLE_TAIL_ROWS == 0

    w_in_p, w_out_p, xq_p, xo_p, xk_p, xv_p, pool_w_p, gmat_p = _cast_call(
        [w_in[0], w_out[0], xq_w[0], xo_w[0], xk_w[0], xv_w[0],
         pool_w[0].reshape(len(POOL_WINDOWS) * POOL_GROUP, POOL_GROUP), _group_indicator()], steps=4)
    w1_p, w2_p = _cast_call([w1[0], w2[0]], steps=16)

    row = lambda a: a[0].reshape(1, -1)
    mixer_w = (w_in_p, row(b_in), conv_w[0], row(conv_b), row(gn_g), row(gn_b), gmat_p, pool_w_p,
               row(pool_scale), w_out_p, row(ln1_g), row(ln1_b), xq_p)
    tail_w = (xo_p, row(ln2_g), row(ln2_b), w1_p, row(b1), w2_p, row(b2), row(ln3_g), row(ln3_b))

    k_p, v_p, kt_p, vb_p = _kv_proj_call(mem_prompt, xk_p, xv_p)
    y_p, conv_p, pool_p = _prompt_call(x_prompt, kt_p, vb_p, mixer_w + tail_w)

    xs = x_sample.reshape(n_dec * dec_seq, D_MODEL)
    x1_s, q_s, conv_s, pool_s = _sample_mixer_call(
        xs, state_conv[0].transpose(1, 0, 2), state_pool[0].transpose(1, 0, 2), mixer_w, dec_seq)
    o_s = _sample_attn_call(q_s, _to_head_split_view(cache_mem_k[0]), _to_head_split_view(cache_mem_v[0]), dec_seq)
    y_s = _sample_tail_call(x1_s, o_s, tail_w)

    return (y_p, y_s.reshape(n_dec, dec_seq, D_MODEL),
            _from_head_split_view(k_p)[None], _from_head_split_view(v_p)[None],
            conv_p[None], conv_s.transpose(1, 0, 2)[None], pool_p[None], pool_s.transpose(1, 0, 2)[None])
```

```python
import functools

import jax
import jax.numpy as jnp
from jax import lax
from jax.experimental import pallas as pl
from jax.experimental.pallas import tpu as pltpu

F32 = jnp.float32
BF16 = jnp.bfloat16
U32 = jnp.uint32

LANES = 128
D_MODEL = 1024
D_CONV = 512
D_POOL = 512
D_IN = 2 * D_CONV + D_POOL
CONV_WIDTH = 31
CONV_CTX = CONV_WIDTH - 1
CONV_HEADS = 8
CONV_HEAD_DIM = D_CONV // CONV_HEADS
POOL_WINDOWS = (2, 4, 8, 16)
POOL_GROUP = D_POOL // len(POOL_WINDOWS)
POOL_CTX = max(POOL_WINDOWS) - 1
N_MEM = 256
X_HEADS = 4
X_HEAD_DIM = D_MODEL // X_HEADS
HEAD_CHUNKS = X_HEAD_DIM // LANES
KV_ROWS = N_MEM * X_HEADS * HEAD_CHUNKS
KV_ROW_STRIDE = X_HEADS * HEAD_CHUNKS
D_FF = 4 * D_MODEL
LN_EPS = 1e-5
DEPTH = 1
PAST_LEN = 16384
DN_ALPHA = (2.0 * DEPTH) ** 0.25
ATTN_SCALE = X_HEAD_DIM ** -0.5

assert POOL_GROUP == LANES and X_HEAD_DIM % LANES == 0

CONV_SLABS = D_CONV // LANES
POOL_SLABS = D_POOL // LANES
CONV_PAD = 32
POOL_PAD = 16
GN_HALF = 256

PROMPT_BLOCK = 256
CONV_ROWS = 32
SAMPLE_MIX_BATCH = 32
SAMPLE_ATTN_ROWS = 16
SUBLANES = 8
SAMPLE_TAIL_ROWS = 256
CAST_ROWS = 256
VMEM_LIMIT = 56 * 1024 * 1024


def _unpack(w_u32):
    return pltpu.bitcast(w_u32, BF16)


def _pack(w_bf16):
    return pltpu.bitcast(w_bf16, U32)


def _bdot(a, w):
    return jnp.dot(a.astype(BF16), w, preferred_element_type=F32)


def _layer_norm(x, g, b):
    mu = jnp.mean(x, axis=-1, keepdims=True)
    d = x - mu
    var = jnp.mean(d * d, axis=-1, keepdims=True)
    return d * lax.rsqrt(var + LN_EPS) * g + b


def _group_norm_swish(y, gmat, gn_g, gn_b):
    inv = 1.0 / CONV_HEAD_DIM
    mu = _bdot(y, gmat) * inv
    d = y - mu
    var = _bdot(d * d, gmat) * inv
    n = d * lax.rsqrt(var + LN_EPS) * gn_g + gn_b
    return n * jax.nn.sigmoid(n)


def _mixer_acts(ybuf, dbuf, mixbuf, gmat_ref, gn_g_ref, gn_b_ref, pool_w_ref, pool_scale_ref):
    gmat = _unpack(gmat_ref[...])
    for j in range(D_CONV // GN_HALF):
        sl = slice(j * GN_HALF, (j + 1) * GN_HALF)
        o = _group_norm_swish(ybuf[:, sl], gmat, gn_g_ref[:, sl], gn_b_ref[:, sl])
        mixbuf[:, sl] = o.astype(BF16)
    half = POOL_GROUP // 2
    for g in range(len(POOL_WINDOWS)):
        sl = slice(g * POOL_GROUP, (g + 1) * POOL_GROUP)
        pw = _unpack(pool_w_ref[g * half:(g + 1) * half, :])
        mixed = _bdot(dbuf[:, sl], pw) * pool_scale_ref[:, sl]
        mixbuf[:, D_CONV + g * POOL_GROUP:D_CONV + (g + 1) * POOL_GROUP] = mixed.astype(BF16)


def _mixer_out(mixbuf, x, w_out_ref):
    return DN_ALPHA * x + jnp.dot(mixbuf[...], _unpack(w_out_ref[...]), preferred_element_type=F32)


def _tail(x1, o_bf16, xo_ref, ln2_g_ref, ln2_b_ref, w1_ref, b1_ref, w2_ref, b2_ref, ln3_g_ref, ln3_b_ref):
    attn = jnp.dot(o_bf16, _unpack(xo_ref[...]), preferred_element_type=F32)
    x2 = _layer_norm(DN_ALPHA * x1 + attn, ln2_g_ref[...], ln2_b_ref[...])
    hdn = jnp.maximum(_bdot(x2, _unpack(w1_ref[...])) + b1_ref[...], 0.0)
    f = _bdot(hdn * hdn, _unpack(w2_ref[...])) + b2_ref[...]
    return _layer_norm(DN_ALPHA * x2 + f, ln3_g_ref[...], ln3_b_ref[...])


def _const_spec(shape):
    nd = len(shape)
    return pl.BlockSpec(shape, lambda *_: (0,) * nd, pipeline_mode=pl.Buffered(1))


def _params(n_axes):
    return pltpu.CompilerParams(dimension_semantics=("arbitrary",) * n_axes, vmem_limit_bytes=VMEM_LIMIT)


def _cast_kernel(*refs):
    n = len(refs) // 2
    for src, dst in zip(refs[:n], refs[n:]):
        dst[...] = _pack(src[...].astype(BF16))


def _cast_call(ws, steps):
    in_specs, out_specs, out_shape = [], [], []
    for w in ws:
        k, n = w.shape
        rows = k // steps
        assert rows * steps == k and rows % 16 == 0
        in_specs.append(pl.BlockSpec((rows, n), lambda i: (i, 0)))
        out_specs.append(pl.BlockSpec((rows // 2, n), lambda i: (i, 0)))
        out_shape.append(jax.ShapeDtypeStruct((k // 2, n), U32))
    return pl.pallas_call(
        _cast_kernel, grid=(steps,), in_specs=in_specs, out_specs=out_specs, out_shape=out_shape,
        compiler_params=_params(1), name="cast_weights",
    )(*ws)


def _store_head_split(dst_ref, val):
    for hd in range(X_HEADS):
        for c in range(HEAD_CHUNKS):
            col = hd * X_HEAD_DIM + c * LANES
            dst_ref[0, pl.ds(c * X_HEADS + hd, N_MEM, stride=KV_ROW_STRIDE), :] = val[:, col:col + LANES]


def _kv_proj_kernel(mem_ref, xk_ref, xv_ref, k_ref, v_ref, kt_ref, vb_ref):
    m = mem_ref[0].astype(BF16)
    k = jnp.dot(m, _unpack(xk_ref[...]), preferred_element_type=F32)
    v = jnp.dot(m, _unpack(xv_ref[...]), preferred_element_type=F32)
    _store_head_split(k_ref, k)
    _store_head_split(v_ref, v)
    kt_ref[0] = _pack(k.T.astype(BF16))
    vb_ref[0] = _pack(v.astype(BF16))


def _kv_proj_call(mem, xk_w, xv_w):
    nb = mem.shape[0]
    blk = lambda s: pl.BlockSpec((1,) + s, lambda b: (b, 0, 0))
    return pl.pallas_call(
        _kv_proj_kernel,
        grid=(nb,),
        in_specs=[blk((N_MEM, D_MODEL)), _const_spec(xk_w.shape), _const_spec(xv_w.shape)],
        out_specs=[blk((KV_ROWS, LANES)), blk((KV_ROWS, LANES)),
                   blk((D_MODEL // 2, N_MEM)), blk((N_MEM // 2, D_MODEL))],
        out_shape=[
            jax.ShapeDtypeStruct((nb, KV_ROWS, LANES), F32),
            jax.ShapeDtypeStruct((nb, KV_ROWS, LANES), F32),
            jax.ShapeDtypeStruct((nb, D_MODEL // 2, N_MEM), U32),
            jax.ShapeDtypeStruct((nb, N_MEM // 2, D_MODEL), U32),
        ],
        compiler_params=_params(1),
        name="kv_proj",
    )(mem, xk_w, xv_w)


def _prompt_kernel(x_ref, kt_ref, v_ref, qs_ref, kc_ref, vc_ref,
                   w_in_ref, b_in_ref, conv_w_ref, conv_b_ref, gn_g_ref, gn_b_ref, gmat_ref,
                   pool_w_ref, pool_scale_ref, w_out_ref, ln1_g_ref, ln1_b_ref, xq_ref,
                   xo_ref, ln2_g_ref, ln2_b_ref, w1_ref, b1_ref, w2_ref, b2_ref, ln3_g_ref, ln3_b_ref,
                   y_ref, conv_new_ref, pool_new_ref, os_ref,
                   ubuf, pbuf, ybuf, dbuf, mixbuf, obuf, x1buf, x1prev, x2buf, hbuf, *, nt):
    tm = PROMPT_BLOCK
    g = pl.program_id(0)
    n_blocks = pl.num_programs(0) - 1
    t = jnp.minimum(g, n_blocks - 1) % nt

    @pl.when(t == 0)
    def _():
        ubuf[:, 0:CONV_PAD, :] = jnp.zeros((CONV_SLABS, CONV_PAD, LANES), F32)
        pbuf[:, 0:POOL_PAD, :] = jnp.zeros((POOL_SLABS, POOL_PAD, LANES), F32)

    @pl.when(t > 0)
    def _():
        ubuf[:, 0:CONV_PAD, :] = ubuf[:, tm:tm + CONV_PAD, :]
        pbuf[:, 0:POOL_PAD, :] = pbuf[:, tm:tm + POOL_PAD, :]

    @pl.when(g == 0)
    def _():
        x1buf[...] = jnp.zeros((tm, D_MODEL), F32)

    x = x_ref[0]
    h = _bdot(x, _unpack(w_in_ref[...])) + b_in_ref[...]
    u = h[:, 0:D_CONV] * jax.nn.sigmoid(h[:, D_CONV:2 * D_CONV])
    for s in range(CONV_SLABS):
        ubuf[s, CONV_PAD:CONV_PAD + tm, :] = u[:, s * LANES:(s + 1) * LANES]
    for s in range(POOL_SLABS):
        pbuf[s, POOL_PAD:POOL_PAD + tm, :] = h[:, 2 * D_CONV + s * LANES:2 * D_CONV + (s + 1) * LANES]

    x1prev[...] = _layer_norm(x1buf[...], ln1_g_ref[...], ln1_b_ref[...])
    q = (_bdot(x1prev[...], _unpack(xq_ref[...])) * ATTN_SCALE).astype(BF16)
    head = lambda hd: slice(hd * X_HEAD_DIM, (hd + 1) * X_HEAD_DIM)
    scores = [jnp.dot(q[:, head(hd)], _unpack(kt_ref[0, hd * X_HEAD_DIM // 2:(hd + 1) * X_HEAD_DIM // 2, :]),
                      preferred_element_type=F32) for hd in range(X_HEADS)]
    probs_dec = _sample_attn_probs(qs_ref, kc_ref)
    probs = []
    for s in scores:
        e = jnp.exp(s - jnp.max(s, axis=-1, keepdims=True))
        probs.append((e / jnp.sum(e, axis=-1, keepdims=True)).astype(BF16))
    for hd in range(X_HEADS):
        obuf[:, head(hd)] = jnp.dot(probs[hd], _unpack(v_ref[0, :, head(hd)]),
                                    preferred_element_type=F32).astype(BF16)

    base = CONV_PAD - CONV_CTX
    for s in range(CONV_SLABS):
        sl = slice(s * LANES, (s + 1) * LANES)
        for c in range(tm // CONV_ROWS):
            r0 = c * CONV_ROWS
            acc = jnp.broadcast_to(conv_b_ref[:, sl], (CONV_ROWS, LANES))
            for k in range(CONV_WIDTH):
                acc = acc + ubuf[s, base + r0 + k:base + r0 + k + CONV_ROWS, :] * conv_w_ref[k:k + 1, sl]
            ybuf[r0:r0 + CONV_ROWS, sl] = acc

    pos = t * tm + lax.broadcasted_iota(jnp.int32, (tm, POOL_GROUP), 0)
    for g, w in enumerate(POOL_WINDOWS):
        cur = pbuf[g, POOL_PAD:POOL_PAD + tm, :]
        ws = cur
        for j in range(1, w):
            ws = ws + pbuf[g, POOL_PAD - j:POOL_PAD - j + tm, :]
        cnt = jnp.minimum(pos + 1, w).astype(F32)
        dbuf[:, g * POOL_GROUP:(g + 1) * POOL_GROUP] = ws / cnt - cur

    attn = jnp.dot(obuf[...], _unpack(xo_ref[...]), preferred_element_type=F32)
    _sample_attn_values(probs_dec, vc_ref, os_ref, os_ref.shape[0] // vc_ref.shape[0])
    x2buf[...] = _layer_norm(DN_ALPHA * x1prev[...] + attn, ln2_g_ref[...], ln2_b_ref[...])
    hdn =jnp.maximum(_bdot(x2buf[...], _unpack(w1_ref[...])) + b1_ref[...], 0.0)
    hbuf[...] = (hdn * hdn).astype(BF16)

    _mixer_acts(ybuf, dbuf, mixbuf, gmat_ref, gn_g_ref, gn_b_ref, pool_w_ref, pool_scale_ref)

    f = jnp.dot(hbuf[...], _unpack(w2_ref[...]), preferred_element_type=F32) + b2_ref[...]
    x1buf[...] = _mixer_out(mixbuf, x, w_out_ref)
    y_ref[0] = _layer_norm(DN_ALPHA * x2buf[...] + f, ln3_g_ref[...], ln3_b_ref[...])

    @pl.when(jnp.logical_and(t == nt - 1, g < n_blocks))
    def _():
        for s in range(CONV_SLABS):
            conv_new_ref[0, :, s * LANES:(s + 1) * LANES] = ubuf[s, CONV_PAD + tm - CONV_CTX:CONV_PAD + tm, :]
        for s in range(POOL_SLABS):
            pool_new_ref[0, :, s * LANES:(s + 1) * LANES] = pbuf[s, POOL_PAD + tm - POOL_CTX:POOL_PAD + tm, :]


def _prompt_call(x, kt, vb, q_dec, k_dec, v_dec, dec_seq, weights):
    nb, seq, _ = x.shape
    tm = PROMPT_BLOCK
    nt = seq // tm
    n_blocks = nb * nt
    n_dec = k_dec.shape[0]
    dec_per_step = -(-n_dec // n_blocks)
    dec_steps = n_dec // dec_per_step
    dec_rows = dec_per_step * dec_seq
    assert dec_steps * dec_per_step == n_dec and dec_steps <= n_blocks + 1
    assert dec_rows % SUBLANES == 0 and SAMPLE_ATTN_ROWS % dec_rows == 0
    cur = lambda g: jnp.minimum(g, n_blocks - 1)
    prev = lambda g: jnp.maximum(g - 1, 0)
    dec = lambda g: jnp.minimum(g, dec_steps - 1)
    in_specs = [
        pl.BlockSpec((1, tm, D_MODEL), lambda g: (cur(g) // nt, cur(g) % nt, 0)),
        pl.BlockSpec((1, D_MODEL // 2, N_MEM), lambda g: (prev(g) // nt, 0, 0)),
        pl.BlockSpec((1, N_MEM // 2, D_MODEL), lambda g: (prev(g) // nt, 0, 0)),
        pl.BlockSpec((dec_rows, D_MODEL), lambda g: (dec(g), 0)),
        pl.BlockSpec((dec_per_step, KV_ROWS, LANES), lambda g: (dec(g), 0, 0)),
        pl.BlockSpec((dec_per_step, KV_ROWS, LANES), lambda g: (dec(g), 0, 0)),
    ] + [_const_spec(w.shape) for w in weights]
    out_specs = [
        pl.BlockSpec((1, tm, D_MODEL), lambda g: (prev(g) // nt, prev(g) % nt, 0)),
        pl.BlockSpec((1, CONV_CTX, D_CONV), lambda g: (cur(g) // nt, 0, 0)),
        pl.BlockSpec((1, POOL_CTX, D_POOL), lambda g: (cur(g) // nt, 0, 0)),
        pl.BlockSpec((dec_rows, D_MODEL), lambda g: (dec(g), 0)),
    ]
    out_shape = [
        jax.ShapeDtypeStruct((nb, seq, D_MODEL), F32),
        jax.ShapeDtypeStruct((nb, CONV_CTX, D_CONV), F32),
        jax.ShapeDtypeStruct((nb, POOL_CTX, D_POOL), F32),
        jax.ShapeDtypeStruct(q_dec.shape, F32),
    ]
    scratch = [
        pltpu.VMEM((CONV_SLABS, CONV_PAD + tm, LANES), F32),
        pltpu.VMEM((POOL_SLABS, POOL_PAD + tm, LANES), F32),
        pltpu.VMEM((tm, D_CONV), F32),
        pltpu.VMEM((tm, D_POOL), F32),
        pltpu.VMEM((tm, D_MODEL), BF16),
        pltpu.VMEM((tm, D_MODEL), BF16),
        pltpu.VMEM((tm, D_MODEL), F32),
        pltpu.VMEM((tm, D_MODEL), F32),
        pltpu.VMEM((tm, D_MODEL), F32),
        pltpu.VMEM((tm, D_FF), BF16),
    ]
    return pl.pallas_call(
        functools.partial(_prompt_kernel, nt=nt),
        grid=(n_blocks + 1,),
        in_specs=in_specs,
        out_specs=out_specs,
        out_shape=out_shape,
        scratch_shapes=scratch,
        compiler_params=_params(1),
        name="prompt_layer",
    )(x, kt, vb, q_dec, k_dec, v_dec, *weights)


def _sample_mixer_kernel(x_ref, sconv_ref, spool_ref,
                         w_in_ref, b_in_ref, conv_w_ref, conv_b_ref, gn_g_ref, gn_b_ref, gmat_ref,
                         pool_w_ref, pool_scale_ref, w_out_ref, ln1_g_ref, ln1_b_ref, xq_ref,
                         x1_ref, q_ref, conv_new_ref, pool_new_ref,
                         hbuf, yslab, dslab, ybuf, dbuf, mixbuf, *, dec_seq):
    gb = SAMPLE_MIX_BATCH
    ts = dec_seq
    x = x_ref[...]
    h = _bdot(x, _unpack(w_in_ref[...])) + b_in_ref[...]
    u = h[:, 0:D_CONV] * jax.nn.sigmoid(h[:, D_CONV:2 * D_CONV])
    for s in range(CONV_SLABS):
        hbuf[s] = u[:, s * LANES:(s + 1) * LANES]
    for s in range(POOL_SLABS):
        hbuf[CONV_SLABS + s] = h[:, 2 * D_CONV + s * LANES:2 * D_CONV + (s + 1) * LANES]

    def step_rows(slab, t):
        return hbuf[slab, pl.ds(t, gb, stride=ts), :]

    for s in range(CONV_SLABS):
        sl = slice(s * LANES, (s + 1) * LANES)
        new = [step_rows(s, t) for t in range(ts)]
        ext = lambda j: sconv_ref[j, :, sl] if j < CONV_CTX else new[j - CONV_CTX]
        for j in range(CONV_CTX - ts):
            conv_new_ref[j, :, sl] = sconv_ref[j + ts, :, sl]
        for t in range(ts):
            conv_new_ref[CONV_CTX - ts + t, :, sl] = new[t]
            acc = jnp.broadcast_to(conv_b_ref[:, sl], (gb, LANES))
            for k in range(CONV_WIDTH):
                acc = acc + ext(t + k) * conv_w_ref[k:k + 1, sl]
            yslab[s, pl.ds(t, gb, stride=ts), :] = acc

    for g, w in enumerate(POOL_WINDOWS):
        sl = slice(g * LANES, (g + 1) * LANES)
        new = [step_rows(CONV_SLABS + g, t) for t in range(ts)]
        ext = lambda j: spool_ref[j, :, sl] if j < POOL_CTX else new[j - POOL_CTX]
        for j in range(POOL_CTX - ts):
            pool_new_ref[j, :, sl] = spool_ref[j + ts, :, sl]
        for t in range(ts):
            pool_new_ref[POOL_CTX - ts + t, :, sl] = new[t]
            ws = new[t]
            for j in range(1, w):
                ws = ws + ext(POOL_CTX + t - j)
            cnt = float(min(PAST_LEN + t + 1, w))
            dslab[g, pl.ds(t, gb, stride=ts), :] = ws / cnt - new[t]

    for s in range(CONV_SLABS):
        ybuf[:, s * LANES:(s + 1) * LANES] = yslab[s]
    for s in range(POOL_SLABS):
        dbuf[:, s * LANES:(s + 1) * LANES] = dslab[s]

    _mixer_acts(ybuf, dbuf, mixbuf, gmat_ref, gn_g_ref, gn_b_ref, pool_w_ref, pool_scale_ref)
    x1 = _layer_norm(_mixer_out(mixbuf, x, w_out_ref), ln1_g_ref[...], ln1_b_ref[...])
    x1_ref[...] = x1
    q_ref[...] = _bdot(x1, _unpack(xq_ref[...])) * ATTN_SCALE


def _sample_mixer_call(xs, sconv_tm, spool_tm, weights, dec_seq):
    n_tok = xs.shape[0]
    nb = n_tok // dec_seq
    gb = SAMPLE_MIX_BATCH
    rows = gb * dec_seq
    row_spec = pl.BlockSpec((rows, D_MODEL), lambda i: (i, 0))
    conv_spec = pl.BlockSpec((CONV_CTX, gb, D_CONV), lambda i: (0, i, 0))
    pool_spec = pl.BlockSpec((POOL_CTX, gb, D_POOL), lambda i: (0, i, 0))
    out_shape = [
        jax.ShapeDtypeStruct((n_tok, D_MODEL), F32),
        jax.ShapeDtypeStruct((n_tok, D_MODEL), F32),
        jax.ShapeDtypeStruct((CONV_CTX, nb, D_CONV), F32),
        jax.ShapeDtypeStruct((POOL_CTX, nb, D_POOL), F32),
    ]
    scratch = [
        pltpu.VMEM((CONV_SLABS + POOL_SLABS, rows, LANES), F32),
        pltpu.VMEM((CONV_SLABS, rows, LANES), F32),
        pltpu.VMEM((POOL_SLABS, rows, LANES), F32),
        pltpu.VMEM((rows, D_CONV), F32),
        pltpu.VMEM((rows, D_POOL), F32),
        pltpu.VMEM((rows, D_MODEL), BF16),
    ]
    return pl.pallas_call(
        functools.partial(_sample_mixer_kernel, dec_seq=dec_seq),
        grid=(nb // gb,),
        in_specs=[row_spec, conv_spec, pool_spec] + [_const_spec(w.shape) for w in weights],
        out_specs=[row_spec, row_spec, conv_spec, pool_spec],
        out_shape=out_shape,
        scratch_shapes=scratch,
        compiler_params=_params(1),
        name="sample_mixer",
    )(xs, sconv_tm, spool_tm, *weights)


def _load_head(ref, b, hd):
    parts = [ref[b, pl.ds(c * X_HEADS + hd, N_MEM, stride=KV_ROW_STRIDE), :] for c in range(HEAD_CHUNKS)]
    return jnp.concatenate(parts, axis=1).astype(BF16)


def _sample_attn_probs(q_ref, k_ref):
    q8 = q_ref[...]
    q = jnp.concatenate([q8] * (SAMPLE_ATTN_ROWS // q8.shape[0]), axis=0).astype(BF16)
    s = jnp.concatenate(
        [lax.dot_general(q[:, hd * X_HEAD_DIM:(hd + 1) * X_HEAD_DIM], _load_head(k_ref, b, hd),
                         (((1,), (1,)), ((), ())), preferred_element_type=F32)
         for b in range(k_ref.shape[0]) for hd in range(X_HEADS)], axis=0)
    e = jnp.exp(s - jnp.max(s, axis=-1, keepdims=True))
    return (e / jnp.sum(e, axis=-1, keepdims=True)).astype(BF16)


def _sample_attn_values(a, v_ref, o_ref, ts):
    for b in range(v_ref.shape[0]):
        for hd in range(X_HEADS):
            i = b * X_HEADS + hd
            o = jnp.dot(a[i * SAMPLE_ATTN_ROWS:(i + 1) * SAMPLE_ATTN_ROWS, :], _load_head(v_ref, b, hd),
                        preferred_element_type=F32)
            o_ref[b * ts:(b + 1) * ts, hd * X_HEAD_DIM:(hd + 1) * X_HEAD_DIM] = o[b * ts:(b + 1) * ts, :]


def _sample_tail_kernel(x1_ref, o_ref, xo_ref, ln2_g_ref, ln2_b_ref, w1_ref, b1_ref, w2_ref, b2_ref,
                        ln3_g_ref, ln3_b_ref, y_ref):
    y_ref[...] = _tail(x1_ref[...], o_ref[...].astype(BF16), xo_ref, ln2_g_ref, ln2_b_ref,
                       w1_ref, b1_ref, w2_ref, b2_ref, ln3_g_ref, ln3_b_ref)


def _sample_tail_call(x1, o, weights):
    n_tok = x1.shape[0]
    rows = SAMPLE_TAIL_ROWS
    row_spec = pl.BlockSpec((rows, D_MODEL), lambda i: (i, 0))
    return pl.pallas_call(
        _sample_tail_kernel,
        grid=(n_tok // rows,),
        in_specs=[row_spec, row_spec] + [_const_spec(w.shape) for w in weights],
        out_specs=row_spec,
        out_shape=jax.ShapeDtypeStruct((n_tok, D_MODEL), F32),
        compiler_params=_params(1),
        name="sample_tail",
    )(x1, o, *weights)


def _group_indicator():
    gid = jnp.arange(GN_HALF) // CONV_HEAD_DIM
    return (gid[:, None] == gid[None, :]).astype(F32)


def _to_head_split_view(a):
    lead = a.shape[:-3]
    n = len(lead)
    a = a.reshape(lead + (N_MEM, X_HEADS, HEAD_CHUNKS, LANES))
    a = a.transpose(tuple(range(n)) + (n, n + 2, n + 1, n + 3))
    return a.reshape(lead + (KV_ROWS, LANES))


def _from_head_split_view(a):
    lead = a.shape[:-2]
    n = len(lead)
    a = a.reshape(lead + (N_MEM, HEAD_CHUNKS, X_HEADS, LANES))
    a = a.transpose(tuple(range(n)) + (n, n + 2, n + 1, n + 3))
    return a.reshape(lead + (N_MEM, X_HEADS, X_HEAD_DIM))


def kernel(x_prompt, x_sample, mem_prompt, cache_mem_k, cache_mem_v, state_conv, state_pool, w_in, b_in, conv_w, conv_b, gn_g, gn_b, pool_w, pool_scale, w_out, ln1_g, ln1_b, xq_w, xk_w, xv_w, xo_w, ln2_g, ln2_b, w1, b1, w2, b2, ln3_g, ln3_b):
    assert w_in.shape[0] == DEPTH == 1
    n_prompt, seq, _ = x_prompt.shape
    n_dec, dec_seq, _ = x_sample.shape
    assert seq % PROMPT_BLOCK == 0 and n_dec % SAMPLE_MIX_BATCH == 0
    assert (n_dec * dec_seq) % SAMPLE_TAIL_ROWS == 0

    w_in_p, w_out_p, xq_p, xo_p, xk_p, xv_p, pool_w_p, gmat_p = _cast_call(
        [w_in[0], w_out[0], xq_w[0], xo_w[0], xk_w[0], xv_w[0],
         pool_w[0].reshape(len(POOL_WINDOWS) * POOL_GROUP, POOL_GROUP), _group_indicator()], steps=4)
    w1_p, w2_p = _cast_call([w1[0], w2[0]], steps=16)

    row = lambda a: a[0].reshape(1, -1)
    mixer_w = (w_in_p, row(b_in), conv_w[0], row(conv_b), row(gn_g), row(gn_b), gmat_p, pool_w_p,
               row(pool_scale), w_out_p, row(ln1_g), row(ln1_b), xq_p)
    tail_w = (xo_p, row(ln2_g), row(ln2_b), w1_p, row(b1), w2_p, row(b2), row(ln3_g), row(ln3_b))

    xs = x_sample.reshape(n_dec * dec_seq, D_MODEL)
    x1_s, q_s, conv_s, pool_s = _sample_mixer_call(
        xs, state_conv[0].transpose(1, 0, 2), state_pool[0].transpose(1, 0, 2), mixer_w, dec_seq)

    k_p, v_p, kt_p, vb_p = _kv_proj_call(mem_prompt, xk_p, xv_p)
    y_p, conv_p, pool_p, o_s = _prompt_call(
        x_prompt, kt_p, vb_p, q_s, _to_head_split_view(cache_mem_k[0]), _to_head_split_view(cache_mem_v[0]),
        dec_seq, mixer_w + tail_w)

    y_s = _sample_tail_call(x1_s, o_s, tail_w)

    return (y_p, y_s.reshape(n_dec, dec_seq, D_MODEL),
            _from_head_split_view(k_p)[None], _from_head_split_view(v_p)[None],
            conv_p[None], conv_s.transpose(1, 0, 2)[None], pool_p[None], pool_s.transpose(1, 0, 2)[None])
```

```python
import functools

import jax
import jax.numpy as jnp
from jax import lax
from jax.experimental import pallas as pl
from jax.experimental.pallas import tpu as pltpu

F32 = jnp.float32
BF16 = jnp.bfloat16
U32 = jnp.uint32

LANES = 128
SUBLANES = 8
D_MODEL = 1024
D_CONV = 512
D_POOL = 512
D_IN = 2 * D_CONV + D_POOL
CONV_WIDTH = 31
CONV_CTX = CONV_WIDTH - 1
CONV_HEADS = 8
CONV_HEAD_DIM = D_CONV // CONV_HEADS
POOL_WINDOWS = (2, 4, 8, 16)
POOL_GROUP = D_POOL // len(POOL_WINDOWS)
POOL_CTX = max(POOL_WINDOWS) - 1
N_MEM = 256
X_HEADS = 4
X_HEAD_DIM = D_MODEL // X_HEADS
HEAD_CHUNKS = X_HEAD_DIM // LANES
KV_ROWS = N_MEM * X_HEADS * HEAD_CHUNKS
KV_ROW_STRIDE = X_HEADS * HEAD_CHUNKS
D_FF = 4 * D_MODEL
LN_EPS = 1e-5
DEPTH = 1
PAST_LEN = 16384
DN_ALPHA = (2.0 * DEPTH) ** 0.25
ATTN_SCALE = X_HEAD_DIM ** -0.5

assert POOL_GROUP == LANES and X_HEAD_DIM % LANES == 0

CONV_SLABS = D_CONV // LANES
POOL_SLABS = D_POOL // LANES
CONV_PAD = 32
POOL_PAD = 16
GN_HALF = 256

PROMPT_BLOCK = 256
CONV_ROWS = 32
SAMPLE_MIX_BATCH = 32
SAMPLE_ATTN_ROWS = 16
TAIL_FF_CHUNK = 1024
CAST_STEPS = 8
VMEM_LIMIT = 56 * 1024 * 1024


def _unpack(w_u32):
    return pltpu.bitcast(w_u32, BF16)


def _pack(w_bf16):
    return pltpu.bitcast(w_bf16, U32)


def _bdot(a, w):
    return jnp.dot(a.astype(BF16), w, preferred_element_type=F32)


def _layer_norm(x, g, b):
    mu = jnp.mean(x, axis=-1, keepdims=True)
    d = x - mu
    var = jnp.mean(d * d, axis=-1, keepdims=True)
    return d * lax.rsqrt(var + LN_EPS) * g + b


def _group_norm_swish(y, gmat, gn_g, gn_b):
    inv = 1.0 / CONV_HEAD_DIM
    mu = _bdot(y, gmat) * inv
    d = y - mu
    var = _bdot(d * d, gmat) * inv
    n = d * lax.rsqrt(var + LN_EPS) * gn_g + gn_b
    return n * jax.nn.sigmoid(n)


def _mixer_acts(ybuf, dbuf, mixbuf, gmat_ref, gn_g_ref, gn_b_ref, pool_w_ref, pool_scale_ref):
    gmat = _unpack(gmat_ref[...])
    for j in range(D_CONV // GN_HALF):
        sl = slice(j * GN_HALF, (j + 1) * GN_HALF)
        o = _group_norm_swish(ybuf[:, sl], gmat, gn_g_ref[:, sl], gn_b_ref[:, sl])
        mixbuf[:, sl] = o.astype(BF16)
    half = POOL_GROUP // 2
    for g in range(len(POOL_WINDOWS)):
        sl = slice(g * POOL_GROUP, (g + 1) * POOL_GROUP)
        pw = _unpack(pool_w_ref[g * half:(g + 1) * half, :])
        mixed = _bdot(dbuf[:, sl], pw) * pool_scale_ref[:, sl]
        mixbuf[:, D_CONV + g * POOL_GROUP:D_CONV + (g + 1) * POOL_GROUP] = mixed.astype(BF16)


def _mixer_out(mixbuf, x, w_out_ref):
    return DN_ALPHA * x + jnp.dot(mixbuf[...], _unpack(w_out_ref[...]), preferred_element_type=F32)


def _const_spec(shape):
    nd = len(shape)
    return pl.BlockSpec(shape, lambda *_: (0,) * nd, pipeline_mode=pl.Buffered(1))


def _params(n_axes):
    return pltpu.CompilerParams(dimension_semantics=("arbitrary",) * n_axes, vmem_limit_bytes=VMEM_LIMIT)


def _cast_kernel(*refs):
    n = len(refs) // 2
    for src, dst in zip(refs[:n], refs[n:]):
        dst[...] = _pack(src[...].astype(BF16))


def _cast_call(ws):
    in_specs, out_specs, out_shape = [], [], []
    for w in ws:
        k, n = w.shape
        rows = k // CAST_STEPS
        assert rows * CAST_STEPS == k and rows % 16 == 0
        in_specs.append(pl.BlockSpec((rows, n), lambda i: (i, 0)))
        out_specs.append(pl.BlockSpec((rows // 2, n), lambda i: (i, 0)))
        out_shape.append(jax.ShapeDtypeStruct((k // 2, n), U32))
    return pl.pallas_call(
        _cast_kernel, grid=(CAST_STEPS,), in_specs=in_specs, out_specs=out_specs, out_shape=out_shape,
        compiler_params=_params(1), name="cast_weights",
    )(*ws)


def _store_head_split(dst_ref, val):
    for hd in range(X_HEADS):
        for c in range(HEAD_CHUNKS):
            col = hd * X_HEAD_DIM + c * LANES
            dst_ref[0, pl.ds(c * X_HEADS + hd, N_MEM, stride=KV_ROW_STRIDE), :] = val[:, col:col + LANES]


def _kv_proj_kernel(mem_ref, xk_ref, xv_ref, k_ref, v_ref, kt_ref, vb_ref):
    m = mem_ref[0].astype(BF16)
    k = jnp.dot(m, _unpack(xk_ref[...]), preferred_element_type=F32)
    v = jnp.dot(m, _unpack(xv_ref[...]), preferred_element_type=F32)
    _store_head_split(k_ref, k)
    _store_head_split(v_ref, v)
    kt_ref[0] = _pack(k.T.astype(BF16))
    vb_ref[0] = _pack(v.astype(BF16))


def _kv_proj_call(mem, xk_w, xv_w):
    nb = mem.shape[0]
    blk = lambda s: pl.BlockSpec((1,) + s, lambda b: (b, 0, 0))
    return pl.pallas_call(
        _kv_proj_kernel,
        grid=(nb,),
        in_specs=[blk((N_MEM, D_MODEL)), _const_spec(xk_w.shape), _const_spec(xv_w.shape)],
        out_specs=[blk((KV_ROWS, LANES)), blk((KV_ROWS, LANES)),
                   blk((D_MODEL // 2, N_MEM)), blk((N_MEM // 2, D_MODEL))],
        out_shape=[
            jax.ShapeDtypeStruct((nb, KV_ROWS, LANES), F32),
            jax.ShapeDtypeStruct((nb, KV_ROWS, LANES), F32),
            jax.ShapeDtypeStruct((nb, D_MODEL // 2, N_MEM), U32),
            jax.ShapeDtypeStruct((nb, N_MEM // 2, D_MODEL), U32),
        ],
        compiler_params=_params(1),
        name="kv_proj",
    )(mem, xk_w, xv_w)


def _prompt_kernel(x_ref, kt_ref, v_ref, qs_ref, kc_ref, vc_ref,
                   w_in_ref, b_in_ref, conv_w_ref, conv_b_ref, gn_g_ref, gn_b_ref, gmat_ref,
                   pool_w_ref, pool_scale_ref, w_out_ref, ln1_g_ref, ln1_b_ref, xq_ref,
                   xo_ref, ln2_g_ref, ln2_b_ref, w1_ref, b1_ref, w2_ref, b2_ref, ln3_g_ref, ln3_b_ref,
                   y_ref, conv_new_ref, pool_new_ref, os_ref,
                   ubuf, pbuf, ybuf, dbuf, mixbuf, obuf, x1buf, x1prev, x2buf, hbuf, *, nt):
    tm = PROMPT_BLOCK
    step = pl.program_id(0)
    n_blocks = pl.num_programs(0) - 1
    t = jnp.minimum(step, n_blocks - 1) % nt

    @pl.when(t == 0)
    def _():
        ubuf[:, 0:CONV_PAD, :] = jnp.zeros((CONV_SLABS, CONV_PAD, LANES), F32)
        pbuf[:, 0:POOL_PAD, :] = jnp.zeros((POOL_SLABS, POOL_PAD, LANES), F32)

    @pl.when(t > 0)
    def _():
        ubuf[:, 0:CONV_PAD, :] = ubuf[:, tm:tm + CONV_PAD, :]
        pbuf[:, 0:POOL_PAD, :] = pbuf[:, tm:tm + POOL_PAD, :]

    @pl.when(step == 0)
    def _():
        x1buf[...] = jnp.zeros((tm, D_MODEL), F32)

    x = x_ref[0]
    h = _bdot(x, _unpack(w_in_ref[...])) + b_in_ref[...]
    u = h[:, 0:D_CONV] * jax.nn.sigmoid(h[:, D_CONV:2 * D_CONV])
    for s in range(CONV_SLABS):
        ubuf[s, CONV_PAD:CONV_PAD + tm, :] = u[:, s * LANES:(s + 1) * LANES]
    for s in range(POOL_SLABS):
        pbuf[s, POOL_PAD:POOL_PAD + tm, :] = h[:, 2 * D_CONV + s * LANES:2 * D_CONV + (s + 1) * LANES]

    x1prev[...] = _layer_norm(x1buf[...], ln1_g_ref[...], ln1_b_ref[...])
    q = (_bdot(x1prev[...], _unpack(xq_ref[...])) * ATTN_SCALE).astype(BF16)
    head = lambda hd: slice(hd * X_HEAD_DIM, (hd + 1) * X_HEAD_DIM)
    scores = [jnp.dot(q[:, head(hd)], _unpack(kt_ref[0, hd * X_HEAD_DIM // 2:(hd + 1) * X_HEAD_DIM // 2, :]),
                      preferred_element_type=F32) for hd in range(X_HEADS)]
    probs_dec = _sample_attn_probs(qs_ref, kc_ref)
    probs = []
    for s in scores:
        e = jnp.exp(s - jnp.max(s, axis=-1, keepdims=True))
        probs.append((e / jnp.sum(e, axis=-1, keepdims=True)).astype(BF16))
    for hd in range(X_HEADS):
        obuf[:, head(hd)] = jnp.dot(probs[hd], _unpack(v_ref[0, :, head(hd)]),
                                    preferred_element_type=F32).astype(BF16)

    base = CONV_PAD - CONV_CTX
    for s in range(CONV_SLABS):
        sl = slice(s * LANES, (s + 1) * LANES)
        for c in range(tm // CONV_ROWS):
            r0 = c * CONV_ROWS
            acc = jnp.broadcast_to(conv_b_ref[:, sl], (CONV_ROWS, LANES))
            for k in range(CONV_WIDTH):
                acc = acc + ubuf[s, base + r0 + k:base + r0 + k + CONV_ROWS, :] * conv_w_ref[k, :, sl]
            ybuf[r0:r0 + CONV_ROWS, sl] = acc

    pos = t * tm + lax.broadcasted_iota(jnp.int32, (tm, POOL_GROUP), 0)
    for gi, w in enumerate(POOL_WINDOWS):
        cur = pbuf[gi, POOL_PAD:POOL_PAD + tm, :]
        ws = cur
        for j in range(1, w):
            ws = ws + pbuf[gi, POOL_PAD - j:POOL_PAD - j + tm, :]
        cnt = jnp.minimum(pos + 1, w).astype(F32)
        dbuf[:, gi * POOL_GROUP:(gi + 1) * POOL_GROUP] = ws / cnt - cur

    attn = jnp.dot(obuf[...], _unpack(xo_ref[...]), preferred_element_type=F32)
    _sample_attn_values(probs_dec, vc_ref, os_ref, os_ref.shape[0] // vc_ref.shape[0])
    x2buf[...] = _layer_norm(DN_ALPHA * x1prev[...] + attn, ln2_g_ref[...], ln2_b_ref[...])
    hdn = jnp.maximum(_bdot(x2buf[...], _unpack(w1_ref[...])) + b1_ref[...], 0.0)
    hbuf[...] = (hdn * hdn).astype(BF16)

    _mixer_acts(ybuf, dbuf, mixbuf, gmat_ref, gn_g_ref, gn_b_ref, pool_w_ref, pool_scale_ref)

    f = jnp.dot(hbuf[...], _unpack(w2_ref[...]), preferred_element_type=F32) + b2_ref[...]
    x1buf[...] = _mixer_out(mixbuf, x, w_out_ref)
    y_ref[0] = _layer_norm(DN_ALPHA * x2buf[...] + f, ln3_g_ref[...], ln3_b_ref[...])

    @pl.when(jnp.logical_and(t == nt - 1, step < n_blocks))
    def _():
        for s in range(CONV_SLABS):
            conv_new_ref[0, :, s * LANES:(s + 1) * LANES] = ubuf[s, CONV_PAD + tm - CONV_CTX:CONV_PAD + tm, :]
        for s in range(POOL_SLABS):
            pool_new_ref[0, :, s * LANES:(s + 1) * LANES] = pbuf[s, POOL_PAD + tm - POOL_CTX:POOL_PAD + tm, :]


def _prompt_call(x, kt, vb, q_dec, k_dec, v_dec, dec_seq, weights):
    nb, seq, _ = x.shape
    tm = PROMPT_BLOCK
    nt = seq // tm
    n_blocks = nb * nt
    n_dec = k_dec.shape[0]
    dec_per_step = -(-n_dec // n_blocks)
    dec_steps = n_dec // dec_per_step
    dec_rows = dec_per_step * dec_seq
    assert dec_steps * dec_per_step == n_dec and dec_steps <= n_blocks + 1
    assert dec_rows % SUBLANES == 0 and SAMPLE_ATTN_ROWS % dec_rows == 0
    cur = lambda g: jnp.minimum(g, n_blocks - 1)
    prev = lambda g: jnp.maximum(g - 1, 0)
    dec = lambda g: jnp.minimum(g, dec_steps - 1)
    in_specs = [
        pl.BlockSpec((1, tm, D_MODEL), lambda g: (cur(g) // nt, cur(g) % nt, 0)),
        pl.BlockSpec((1, D_MODEL // 2, N_MEM), lambda g: (prev(g) // nt, 0, 0)),
        pl.BlockSpec((1, N_MEM // 2, D_MODEL), lambda g: (prev(g) // nt, 0, 0)),
        pl.BlockSpec((dec_rows, D_MODEL), lambda g: (dec(g), 0)),
        pl.BlockSpec((dec_per_step, KV_ROWS, LANES), lambda g: (dec(g), 0, 0)),
        pl.BlockSpec((dec_per_step, KV_ROWS, LANES), lambda g: (dec(g), 0, 0)),
    ] + [_const_spec(w.shape) for w in weights]
    out_specs = [
        pl.BlockSpec((1, tm, D_MODEL), lambda g: (prev(g) // nt, prev(g) % nt, 0)),
        pl.BlockSpec((1, CONV_CTX, D_CONV), lambda g: (cur(g) // nt, 0, 0)),
        pl.BlockSpec((1, POOL_CTX, D_POOL), lambda g: (cur(g) // nt, 0, 0)),
        pl.BlockSpec((dec_rows, D_MODEL), lambda g: (dec(g), 0)),
    ]
    out_shape = [
        jax.ShapeDtypeStruct((nb, seq, D_MODEL), F32),
        jax.ShapeDtypeStruct((nb, CONV_CTX, D_CONV), F32),
        jax.ShapeDtypeStruct((nb, POOL_CTX, D_POOL), F32),
        jax.ShapeDtypeStruct(q_dec.shape, F32),
    ]
    scratch = [
        pltpu.VMEM((CONV_SLABS, CONV_PAD + tm, LANES), F32),
        pltpu.VMEM((POOL_SLABS, POOL_PAD + tm, LANES), F32),
        pltpu.VMEM((tm, D_CONV), F32),
        pltpu.VMEM((tm, D_POOL), F32),
        pltpu.VMEM((tm, D_MODEL), BF16),
        pltpu.VMEM((tm, D_MODEL), BF16),
        pltpu.VMEM((tm, D_MODEL), F32),
        pltpu.VMEM((tm, D_MODEL), F32),
        pltpu.VMEM((tm, D_MODEL), F32),
        pltpu.VMEM((tm, D_FF), BF16),
    ]
    return pl.pallas_call(
        functools.partial(_prompt_kernel, nt=nt),
        grid=(n_blocks + 1,),
        in_specs=in_specs,
        out_specs=out_specs,
        out_shape=out_shape,
        scratch_shapes=scratch,
        compiler_params=_params(1),
        name="prompt_layer",
    )(x, kt, vb, q_dec, k_dec, v_dec, *weights)


def _sample_mixer_kernel(x_ref, sconv_ref, spool_ref,
                         w_in_ref, b_in_ref, conv_w_ref, conv_b_ref, gn_g_ref, gn_b_ref, gmat_ref,
                         pool_w_ref, pool_scale_ref, w_out_ref, ln1_g_ref, ln1_b_ref, xq_ref,
                         x1_ref, q_ref, conv_new_ref, pool_new_ref,
                         hbuf, yslab, dslab, ybuf, dbuf, mixbuf, *, dec_seq):
    gb = SAMPLE_MIX_BATCH
    ts = dec_seq
    x = x_ref[...]
    h = _bdot(x, _unpack(w_in_ref[...])) + b_in_ref[...]
    u = h[:, 0:D_CONV] * jax.nn.sigmoid(h[:, D_CONV:2 * D_CONV])
    for s in range(CONV_SLABS):
        hbuf[s] = u[:, s * LANES:(s + 1) * LANES]
    for s in range(POOL_SLABS):
        hbuf[CONV_SLABS + s] = h[:, 2 * D_CONV + s * LANES:2 * D_CONV + (s + 1) * LANES]

    def step_rows(slab, t):
        return hbuf[slab, pl.ds(t, gb, stride=ts), :]

    for s in range(CONV_SLABS):
        sl = slice(s * LANES, (s + 1) * LANES)
        new = [step_rows(s, t) for t in range(ts)]
        ext = lambda j: sconv_ref[j, :, sl] if j < CONV_CTX else new[j - CONV_CTX]
        for j in range(CONV_CTX - ts):
            conv_new_ref[j, :, sl] = sconv_ref[j + ts, :, sl]
        for t in range(ts):
            conv_new_ref[CONV_CTX - ts + t, :, sl] = new[t]
            acc = jnp.broadcast_to(conv_b_ref[:, sl], (gb, LANES))
            for k in range(CONV_WIDTH):
                acc = acc + ext(t + k) * conv_w_ref[k, :, sl]
            yslab[s, pl.ds(t, gb, stride=ts), :] = acc

    for g, w in enumerate(POOL_WINDOWS):
        sl = slice(g * LANES, (g + 1) * LANES)
        new = [step_rows(CONV_SLABS + g, t) for t in range(ts)]
        ext = lambda j: spool_ref[j, :, sl] if j < POOL_CTX else new[j - POOL_CTX]
        for j in range(POOL_CTX - ts):
            pool_new_ref[j, :, sl] = spool_ref[j + ts, :, sl]
        for t in range(ts):
            pool_new_ref[POOL_CTX - ts + t, :, sl] = new[t]
            ws = new[t]
            for j in range(1, w):
                ws = ws + ext(POOL_CTX + t - j)
            cnt = float(min(PAST_LEN + t + 1, w))
            dslab[g, pl.ds(t, gb, stride=ts), :] = ws / cnt - new[t]

    for s in range(CONV_SLABS):
        ybuf[:, s * LANES:(s + 1) * LANES] = yslab[s]
    for s in range(POOL_SLABS):
        dbuf[:, s * LANES:(s + 1) * LANES] = dslab[s]

    _mixer_acts(ybuf, dbuf, mixbuf, gmat_ref, gn_g_ref, gn_b_ref, pool_w_ref, pool_scale_ref)
    x1 = _layer_norm(_mixer_out(mixbuf, x, w_out_ref), ln1_g_ref[...], ln1_b_ref[...])
    x1_ref[...] = x1
    q_ref[...] = _bdot(x1, _unpack(xq_ref[...])) * ATTN_SCALE


def _sample_mixer_call(xs, sconv_tm, spool_tm, weights, dec_seq):
    n_tok = xs.shape[0]
    nb = n_tok // dec_seq
    gb = SAMPLE_MIX_BATCH
    rows = gb * dec_seq
    row_spec = pl.BlockSpec((rows, D_MODEL), lambda i: (i, 0))
    conv_spec = pl.BlockSpec((CONV_CTX, gb, D_CONV), lambda i: (0, i, 0))
    pool_spec = pl.BlockSpec((POOL_CTX, gb, D_POOL), lambda i: (0, i, 0))
    out_shape = [
        jax.ShapeDtypeStruct((n_tok, D_MODEL), F32),
        jax.ShapeDtypeStruct((n_tok, D_MODEL), F32),
        jax.ShapeDtypeStruct((CONV_CTX, nb, D_CONV), F32),
        jax.ShapeDtypeStruct((POOL_CTX, nb, D_POOL), F32),
    ]
    scratch = [
        pltpu.VMEM((CONV_SLABS + POOL_SLABS, rows, LANES), F32),
        pltpu.VMEM((CONV_SLABS, rows, LANES), F32),
        pltpu.VMEM((POOL_SLABS, rows, LANES), F32),
        pltpu.VMEM((rows, D_CONV), F32),
        pltpu.VMEM((rows, D_POOL), F32),
        pltpu.VMEM((rows, D_MODEL), BF16),
    ]
    return pl.pallas_call(
        functools.partial(_sample_mixer_kernel, dec_seq=dec_seq),
        grid=(nb // gb,),
        in_specs=[row_spec, conv_spec, pool_spec] + [_const_spec(w.shape) for w in weights],
        out_specs=[row_spec, row_spec, conv_spec, pool_spec],
        out_shape=out_shape,
        scratch_shapes=scratch,
        compiler_params=_params(1),
        name="sample_mixer",
    )(xs, sconv_tm, spool_tm, *weights)


def _load_head(ref, b, hd):
    parts = [ref[b, pl.ds(c * X_HEADS + hd, N_MEM, stride=KV_ROW_STRIDE), :] for c in range(HEAD_CHUNKS)]
    return jnp.concatenate(parts, axis=1).astype(BF16)


def _sample_attn_probs(q_ref, k_ref):
    q8 = q_ref[...]
    q = jnp.concatenate([q8] * (SAMPLE_ATTN_ROWS // q8.shape[0]), axis=0).astype(BF16)
    s = jnp.concatenate(
        [lax.dot_general(q[:, hd * X_HEAD_DIM:(hd + 1) * X_HEAD_DIM], _load_head(k_ref, b, hd),
                         (((1,), (1,)), ((), ())), preferred_element_type=F32)
         for b in range(k_ref.shape[0]) for hd in range(X_HEADS)], axis=0)
    e = jnp.exp(s - jnp.max(s, axis=-1, keepdims=True))
    return (e / jnp.sum(e, axis=-1, keepdims=True)).astype(BF16)


def _sample_attn_values(a, v_ref, o_ref, ts):
    for b in range(v_ref.shape[0]):
        for hd in range(X_HEADS):
            i = b * X_HEADS + hd
            o = jnp.dot(a[i * SAMPLE_ATTN_ROWS:(i + 1) * SAMPLE_ATTN_ROWS, :], _load_head(v_ref, b, hd),
                        preferred_element_type=F32)
            o_ref[b * ts:(b + 1) * ts, hd * X_HEAD_DIM:(hd + 1) * X_HEAD_DIM] = o[b * ts:(b + 1) * ts, :]


def _sample_tail_kernel(x1_ref, o_ref, xo_ref, ln2_g_ref, ln2_b_ref, w1_ref, b1_ref, w2_ref, b2_ref,
                        ln3_g_ref, ln3_b_ref, y_ref, x2buf, x2bf, fbuf):
    c = pl.program_id(0)

    @pl.when(c == 0)
    def _():
        attn = _bdot(o_ref[...], _unpack(xo_ref[...]))
        x2 = _layer_norm(DN_ALPHA * x1_ref[...] + attn, ln2_g_ref[...], ln2_b_ref[...])
        x2buf[...] = x2
        x2bf[...] = x2.astype(BF16)
        fbuf[...] = jnp.broadcast_to(b2_ref[...], fbuf.shape)

    hdn = jnp.maximum(jnp.dot(x2bf[...], _unpack(w1_ref[...]), preferred_element_type=F32) + b1_ref[...], 0.0)
    fbuf[...] += _bdot(hdn * hdn, _unpack(w2_ref[...]))

    @pl.when(c == pl.num_programs(0) - 1)
    def _():
        y_ref[...] = _layer_norm(DN_ALPHA * x2buf[...] + fbuf[...], ln3_g_ref[...], ln3_b_ref[...])


def _sample_tail_call(x1, o, weights):
    xo_p, ln2_g, ln2_b, w1_p, b1, w2_p, b2, ln3_g, ln3_b = weights
    n_tok = x1.shape[0]
    fc = TAIL_FF_CHUNK
    in_specs = [
        _const_spec(x1.shape), _const_spec(o.shape), _const_spec(xo_p.shape),
        _const_spec(ln2_g.shape), _const_spec(ln2_b.shape),
        pl.BlockSpec((D_MODEL // 2, fc), lambda c: (0, c)),
        pl.BlockSpec((1, fc), lambda c: (0, c)),
        pl.BlockSpec((fc // 2, D_MODEL), lambda c: (c, 0)),
        _const_spec(b2.shape), _const_spec(ln3_g.shape), _const_spec(ln3_b.shape),
    ]
    return pl.pallas_call(
        _sample_tail_kernel,
        grid=(D_FF // fc,),
        in_specs=in_specs,
        out_specs=pl.BlockSpec((n_tok, D_MODEL), lambda c: (0, 0)),
        out_shape=jax.ShapeDtypeStruct((n_tok, D_MODEL), F32),
        scratch_shapes=[pltpu.VMEM((n_tok, D_MODEL), F32), pltpu.VMEM((n_tok, D_MODEL), BF16),
                        pltpu.VMEM((n_tok, D_MODEL), F32)],
        compiler_params=_params(1),
        name="sample_tail",
    )(x1, o, *weights)


def _group_indicator():
    gid = jnp.arange(GN_HALF) // CONV_HEAD_DIM
    return (gid[:, None] == gid[None, :]).astype(F32)


def _to_head_split_view(a):
    lead = a.shape[:-3]
    n = len(lead)
    a = a.reshape(lead + (N_MEM, X_HEADS, HEAD_CHUNKS, LANES))
    a = a.transpose(tuple(range(n)) + (n, n + 2, n + 1, n + 3))
    return a.reshape(lead + (KV_ROWS, LANES))


def _from_head_split_view(a):
    lead = a.shape[:-2]
    n = len(lead)
    a = a.reshape(lead + (N_MEM, HEAD_CHUNKS, X_HEADS, LANES))
    a = a.transpose(tuple(range(n)) + (n, n + 2, n + 1, n + 3))
    return a.reshape(lead + (N_MEM, X_HEADS, X_HEAD_DIM))


def kernel(x_prompt, x_sample, mem_prompt, cache_mem_k, cache_mem_v, state_conv, state_pool, w_in, b_in, conv_w, conv_b, gn_g, gn_b, pool_w, pool_scale, w_out, ln1_g, ln1_b, xq_w, xk_w, xv_w, xo_w, ln2_g, ln2_b, w1, b1, w2, b2, ln3_g, ln3_b):
    assert w_in.shape[0] == DEPTH == 1
    n_prompt, seq, _ = x_prompt.shape
    n_dec, dec_seq, _ = x_sample.shape
    assert seq % PROMPT_BLOCK == 0 and n_dec % SAMPLE_MIX_BATCH == 0

    w_in_p, w_out_p, xq_p, xo_p, xk_p, xv_p, pool_w_p, gmat_p, w1_p, w2_p = _cast_call(
        [w_in[0], w_out[0], xq_w[0], xo_w[0], xk_w[0], xv_w[0],
         pool_w[0].reshape(len(POOL_WINDOWS) * POOL_GROUP, POOL_GROUP), _group_indicator(), w1[0], w2[0]])

    row = lambda a: a[0].reshape(1, -1)
    mixer_w = (w_in_p, row(b_in), conv_w.transpose(1, 0, 2), row(conv_b), row(gn_g), row(gn_b), gmat_p, pool_w_p,
               row(pool_scale), w_out_p, row(ln1_g), row(ln1_b), xq_p)
    tail_w = (xo_p, row(ln2_g), row(ln2_b), w1_p, row(b1), w2_p, row(b2), row(ln3_g), row(ln3_b))

    xs = x_sample.reshape(n_dec * dec_seq, D_MODEL)
    x1_s, q_s, conv_s, pool_s = _sample_mixer_call(
        xs, state_conv[0].transpose(1, 0, 2), state_pool[0].transpose(1, 0, 2), mixer_w, dec_seq)

    k_p, v_p, kt_p, vb_p = _kv_proj_call(mem_prompt, xk_p, xv_p)
    y_p, conv_p, pool_p, o_s = _prompt_call(
        x_prompt, kt_p, vb_p, q_s, _to_head_split_view(cache_mem_k[0]), _to_head_split_view(cache_mem_v[0]),
        dec_seq, mixer_w + tail_w)

    y_s = _sample_tail_call(x1_s, o_s, tail_w)

    return (y_p, y_s.reshape(n_dec, dec_seq, D_MODEL),
            _from_head_split_view(k_p)[None], _from_head_split_view(v_p)[None],
            conv_p[None], conv_s.transpose(1, 0, 2)[None], pool_p[None], pool_s.transpose(1, 0, 2)[None])
```

```python
import functools

import jax
import jax.numpy as jnp
from jax import lax
from jax.experimental import pallas as pl
from jax.experimental.pallas import tpu as pltpu

F32 = jnp.float32
BF16 = jnp.bfloat16
U32 = jnp.uint32

LANES = 128
SUBLANES = 8
D_MODEL = 1024
D_CONV = 512
D_POOL = 512
D_IN = 2 * D_CONV + D_POOL
CONV_WIDTH = 31
CONV_CTX = CONV_WIDTH - 1
CONV_HEADS = 8
CONV_HEAD_DIM = D_CONV // CONV_HEADS
POOL_WINDOWS = (2, 4, 8, 16)
POOL_GROUP = D_POOL // len(POOL_WINDOWS)
POOL_CTX = max(POOL_WINDOWS) - 1
N_MEM = 256
X_HEADS = 4
X_HEAD_DIM = D_MODEL // X_HEADS
HEAD_CHUNKS = X_HEAD_DIM // LANES
KV_ROWS = N_MEM * X_HEADS * HEAD_CHUNKS
KV_ROW_STRIDE = X_HEADS * HEAD_CHUNKS
D_FF = 4 * D_MODEL
LN_EPS = 1e-5
DEPTH = 1
PAST_LEN = 16384
DN_ALPHA = (2.0 * DEPTH) ** 0.25
ATTN_SCALE = X_HEAD_DIM ** -0.5

assert POOL_GROUP == LANES and X_HEAD_DIM % LANES == 0

CONV_SLABS = D_CONV // LANES
POOL_SLABS = D_POOL // LANES
CONV_PAD = 32
POOL_PAD = 16
GN_HALF = 256

PROMPT_BLOCK = 256
CONV_ROWS = 32
SAMPLE_MIX_BATCH = 32
SAMPLE_ATTN_ROWS = 16
TAIL_FF_CHUNK = 1024
CAST_STEPS = 8
VMEM_LIMIT = 56 * 1024 * 1024


def _unpack(w_u32):
    return pltpu.bitcast(w_u32, BF16)


def _pack(w_bf16):
    return pltpu.bitcast(w_bf16, U32)


def _bdot(a, w):
    return jnp.dot(a.astype(BF16), w, preferred_element_type=F32)


def _layer_norm(x, g, b):
    mu = jnp.mean(x, axis=-1, keepdims=True)
    d = x - mu
    var = jnp.mean(d * d, axis=-1, keepdims=True)
    return d * lax.rsqrt(var + LN_EPS) * g + b


def _group_norm_swish(y, gmat, gn_g, gn_b):
    inv = 1.0 / CONV_HEAD_DIM
    mu = _bdot(y, gmat) * inv
    d = y - mu
    var = _bdot(d * d, gmat) * inv
    n = d * lax.rsqrt(var + LN_EPS) * gn_g + gn_b
    return n * jax.nn.sigmoid(n)


def _mixer_acts(ybuf, dbuf, mixbuf, gmat_ref, gn_g_ref, gn_b_ref, pool_w_ref, pool_scale_ref):
    gmat = _unpack(gmat_ref[...])
    for j in range(D_CONV // GN_HALF):
        sl = slice(j * GN_HALF, (j + 1) * GN_HALF)
        o = _group_norm_swish(ybuf[:, sl], gmat, gn_g_ref[:, sl], gn_b_ref[:, sl])
        mixbuf[:, sl] = o.astype(BF16)
    half = POOL_GROUP // 2
    for g in range(len(POOL_WINDOWS)):
        sl = slice(g * POOL_GROUP, (g + 1) * POOL_GROUP)
        pw = _unpack(pool_w_ref[g * half:(g + 1) * half, :])
        mixed = _bdot(dbuf[:, sl], pw) * pool_scale_ref[:, sl]
        mixbuf[:, D_CONV + g * POOL_GROUP:D_CONV + (g + 1) * POOL_GROUP] = mixed.astype(BF16)


def _mixer_out(mixbuf, x, w_out_ref):
    return DN_ALPHA * x + jnp.dot(mixbuf[...], _unpack(w_out_ref[...]), preferred_element_type=F32)


def _const_spec(shape):
    nd = len(shape)
    return pl.BlockSpec(shape, lambda *_: (0,) * nd, pipeline_mode=pl.Buffered(1))


def _params(n_axes):
    return pltpu.CompilerParams(dimension_semantics=("arbitrary",) * n_axes, vmem_limit_bytes=VMEM_LIMIT)


VEC_FIELDS = (("b1", D_FF), ("b_in", D_IN), ("conv_b", D_CONV), ("gn_g", D_CONV), ("gn_b", D_CONV),
              ("pool_scale", D_POOL), ("ln1_g", D_MODEL), ("ln1_b", D_MODEL), ("ln2_g", D_MODEL),
              ("ln2_b", D_MODEL), ("b2", D_MODEL), ("ln3_g", D_MODEL), ("ln3_b", D_MODEL))
VEC_LEN = sum(n for _, n in VEC_FIELDS)
WA_FIELDS = (("w_in", D_IN), ("w_out", D_MODEL), ("xq", D_MODEL))
WB_FIELDS = (("xo", D_MODEL), ("w1", D_FF))


def _views(ref, fields):
    out, off = {}, 0
    for name, n in fields:
        out[name] = ref.at[:, off:off + n]
        off += n
    return out


def _cast_kernel(*refs, widths, n_vecs):
    n_mats = sum(len(g) for g in widths)
    srcs, vec_srcs = refs[:n_mats], refs[n_mats:n_mats + n_vecs]
    dsts, vec_dst = refs[n_mats + n_vecs:-1], refs[-1]
    i = 0
    for dst, group in zip(dsts, widths):
        off = 0
        for n in group:
            dst[:, off:off + n] = _pack(srcs[i][...].astype(BF16))
            off += n
            i += 1
    @pl.when(pl.program_id(0) == 0)
    def _():
        off = 0
        for src in vec_srcs:
            n = src.shape[1]
            vec_dst[:, off:off + n] = src[...]
            off += n


def _cast_call(groups, vectors):
    in_specs, out_specs, out_shape, widths = [], [], [], []
    for group in groups:
        k = group[0].shape[0]
        rows = k // CAST_STEPS
        assert rows * CAST_STEPS == k and rows % 16 == 0 and all(w.shape[0] == k for w in group)
        assert all(w.shape[1] % LANES == 0 for w in group)
        widths.append(tuple(w.shape[1] for w in group))
        total = sum(widths[-1])
        in_specs += [pl.BlockSpec((rows, w.shape[1]), lambda i: (i, 0)) for w in group]
        out_specs.append(pl.BlockSpec((rows // 2, total), lambda i: (i, 0)))
        out_shape.append(jax.ShapeDtypeStruct((k // 2, total), U32))
    assert all(v.shape[1] % LANES == 0 for v in vectors)
    vec_len = sum(v.shape[1] for v in vectors)
    in_specs += [pl.BlockSpec(v.shape, lambda i: (0, 0)) for v in vectors]
    out_specs.append(pl.BlockSpec((1, vec_len), lambda i: (0, 0)))
    out_shape.append(jax.ShapeDtypeStruct((1, vec_len), F32))
    return pl.pallas_call(
        functools.partial(_cast_kernel, widths=tuple(widths), n_vecs=len(vectors)), grid=(CAST_STEPS,),
        in_specs=in_specs, out_specs=out_specs, out_shape=out_shape, compiler_params=_params(1),
        name="cast_weights",
    )(*[w for group in groups for w in group], *vectors)


def _store_head_split(dst_ref, val):
    for hd in range(X_HEADS):
        for c in range(HEAD_CHUNKS):
            col = hd * X_HEAD_DIM + c * LANES
            dst_ref[0, pl.ds(c * X_HEADS + hd, N_MEM, stride=KV_ROW_STRIDE), :] = val[:, col:col + LANES]


def _kv_proj_kernel(mem_ref, wkv_ref, k_ref, v_ref, kt_ref, vb_ref):
    m = mem_ref[0].astype(BF16)
    k = jnp.dot(m, _unpack(wkv_ref[:, 0:D_MODEL]), preferred_element_type=F32)
    v = jnp.dot(m, _unpack(wkv_ref[:, D_MODEL:2 * D_MODEL]), preferred_element_type=F32)
    _store_head_split(k_ref, k)
    _store_head_split(v_ref, v)
    kt_ref[0] = _pack(k.T.astype(BF16))
    vb_ref[0] = _pack(v.astype(BF16))


def _kv_proj_call(mem, wkv):
    nb = mem.shape[0]
    blk = lambda s: pl.BlockSpec((1,) + s, lambda b: (b, 0, 0))
    return pl.pallas_call(
        _kv_proj_kernel,
        grid=(nb,),
        in_specs=[blk((N_MEM, D_MODEL)), _const_spec(wkv.shape)],
        out_specs=[blk((KV_ROWS, LANES)), blk((KV_ROWS, LANES)),
                   blk((D_MODEL // 2, N_MEM)), blk((N_MEM // 2, D_MODEL))],
        out_shape=[
            jax.ShapeDtypeStruct((nb, KV_ROWS, LANES), F32),
            jax.ShapeDtypeStruct((nb, KV_ROWS, LANES), F32),
            jax.ShapeDtypeStruct((nb, D_MODEL // 2, N_MEM), U32),
            jax.ShapeDtypeStruct((nb, N_MEM // 2, D_MODEL), U32),
        ],
        compiler_params=_params(1),
        name="kv_proj",
    )(mem, wkv)


def _prompt_kernel(x_ref, kt_ref, v_ref, qs_ref, kc_ref, vc_ref,
                   wa_ref, wb_ref, w2_ref, pool_w_ref, gmat_ref, conv_w_ref, vec_ref,
                   y_ref, conv_new_ref, pool_new_ref, os_ref,
                   ubuf, pbuf, ybuf, dbuf, mixbuf, obuf, x1buf, x1prev, x2buf, hbuf, *, nt):
    wa, wb, vec = _views(wa_ref, WA_FIELDS), _views(wb_ref, WB_FIELDS), _views(vec_ref, VEC_FIELDS)
    w_in_ref, w_out_ref, xq_ref, xo_ref, w1_ref = wa["w_in"], wa["w_out"], wa["xq"], wb["xo"], wb["w1"]
    b_in_ref, conv_b_ref, gn_g_ref, gn_b_ref = vec["b_in"], vec["conv_b"], vec["gn_g"], vec["gn_b"]
    pool_scale_ref, ln1_g_ref, ln1_b_ref = vec["pool_scale"], vec["ln1_g"], vec["ln1_b"]
    ln2_g_ref, ln2_b_ref, b1_ref, b2_ref = vec["ln2_g"], vec["ln2_b"], vec["b1"], vec["b2"]
    ln3_g_ref, ln3_b_ref = vec["ln3_g"], vec["ln3_b"]
    tm = PROMPT_BLOCK
    step = pl.program_id(0)
    n_blocks = pl.num_programs(0) - 1
    t = jnp.minimum(step, n_blocks - 1) % nt

    @pl.when(t == 0)
    def _():
        ubuf[:, 0:CONV_PAD, :] = jnp.zeros((CONV_SLABS, CONV_PAD, LANES), F32)
        pbuf[:, 0:POOL_PAD, :] = jnp.zeros((POOL_SLABS, POOL_PAD, LANES), F32)

    @pl.when(t > 0)
    def _():
        ubuf[:, 0:CONV_PAD, :] = ubuf[:, tm:tm + CONV_PAD, :]
        pbuf[:, 0:POOL_PAD, :] = pbuf[:, tm:tm + POOL_PAD, :]

    @pl.when(step == 0)
    def _():
        x1buf[...] = jnp.zeros((tm, D_MODEL), F32)

    x = x_ref[0]
    h = _bdot(x, _unpack(w_in_ref[...])) + b_in_ref[...]
    u = h[:, 0:D_CONV] * jax.nn.sigmoid(h[:, D_CONV:2 * D_CONV])
    for s in range(CONV_SLABS):
        ubuf[s, CONV_PAD:CONV_PAD + tm, :] = u[:, s * LANES:(s + 1) * LANES]
    for s in range(POOL_SLABS):
        pbuf[s, POOL_PAD:POOL_PAD + tm, :] = h[:, 2 * D_CONV + s * LANES:2 * D_CONV + (s + 1) * LANES]

    x1prev[...] = _layer_norm(x1buf[...], ln1_g_ref[...], ln1_b_ref[...])
    q = (_bdot(x1prev[...], _unpack(xq_ref[...])) * ATTN_SCALE).astype(BF16)
    head = lambda hd: slice(hd * X_HEAD_DIM, (hd + 1) * X_HEAD_DIM)
    scores = [jnp.dot(q[:, head(hd)], _unpack(kt_ref[0, hd * X_HEAD_DIM // 2:(hd + 1) * X_HEAD_DIM // 2, :]),
                      preferred_element_type=F32) for hd in range(X_HEADS)]
    probs_dec = _sample_attn_probs(qs_ref, kc_ref)
    probs = []
    for s in scores:
        e = jnp.exp(s - jnp.max(s, axis=-1, keepdims=True))
        probs.append((e / jnp.sum(e, axis=-1, keepdims=True)).astype(BF16))
    for hd in range(X_HEADS):
        obuf[:, head(hd)] = jnp.dot(probs[hd], _unpack(v_ref[0, :, head(hd)]),
                                    preferred_element_type=F32).astype(BF16)

    base = CONV_PAD - CONV_CTX
    for s in range(CONV_SLABS):
        sl = slice(s * LANES, (s + 1) * LANES)
        for c in range(tm // CONV_ROWS):
            r0 = c * CONV_ROWS
            acc = jnp.broadcast_to(conv_b_ref[:, sl], (CONV_ROWS, LANES))
            for k in range(CONV_WIDTH):
                acc = acc + ubuf[s, base + r0 + k:base + r0 + k + CONV_ROWS, :] * conv_w_ref[k, :, sl]
            ybuf[r0:r0 + CONV_ROWS, sl] = acc

    pos = t * tm + lax.broadcasted_iota(jnp.int32, (tm, POOL_GROUP), 0)
    for gi, w in enumerate(POOL_WINDOWS):
        cur = pbuf[gi, POOL_PAD:POOL_PAD + tm, :]
        ws = cur
        for j in range(1, w):
            ws = ws + pbuf[gi, POOL_PAD - j:POOL_PAD - j + tm, :]
        cnt = jnp.minimum(pos + 1, w).astype(F32)
        dbuf[:, gi * POOL_GROUP:(gi + 1) * POOL_GROUP] = ws / cnt - cur

    attn = jnp.dot(obuf[...], _unpack(xo_ref[...]), preferred_element_type=F32)
    _sample_attn_values(probs_dec, vc_ref, os_ref, os_ref.shape[0] // vc_ref.shape[0])
    x2buf[...] = _layer_norm(DN_ALPHA * x1prev[...] + attn, ln2_g_ref[...], ln2_b_ref[...])
    hdn = jnp.maximum(_bdot(x2buf[...], _unpack(w1_ref[...])) + b1_ref[...], 0.0)
    hbuf[...] = (hdn * hdn).astype(BF16)

    _mixer_acts(ybuf, dbuf, mixbuf, gmat_ref, gn_g_ref, gn_b_ref, pool_w_ref, pool_scale_ref)

    f = jnp.dot(hbuf[...], _unpack(w2_ref[...]), preferred_element_type=F32) + b2_ref[...]
    x1buf[...] = _mixer_out(mixbuf, x, w_out_ref)
    y_ref[0] = _layer_norm(DN_ALPHA * x2buf[...] + f, ln3_g_ref[...], ln3_b_ref[...])

    @pl.when(jnp.logical_and(t == nt - 1, step < n_blocks))
    def _():
        for s in range(CONV_SLABS):
            conv_new_ref[0, :, s * LANES:(s + 1) * LANES] = ubuf[s, CONV_PAD + tm - CONV_CTX:CONV_PAD + tm, :]
        for s in range(POOL_SLABS):
            pool_new_ref[0, :, s * LANES:(s + 1) * LANES] = pbuf[s, POOL_PAD + tm - POOL_CTX:POOL_PAD + tm, :]


def _prompt_call(x, kt, vb, q_dec, k_dec, v_dec, dec_seq, weights):
    nb, seq, _ = x.shape
    tm = PROMPT_BLOCK
    nt = seq // tm
    n_blocks = nb * nt
    n_dec = k_dec.shape[0]
    dec_per_step = -(-n_dec // n_blocks)
    dec_steps = n_dec // dec_per_step
    dec_rows = dec_per_step * dec_seq
    assert dec_steps * dec_per_step == n_dec and dec_steps <= n_blocks + 1
    assert dec_rows % SUBLANES == 0 and SAMPLE_ATTN_ROWS % dec_rows == 0
    cur = lambda g: jnp.minimum(g, n_blocks - 1)
    prev = lambda g: jnp.maximum(g - 1, 0)
    dec = lambda g: jnp.minimum(g, dec_steps - 1)
    in_specs = [
        pl.BlockSpec((1, tm, D_MODEL), lambda g: (cur(g) // nt, cur(g) % nt, 0)),
        pl.BlockSpec((1, D_MODEL // 2, N_MEM), lambda g: (prev(g) // nt, 0, 0)),
        pl.BlockSpec((1, N_MEM // 2, D_MODEL), lambda g: (prev(g) // nt, 0, 0)),
        pl.BlockSpec((dec_rows, D_MODEL), lambda g: (dec(g), 0)),
        pl.BlockSpec((dec_per_step, KV_ROWS, LANES), lambda g: (dec(g), 0, 0)),
        pl.BlockSpec((dec_per_step, KV_ROWS, LANES), lambda g: (dec(g), 0, 0)),
    ] + [_const_spec(w.shape) for w in weights]
    out_specs = [
        pl.BlockSpec((1, tm, D_MODEL), lambda g: (prev(g) // nt, prev(g) % nt, 0)),
        pl.BlockSpec((1, CONV_CTX, D_CONV), lambda g: (cur(g) // nt, 0, 0)),
        pl.BlockSpec((1, POOL_CTX, D_POOL), lambda g: (cur(g) // nt, 0, 0)),
        pl.BlockSpec((dec_rows, D_MODEL), lambda g: (dec(g), 0)),
    ]
    out_shape = [
        jax.ShapeDtypeStruct((nb, seq, D_MODEL), F32),
        jax.ShapeDtypeStruct((nb, CONV_CTX, D_CONV), F32),
        jax.ShapeDtypeStruct((nb, POOL_CTX, D_POOL), F32),
        jax.ShapeDtypeStruct(q_dec.shape, F32),
    ]
    scratch = [
        pltpu.VMEM((CONV_SLABS, CONV_PAD + tm, LANES), F32),
        pltpu.VMEM((POOL_SLABS, POOL_PAD + tm, LANES), F32),
        pltpu.VMEM((tm, D_CONV), F32),
        pltpu.VMEM((tm, D_POOL), F32),
        pltpu.VMEM((tm, D_MODEL), BF16),
        pltpu.VMEM((tm, D_MODEL), BF16),
        pltpu.VMEM((tm, D_MODEL), F32),
        pltpu.VMEM((tm, D_MODEL), F32),
        pltpu.VMEM((tm, D_MODEL), F32),
        pltpu.VMEM((tm, D_FF), BF16),
    ]
    return pl.pallas_call(
        functools.partial(_prompt_kernel, nt=nt),
        grid=(n_blocks + 1,),
        in_specs=in_specs,
        out_specs=out_specs,
        out_shape=out_shape,
        scratch_shapes=scratch,
        compiler_params=_params(1),
        name="prompt_layer",
    )(x, kt, vb, q_dec, k_dec, v_dec, *weights)


def _sample_mixer_kernel(x_ref, sconv_ref, spool_ref,
                         wa_ref, pool_w_ref, gmat_ref, conv_w_ref, vec_ref,
                         x1_ref, q_ref, conv_new_ref, pool_new_ref,
                         hbuf, yslab, dslab, ybuf, dbuf, mixbuf, *, dec_seq):
    wa, vec = _views(wa_ref, WA_FIELDS), _views(vec_ref, VEC_FIELDS)
    w_in_ref, w_out_ref, xq_ref = wa["w_in"], wa["w_out"], wa["xq"]
    b_in_ref, conv_b_ref, gn_g_ref, gn_b_ref = vec["b_in"], vec["conv_b"], vec["gn_g"], vec["gn_b"]
    pool_scale_ref, ln1_g_ref, ln1_b_ref = vec["pool_scale"], vec["ln1_g"], vec["ln1_b"]
    gb = SAMPLE_MIX_BATCH
    ts = dec_seq
    x = x_ref[...]
    h = _bdot(x, _unpack(w_in_ref[...])) + b_in_ref[...]
    u = h[:, 0:D_CONV] * jax.nn.sigmoid(h[:, D_CONV:2 * D_CONV])
    for s in range(CONV_SLABS):
        hbuf[s] = u[:, s * LANES:(s + 1) * LANES]
    for s in range(POOL_SLABS):
        hbuf[CONV_SLABS + s] = h[:, 2 * D_CONV + s * LANES:2 * D_CONV + (s + 1) * LANES]

    def step_rows(slab, t):
        return hbuf[slab, pl.ds(t, gb, stride=ts), :]

    for s in range(CONV_SLABS):
        sl = slice(s * LANES, (s + 1) * LANES)
        new = [step_rows(s, t) for t in range(ts)]
        ext = lambda j: sconv_ref[j, :, sl] if j < CONV_CTX else new[j - CONV_CTX]
        for j in range(CONV_CTX - ts):
            conv_new_ref[j, :, sl] = sconv_ref[j + ts, :, sl]
        for t in range(ts):
            conv_new_ref[CONV_CTX - ts + t, :, sl] = new[t]
            acc = jnp.broadcast_to(conv_b_ref[:, sl], (gb, LANES))
            for k in range(CONV_WIDTH):
                acc = acc + ext(t + k) * conv_w_ref[k, :, sl]
            yslab[s, pl.ds(t, gb, stride=ts), :] = acc

    for g, w in enumerate(POOL_WINDOWS):
        sl = slice(g * LANES, (g + 1) * LANES)
        new = [step_rows(CONV_SLABS + g, t) for t in range(ts)]
        ext = lambda j: spool_ref[j, :, sl] if j < POOL_CTX else new[j - POOL_CTX]
        for j in range(POOL_CTX - ts):
            pool_new_ref[j, :, sl] = spool_ref[j + ts, :, sl]
        for t in range(ts):
            pool_new_ref[POOL_CTX - ts + t, :, sl] = new[t]
            ws = new[t]
            for j in range(1, w):
                ws = ws + ext(POOL_CTX + t - j)
            cnt = float(min(PAST_LEN + t + 1, w))
            dslab[g, pl.ds(t, gb, stride=ts), :] = ws / cnt - new[t]

    for s in range(CONV_SLABS):
        ybuf[:, s * LANES:(s + 1) * LANES] = yslab[s]
    for s in range(POOL_SLABS):
        dbuf[:, s * LANES:(s + 1) * LANES] = dslab[s]

    _mixer_acts(ybuf, dbuf, mixbuf, gmat_ref, gn_g_ref, gn_b_ref, pool_w_ref, pool_scale_ref)
    x1 = _layer_norm(_mixer_out(mixbuf, x, w_out_ref), ln1_g_ref[...], ln1_b_ref[...])
    x1_ref[...] = x1
    q_ref[...] = _bdot(x1, _unpack(xq_ref[...])) * ATTN_SCALE


def _sample_mixer_call(xs, sconv_tm, spool_tm, weights, dec_seq):
    n_tok = xs.shape[0]
    nb = n_tok // dec_seq
    gb = SAMPLE_MIX_BATCH
    rows = gb * dec_seq
    row_spec = pl.BlockSpec((rows, D_MODEL), lambda i: (i, 0))
    conv_spec = pl.BlockSpec((CONV_CTX, gb, D_CONV), lambda i: (0, i, 0))
    pool_spec = pl.BlockSpec((POOL_CTX, gb, D_POOL), lambda i: (0, i, 0))
    out_shape = [
        jax.ShapeDtypeStruct((n_tok, D_MODEL), F32),
        jax.ShapeDtypeStruct((n_tok, D_MODEL), F32),
        jax.ShapeDtypeStruct((CONV_CTX, nb, D_CONV), F32),
        jax.ShapeDtypeStruct((POOL_CTX, nb, D_POOL), F32),
    ]
    scratch = [
        pltpu.VMEM((CONV_SLABS + POOL_SLABS, rows, LANES), F32),
        pltpu.VMEM((CONV_SLABS, rows, LANES), F32),
        pltpu.VMEM((POOL_SLABS, rows, LANES), F32),
        pltpu.VMEM((rows, D_CONV), F32),
        pltpu.VMEM((rows, D_POOL), F32),
        pltpu.VMEM((rows, D_MODEL), BF16),
    ]
    return pl.pallas_call(
        functools.partial(_sample_mixer_kernel, dec_seq=dec_seq),
        grid=(nb // gb,),
        in_specs=[row_spec, conv_spec, pool_spec] + [_const_spec(w.shape) for w in weights],
        out_specs=[row_spec, row_spec, conv_spec, pool_spec],
        out_shape=out_shape,
        scratch_shapes=scratch,
        compiler_params=_params(1),
        name="sample_mixer",
    )(xs, sconv_tm, spool_tm, *weights)


def _load_head(ref, b, hd):
    parts = [ref[b, pl.ds(c * X_HEADS + hd, N_MEM, stride=KV_ROW_STRIDE), :] for c in range(HEAD_CHUNKS)]
    return jnp.concatenate(parts, axis=1).astype(BF16)


def _sample_attn_probs(q_ref, k_ref):
    q8 = q_ref[...]
    q = jnp.concatenate([q8] * (SAMPLE_ATTN_ROWS // q8.shape[0]), axis=0).astype(BF16)
    s = jnp.concatenate(
        [lax.dot_general(q[:, hd * X_HEAD_DIM:(hd + 1) * X_HEAD_DIM], _load_head(k_ref, b, hd),
                         (((1,), (1,)), ((), ())), preferred_element_type=F32)
         for b in range(k_ref.shape[0]) for hd in range(X_HEADS)], axis=0)
    e = jnp.exp(s - jnp.max(s, axis=-1, keepdims=True))
    return (e / jnp.sum(e, axis=-1, keepdims=True)).astype(BF16)


def _sample_attn_values(a, v_ref, o_ref, ts):
    for b in range(v_ref.shape[0]):
        for hd in range(X_HEADS):
            i = b * X_HEADS + hd
            o = jnp.dot(a[i * SAMPLE_ATTN_ROWS:(i + 1) * SAMPLE_ATTN_ROWS, :], _load_head(v_ref, b, hd),
                        preferred_element_type=F32)
            o_ref[b * ts:(b + 1) * ts, hd * X_HEAD_DIM:(hd + 1) * X_HEAD_DIM] = o[b * ts:(b + 1) * ts, :]


def _sample_tail_kernel(x1_ref, o_ref, xo_ref, w1_ref, b1_ref, w2_ref, vec_ref, y_ref, x2buf, x2bf, fbuf):
    vec = _views(vec_ref, VEC_FIELDS)
    ln2_g_ref, ln2_b_ref, b2_ref, ln3_g_ref, ln3_b_ref = (
        vec["ln2_g"], vec["ln2_b"], vec["b2"], vec["ln3_g"], vec["ln3_b"])
    c = pl.program_id(0)

    @pl.when(c == 0)
    def _():
        attn = _bdot(o_ref[...], _unpack(xo_ref[...]))
        x2 = _layer_norm(DN_ALPHA * x1_ref[...] + attn, ln2_g_ref[...], ln2_b_ref[...])
        x2buf[...] = x2
        x2bf[...] = x2.astype(BF16)
        fbuf[...] = jnp.broadcast_to(b2_ref[...], fbuf.shape)

    hdn = jnp.maximum(jnp.dot(x2bf[...], _unpack(w1_ref[...]), preferred_element_type=F32) + b1_ref[...], 0.0)
    fbuf[...] += _bdot(hdn * hdn, _unpack(w2_ref[...]))

    @pl.when(c == pl.num_programs(0) - 1)
    def _():
        y_ref[...] = _layer_norm(DN_ALPHA * x2buf[...] + fbuf[...], ln3_g_ref[...], ln3_b_ref[...])


def _sample_tail_call(x1, o, wb, w2_p, vecs):
    n_tok = x1.shape[0]
    fc = TAIL_FF_CHUNK
    assert fc == D_MODEL and VEC_FIELDS[0] == ("b1", D_FF) and WB_FIELDS[0] == ("xo", D_MODEL)
    in_specs = [
        _const_spec(x1.shape), _const_spec(o.shape),
        pl.BlockSpec((D_MODEL // 2, D_MODEL), lambda c: (0, 0), pipeline_mode=pl.Buffered(1)),
        pl.BlockSpec((D_MODEL // 2, fc), lambda c: (0, c + 1)),
        pl.BlockSpec((1, fc), lambda c: (0, c)),
        pl.BlockSpec((fc // 2, D_MODEL), lambda c: (c, 0)),
        _const_spec(vecs.shape),
    ]
    return pl.pallas_call(
        _sample_tail_kernel,
        grid=(D_FF // fc,),
        in_specs=in_specs,
        out_specs=pl.BlockSpec((n_tok, D_MODEL), lambda c: (0, 0)),
        out_shape=jax.ShapeDtypeStruct((n_tok, D_MODEL), F32),
        scratch_shapes=[pltpu.VMEM((n_tok, D_MODEL), F32), pltpu.VMEM((n_tok, D_MODEL), BF16),
                        pltpu.VMEM((n_tok, D_MODEL), F32)],
        compiler_params=_params(1),
        name="sample_tail",
    )(x1, o, wb, wb, vecs, w2_p, vecs)


def _group_indicator():
    gid = jnp.arange(GN_HALF) // CONV_HEAD_DIM
    return (gid[:, None] == gid[None, :]).astype(F32)


def _to_head_split_view(a):
    lead = a.shape[:-3]
    n = len(lead)
    a = a.reshape(lead + (N_MEM, X_HEADS, HEAD_CHUNKS, LANES))
    a = a.transpose(tuple(range(n)) + (n, n + 2, n + 1, n + 3))
    return a.reshape(lead + (KV_ROWS, LANES))


def _from_head_split_view(a):
    lead = a.shape[:-2]
    n = len(lead)
    a = a.reshape(lead + (N_MEM, HEAD_CHUNKS, X_HEADS, LANES))
    a = a.transpose(tuple(range(n)) + (n, n + 2, n + 1, n + 3))
    return a.reshape(lead + (N_MEM, X_HEADS, X_HEAD_DIM))


def kernel(x_prompt, x_sample, mem_prompt, cache_mem_k, cache_mem_v, state_conv, state_pool, w_in, b_in, conv_w, conv_b, gn_g, gn_b, pool_w, pool_scale, w_out, ln1_g, ln1_b, xq_w, xk_w, xv_w, xo_w, ln2_g, ln2_b, w1, b1, w2, b2, ln3_g, ln3_b):
    assert w_in.shape[0] == DEPTH == 1
    n_prompt, seq, _ = x_prompt.shape
    n_dec, dec_seq, _ = x_sample.shape
    assert seq % PROMPT_BLOCK == 0 and n_dec % SAMPLE_MIX_BATCH == 0

    named = dict(w_in=w_in, w_out=w_out, xq=xq_w, xo=xo_w, w1=w1, b1=b1, b_in=b_in, conv_b=conv_b, gn_g=gn_g,
                 gn_b=gn_b, pool_scale=pool_scale, ln1_g=ln1_g, ln1_b=ln1_b, ln2_g=ln2_g, ln2_b=ln2_b, b2=b2,
                 ln3_g=ln3_g, ln3_b=ln3_b)
    wa, wb, wkv, pool_w_p, gmat_p, w2_p, vecs = _cast_call(
        [[named[name][0] for name, _ in WA_FIELDS], [named[name][0] for name, _ in WB_FIELDS],
         [xk_w[0], xv_w[0]], [pool_w[0].reshape(len(POOL_WINDOWS) * POOL_GROUP, POOL_GROUP)],
         [_group_indicator()], [w2[0]]],
        [named[name] for name, _ in VEC_FIELDS])
    conv_taps = conv_w.transpose(1, 0, 2)

    xs = x_sample.reshape(n_dec * dec_seq, D_MODEL)
    x1_s, q_s, conv_s, pool_s = _sample_mixer_call(
        xs, state_conv[0].transpose(1, 0, 2), state_pool[0].transpose(1, 0, 2),
        (wa, pool_w_p, gmat_p, conv_taps, vecs), dec_seq)

    k_p, v_p, kt_p, vb_p = _kv_proj_call(mem_prompt, wkv)
    y_p, conv_p, pool_p, o_s = _prompt_call(
        x_prompt, kt_p, vb_p, q_s, _to_head_split_view(cache_mem_k[0]), _to_head_split_view(cache_mem_v[0]),
        dec_seq, (wa, wb, w2_p, pool_w_p, gmat_p, conv_taps, vecs))

    y_s = _sample_tail_call(x1_s, o_s, wb, w2_p, vecs)

    return (y_p, y_s.reshape(n_dec, dec_seq, D_MODEL),
            _from_head_split_view(k_p)[None], _from_head_split_view(v_p)[None],
            conv_p[None], conv_s.transpose(1, 0, 2)[None], pool_p[None], pool_s.transpose(1, 0, 2)[None])
```

```python
import functools

import jax
import jax.numpy as jnp
from jax import lax
from jax.experimental import pallas as pl
from jax.experimental.pallas import tpu as pltpu

F32 = jnp.float32
BF16 = jnp.bfloat16
U32 = jnp.uint32

LANES = 128
SUBLANES = 8
D_MODEL = 1024
D_CONV = 512
D_POOL = 512
D_IN = 2 * D_CONV + D_POOL
CONV_WIDTH = 31
CONV_CTX = CONV_WIDTH - 1
CONV_HEADS = 8
CONV_HEAD_DIM = D_CONV // CONV_HEADS
POOL_WINDOWS = (2, 4, 8, 16)
POOL_GROUP = D_POOL // len(POOL_WINDOWS)
POOL_CTX = max(POOL_WINDOWS) - 1
N_MEM = 256
X_HEADS = 4
X_HEAD_DIM = D_MODEL // X_HEADS
HEAD_CHUNKS = X_HEAD_DIM // LANES
KV_ROWS = N_MEM * X_HEADS * HEAD_CHUNKS
KV_ROW_STRIDE = X_HEADS * HEAD_CHUNKS
D_FF = 4 * D_MODEL
LN_EPS = 1e-5
DEPTH = 1
PAST_LEN = 16384
DN_ALPHA = (2.0 * DEPTH) ** 0.25
ATTN_SCALE = X_HEAD_DIM ** -0.5

assert POOL_GROUP == LANES and X_HEAD_DIM % LANES == 0

CONV_SLABS = D_CONV // LANES
POOL_SLABS = D_POOL // LANES
CONV_PAD = 32
POOL_PAD = 16
GN_HALF = 256

PROMPT_BLOCK = 256
CONV_ROWS = 32
SAMPLE_MIX_BATCH = 32
SAMPLE_ATTN_ROWS = 16
TAIL_FF_CHUNK = 1024
CAST_STEPS = 8
VMEM_LIMIT = 56 * 1024 * 1024


def _unpack(w_u32):
    return pltpu.bitcast(w_u32, BF16)


def _pack(w_bf16):
    return pltpu.bitcast(w_bf16, U32)


def _bdot(a, w):
    return jnp.dot(a.astype(BF16), w, preferred_element_type=F32)


def _layer_norm(x, g, b):
    mu = jnp.mean(x, axis=-1, keepdims=True)
    d = x - mu
    var = jnp.mean(d * d, axis=-1, keepdims=True)
    return d * lax.rsqrt(var + LN_EPS) * g + b


def _group_norm_swish(y, gmat, gn_g, gn_b):
    inv = 1.0 / CONV_HEAD_DIM
    yc = y - _bdot(y, gmat) * inv
    rest = _bdot(yc, gmat) * inv
    var = _bdot(yc * yc, gmat) * inv - rest * rest
    d = yc - rest
    n = d * lax.rsqrt(var + LN_EPS) * gn_g + gn_b
    return n * jax.nn.sigmoid(n)


def _mixer_acts(ybuf, dbuf, mixbuf, gmat_ref, gn_g_ref, gn_b_ref, pool_w_ref, pool_scale_ref):
    gmat = _unpack(gmat_ref[...])
    for j in range(D_CONV // GN_HALF):
        sl = slice(j * GN_HALF, (j + 1) * GN_HALF)
        o = _group_norm_swish(ybuf[:, sl], gmat, gn_g_ref[:, sl], gn_b_ref[:, sl])
        mixbuf[:, sl] = o.astype(BF16)
    half = POOL_GROUP // 2
    for g in range(len(POOL_WINDOWS)):
        sl = slice(g * POOL_GROUP, (g + 1) * POOL_GROUP)
        pw = _unpack(pool_w_ref[g * half:(g + 1) * half, :])
        mixed = _bdot(dbuf[:, sl], pw) * pool_scale_ref[:, sl]
        mixbuf[:, D_CONV + g * POOL_GROUP:D_CONV + (g + 1) * POOL_GROUP] = mixed.astype(BF16)


def _mixer_out(mixbuf, x, w_out_ref):
    return DN_ALPHA * x + jnp.dot(mixbuf[...], _unpack(w_out_ref[...]), preferred_element_type=F32)


def _const_spec(shape):
    nd = len(shape)
    return pl.BlockSpec(shape, lambda *_: (0,) * nd, pipeline_mode=pl.Buffered(1))


def _params(n_axes):
    return pltpu.CompilerParams(dimension_semantics=("arbitrary",) * n_axes, vmem_limit_bytes=VMEM_LIMIT)


VEC_FIELDS = (("b1", D_FF), ("b_in", D_IN), ("conv_b", D_CONV), ("gn_g", D_CONV), ("gn_b", D_CONV),
              ("pool_scale", D_POOL), ("ln1_g", D_MODEL), ("ln1_b", D_MODEL), ("ln2_g", D_MODEL),
              ("ln2_b", D_MODEL), ("b2", D_MODEL), ("ln3_g", D_MODEL), ("ln3_b", D_MODEL))
VEC_LEN = sum(n for _, n in VEC_FIELDS)
WA_FIELDS = (("w_in", D_IN), ("w_out", D_MODEL), ("xq", D_MODEL))
WB_FIELDS = (("xo", D_MODEL), ("w1", D_FF))


def _views(ref, fields):
    out, off = {}, 0
    for name, n in fields:
        out[name] = ref.at[:, off:off + n]
        off += n
    return out


def _cast_kernel(*refs, widths, n_vecs):
    n_mats = sum(len(g) for g in widths)
    srcs, vec_srcs = refs[:n_mats], refs[n_mats:n_mats + n_vecs]
    dsts, vec_dst = refs[n_mats + n_vecs:-1], refs[-1]
    i = 0
    for dst, group in zip(dsts, widths):
        off = 0
        for n in group:
            dst[:, off:off + n] = _pack(srcs[i][...].astype(BF16))
            off += n
            i += 1
    @pl.when(pl.program_id(0) == 0)
    def _():
        off = 0
        for src in vec_srcs:
            n = src.shape[1]
            vec_dst[:, off:off + n] = src[...]
            off += n


def _cast_call(groups, vectors):
    in_specs, out_specs, out_shape, widths = [], [], [], []
    for group in groups:
        k = group[0].shape[0]
        rows = k // CAST_STEPS
        assert rows * CAST_STEPS == k and rows % 16 == 0 and all(w.shape[0] == k for w in group)
        assert all(w.shape[1] % LANES == 0 for w in group)
        widths.append(tuple(w.shape[1] for w in group))
        total = sum(widths[-1])
        in_specs += [pl.BlockSpec((rows, w.shape[1]), lambda i: (i, 0)) for w in group]
        out_specs.append(pl.BlockSpec((rows // 2, total), lambda i: (i, 0)))
        out_shape.append(jax.ShapeDtypeStruct((k // 2, total), U32))
    assert all(v.shape[1] % LANES == 0 for v in vectors)
    vec_len = sum(v.shape[1] for v in vectors)
    in_specs += [pl.BlockSpec(v.shape, lambda i: (0, 0)) for v in vectors]
    out_specs.append(pl.BlockSpec((1, vec_len), lambda i: (0, 0)))
    out_shape.append(jax.ShapeDtypeStruct((1, vec_len), F32))
    return pl.pallas_call(
        functools.partial(_cast_kernel, widths=tuple(widths), n_vecs=len(vectors)), grid=(CAST_STEPS,),
        in_specs=in_specs, out_specs=out_specs, out_shape=out_shape, compiler_params=_params(1),
        name="cast_weights",
    )(*[w for group in groups for w in group], *vectors)


def _store_head_split(dst_ref, val):
    for hd in range(X_HEADS):
        for c in range(HEAD_CHUNKS):
            col = hd * X_HEAD_DIM + c * LANES
            dst_ref[0, pl.ds(c * X_HEADS + hd, N_MEM, stride=KV_ROW_STRIDE), :] = val[:, col:col + LANES]


def _kv_proj_kernel(mem_ref, wkv_ref, k_ref, v_ref, kt_ref, vb_ref):
    m = mem_ref[0].astype(BF16)
    k = jnp.dot(m, _unpack(wkv_ref[:, 0:D_MODEL]), preferred_element_type=F32)
    v = jnp.dot(m, _unpack(wkv_ref[:, D_MODEL:2 * D_MODEL]), preferred_element_type=F32)
    _store_head_split(k_ref, k)
    _store_head_split(v_ref, v)
    kt_ref[0] = _pack(k.T.astype(BF16))
    vb_ref[0] = _pack(v.astype(BF16))


def _kv_proj_call(mem, wkv):
    nb = mem.shape[0]
    blk = lambda s: pl.BlockSpec((1,) + s, lambda b: (b, 0, 0))
    return pl.pallas_call(
        _kv_proj_kernel,
        grid=(nb,),
        in_specs=[blk((N_MEM, D_MODEL)), _const_spec(wkv.shape)],
        out_specs=[blk((KV_ROWS, LANES)), blk((KV_ROWS, LANES)),
                   blk((D_MODEL // 2, N_MEM)), blk((N_MEM // 2, D_MODEL))],
        out_shape=[
            jax.ShapeDtypeStruct((nb, KV_ROWS, LANES), F32),
            jax.ShapeDtypeStruct((nb, KV_ROWS, LANES), F32),
            jax.ShapeDtypeStruct((nb, D_MODEL // 2, N_MEM), U32),
            jax.ShapeDtypeStruct((nb, N_MEM // 2, D_MODEL), U32),
        ],
        compiler_params=_params(1),
        name="kv_proj",
    )(mem, wkv)


def _prompt_kernel(x_ref, kt_ref, v_ref, qs_ref, kc_ref, vc_ref,
                   wa_ref, wb_ref, w2_ref, pool_w_ref, gmat_ref, conv_w_ref, vec_ref,
                   y_ref, conv_new_ref, pool_new_ref, os_ref,
                   ubuf, pbuf, ybuf, dbuf, mixbuf, obuf, x1buf, x1prev, x2buf, hbuf, *, nt):
    wa, wb, vec = _views(wa_ref, WA_FIELDS), _views(wb_ref, WB_FIELDS), _views(vec_ref, VEC_FIELDS)
    w_in_ref, w_out_ref, xq_ref, xo_ref, w1_ref = wa["w_in"], wa["w_out"], wa["xq"], wb["xo"], wb["w1"]
    b_in_ref, conv_b_ref, gn_g_ref, gn_b_ref = vec["b_in"], vec["conv_b"], vec["gn_g"], vec["gn_b"]
    pool_scale_ref, ln1_g_ref, ln1_b_ref = vec["pool_scale"], vec["ln1_g"], vec["ln1_b"]
    ln2_g_ref, ln2_b_ref, b1_ref, b2_ref = vec["ln2_g"], vec["ln2_b"], vec["b1"], vec["b2"]
    ln3_g_ref, ln3_b_ref = vec["ln3_g"], vec["ln3_b"]
    tm = PROMPT_BLOCK
    step = pl.program_id(0)
    n_blocks = pl.num_programs(0) - 1
    t = jnp.minimum(step, n_blocks - 1) % nt

    @pl.when(t == 0)
    def _():
        ubuf[:, 0:CONV_PAD, :] = jnp.zeros((CONV_SLABS, CONV_PAD, LANES), F32)
        pbuf[:, 0:POOL_PAD, :] = jnp.zeros((POOL_SLABS, POOL_PAD, LANES), F32)

    @pl.when(t > 0)
    def _():
        ubuf[:, 0:CONV_PAD, :] = ubuf[:, tm:tm + CONV_PAD, :]
        pbuf[:, 0:POOL_PAD, :] = pbuf[:, tm:tm + POOL_PAD, :]

    head = lambda hd: slice(hd * X_HEAD_DIM, (hd + 1) * X_HEAD_DIM)

    val = {}

    def mix_in():
        val["x"] = x_ref[0]
        h = _bdot(val["x"], _unpack(w_in_ref[...])) + b_in_ref[...]
        u = h[:, 0:D_CONV] * jax.nn.sigmoid(h[:, D_CONV:2 * D_CONV])
        for s in range(CONV_SLABS):
            ubuf[s, CONV_PAD:CONV_PAD + tm, :] = u[:, s * LANES:(s + 1) * LANES]
        for s in range(POOL_SLABS):
            pbuf[s, POOL_PAD:POOL_PAD + tm, :] = h[:, 2 * D_CONV + s * LANES:2 * D_CONV + (s + 1) * LANES]

    def att_scores():
        x1prev[...] = _layer_norm(x1buf[...], ln1_g_ref[...], ln1_b_ref[...])
        q = (_bdot(x1prev[...], _unpack(xq_ref[...])) * ATTN_SCALE).astype(BF16)
        val["scores"] = [
            jnp.dot(q[:, head(hd)], _unpack(kt_ref[0, hd * X_HEAD_DIM // 2:(hd + 1) * X_HEAD_DIM // 2, :]),
                    preferred_element_type=F32) for hd in range(X_HEADS)]

    def dec_probs():
        val["probs_dec"] = _sample_attn_probs(qs_ref, kc_ref)

    def att_values():
        probs = []
        for s in val["scores"]:
            e = jnp.exp(s - jnp.max(s, axis=-1, keepdims=True))
            probs.append((e / jnp.sum(e, axis=-1, keepdims=True)).astype(BF16))
        for hd in range(X_HEADS):
            obuf[:, head(hd)] = jnp.dot(probs[hd], _unpack(v_ref[0, :, head(hd)]),
                                        preferred_element_type=F32).astype(BF16)

    def mix_windows():
        base = CONV_PAD - CONV_CTX
        for s in range(CONV_SLABS):
            sl = slice(s * LANES, (s + 1) * LANES)
            for c in range(tm // CONV_ROWS):
                r0 = c * CONV_ROWS
                acc = jnp.broadcast_to(conv_b_ref[:, sl], (CONV_ROWS, LANES))
                for k in range(CONV_WIDTH):
                    acc = acc + ubuf[s, base + r0 + k:base + r0 + k + CONV_ROWS, :] * conv_w_ref[k, :, sl]
                ybuf[r0:r0 + CONV_ROWS, sl] = acc

        pos = t * tm + lax.broadcasted_iota(jnp.int32, (tm, POOL_GROUP), 0)
        for gi, w in enumerate(POOL_WINDOWS):
            cur = pbuf[gi, POOL_PAD:POOL_PAD + tm, :]
            ws = cur
            for j in range(1, w):
                ws = ws + pbuf[gi, POOL_PAD - j:POOL_PAD - j + tm, :]
            cnt = jnp.minimum(pos + 1, w).astype(F32)
            dbuf[:, gi * POOL_GROUP:(gi + 1) * POOL_GROUP] = ws / cnt - cur

    def att_out():
        val["attn"] = jnp.dot(obuf[...], _unpack(xo_ref[...]), preferred_element_type=F32)

    def dec_values():
        _sample_attn_values(val["probs_dec"], vc_ref, os_ref, os_ref.shape[0] // vc_ref.shape[0])

    def ffn_norm():
        x2buf[...] = _layer_norm(DN_ALPHA * x1prev[...] + val["attn"], ln2_g_ref[...], ln2_b_ref[...])

    def ffn_up():
        hdn = jnp.maximum(_bdot(x2buf[...], _unpack(w1_ref[...])) + b1_ref[...], 0.0)
        hbuf[...] = (hdn * hdn).astype(BF16)

    def mix_acts():
        _mixer_acts(ybuf, dbuf, mixbuf, gmat_ref, gn_g_ref, gn_b_ref, pool_w_ref, pool_scale_ref)

    def ffn_down():
        val["f"] = jnp.dot(hbuf[...], _unpack(w2_ref[...]), preferred_element_type=F32) + b2_ref[...]

    def mix_out():
        x1buf[...] = _mixer_out(mixbuf, val["x"], w_out_ref)

    def out_norm():
        y_ref[0] = _layer_norm(DN_ALPHA * x2buf[...] + val["f"], ln3_g_ref[...], ln3_b_ref[...])

    @pl.when(step == 0)
    def _():
        x1buf[...] = jnp.zeros((tm, D_MODEL), F32)

    for piece in (mix_in, att_scores, dec_probs, att_values, mix_windows, att_out, dec_values, ffn_norm, ffn_up,
                  mix_acts, ffn_down, mix_out, out_norm):
        piece()

    @pl.when(jnp.logical_and(t == nt - 1, step < n_blocks))
    def _():
        for s in range(CONV_SLABS):
            conv_new_ref[0, :, s * LANES:(s + 1) * LANES] = ubuf[s, CONV_PAD + tm - CONV_CTX:CONV_PAD + tm, :]
        for s in range(POOL_SLABS):
            pool_new_ref[0, :, s * LANES:(s + 1) * LANES] = pbuf[s, POOL_PAD + tm - POOL_CTX:POOL_PAD + tm, :]


def _prompt_call(x, kt, vb, q_dec, k_dec, v_dec, dec_seq, weights):
    nb, seq, _ = x.shape
    tm = PROMPT_BLOCK
    nt = seq // tm
    n_blocks = nb * nt
    n_dec = k_dec.shape[0]
    dec_per_step = -(-n_dec // n_blocks)
    dec_steps = n_dec // dec_per_step
    dec_rows = dec_per_step * dec_seq
    assert dec_steps * dec_per_step == n_dec and dec_steps <= n_blocks + 1
    assert dec_rows % SUBLANES == 0 and SAMPLE_ATTN_ROWS % dec_rows == 0
    cur = lambda g: jnp.minimum(g, n_blocks - 1)
    prev = lambda g: jnp.maximum(g - 1, 0)
    dec = lambda g: jnp.minimum(g, dec_steps - 1)
    in_specs = [
        pl.BlockSpec((1, tm, D_MODEL), lambda g: (cur(g) // nt, cur(g) % nt, 0)),
        pl.BlockSpec((1, D_MODEL // 2, N_MEM), lambda g: (prev(g) // nt, 0, 0)),
        pl.BlockSpec((1, N_MEM // 2, D_MODEL), lambda g: (prev(g) // nt, 0, 0)),
        pl.BlockSpec((dec_rows, D_MODEL), lambda g: (dec(g), 0)),
        pl.BlockSpec((dec_per_step, KV_ROWS, LANES), lambda g: (dec(g), 0, 0)),
        pl.BlockSpec((dec_per_step, KV_ROWS, LANES), lambda g: (dec(g), 0, 0)),
    ] + [_const_spec(w.shape) for w in weights]
    out_specs = [
        pl.BlockSpec((1, tm, D_MODEL), lambda g: (prev(g) // nt, prev(g) % nt, 0)),
        pl.BlockSpec((1, CONV_CTX, D_CONV), lambda g: (cur(g) // nt, 0, 0)),
        pl.BlockSpec((1, POOL_CTX, D_POOL), lambda g: (cur(g) // nt, 0, 0)),
        pl.BlockSpec((dec_rows, D_MODEL), lambda g: (dec(g), 0)),
    ]
    out_shape = [
        jax.ShapeDtypeStruct((nb, seq, D_MODEL), F32),
        jax.ShapeDtypeStruct((nb, CONV_CTX, D_CONV), F32),
        jax.ShapeDtypeStruct((nb, POOL_CTX, D_POOL), F32),
        jax.ShapeDtypeStruct(q_dec.shape, F32),
    ]
    scratch = [
        pltpu.VMEM((CONV_SLABS, CONV_PAD + tm, LANES), F32),
        pltpu.VMEM((POOL_SLABS, POOL_PAD + tm, LANES), F32),
        pltpu.VMEM((tm, D_CONV), F32),
        pltpu.VMEM((tm, D_POOL), F32),
        pltpu.VMEM((tm, D_MODEL), BF16),
        pltpu.VMEM((tm, D_MODEL), BF16),
        pltpu.VMEM((tm, D_MODEL), F32),
        pltpu.VMEM((tm, D_MODEL), F32),
        pltpu.VMEM((tm, D_MODEL), F32),
        pltpu.VMEM((tm, D_FF), BF16),
    ]
    return pl.pallas_call(
        functools.partial(_prompt_kernel, nt=nt),
        grid=(n_blocks + 1,),
        in_specs=in_specs,
        out_specs=out_specs,
        out_shape=out_shape,
        scratch_shapes=scratch,
        compiler_params=_params(1),
        name="prompt_layer",
    )(x, kt, vb, q_dec, k_dec, v_dec, *weights)


def _sample_mixer_kernel(x_ref, sconv_ref, spool_ref,
                         wa_ref, pool_w_ref, gmat_ref, conv_w_ref, vec_ref,
                         x1_ref, q_ref, conv_new_ref, pool_new_ref,
                         hbuf, yslab, dslab, ybuf, dbuf, mixbuf, *, dec_seq):
    wa, vec = _views(wa_ref, WA_FIELDS), _views(vec_ref, VEC_FIELDS)
    w_in_ref, w_out_ref, xq_ref = wa["w_in"], wa["w_out"], wa["xq"]
    b_in_ref, conv_b_ref, gn_g_ref, gn_b_ref = vec["b_in"], vec["conv_b"], vec["gn_g"], vec["gn_b"]
    pool_scale_ref, ln1_g_ref, ln1_b_ref = vec["pool_scale"], vec["ln1_g"], vec["ln1_b"]
    gb = SAMPLE_MIX_BATCH
    ts = dec_seq
    x = x_ref[...]
    h = _bdot(x, _unpack(w_in_ref[...])) + b_in_ref[...]
    u = h[:, 0:D_CONV] * jax.nn.sigmoid(h[:, D_CONV:2 * D_CONV])
    for s in range(CONV_SLABS):
        hbuf[s] = u[:, s * LANES:(s + 1) * LANES]
    for s in range(POOL_SLABS):
        hbuf[CONV_SLABS + s] = h[:, 2 * D_CONV + s * LANES:2 * D_CONV + (s + 1) * LANES]

    def step_rows(slab, t):
        return hbuf[slab, pl.ds(t, gb, stride=ts), :]

    for s in range(CONV_SLABS):
        sl = slice(s * LANES, (s + 1) * LANES)
        new = [step_rows(s, t) for t in range(ts)]
        ext = lambda j: sconv_ref[j, :, sl] if j < CONV_CTX else new[j - CONV_CTX]
        for j in range(CONV_CTX - ts):
            conv_new_ref[j, :, sl] = sconv_ref[j + ts, :, sl]
        for t in range(ts):
            conv_new_ref[CONV_CTX - ts + t, :, sl] = new[t]
            acc = jnp.broadcast_to(conv_b_ref[:, sl], (gb, LANES))
            for k in range(CONV_WIDTH):
                acc = acc + ext(t + k) * conv_w_ref[k, :, sl]
            yslab[s, pl.ds(t, gb, stride=ts), :] = acc

    for g, w in enumerate(POOL_WINDOWS):
        sl = slice(g * LANES, (g + 1) * LANES)
        new = [step_rows(CONV_SLABS + g, t) for t in range(ts)]
        ext = lambda j: spool_ref[j, :, sl] if j < POOL_CTX else new[j - POOL_CTX]
        for j in range(POOL_CTX - ts):
            pool_new_ref[j, :, sl] = spool_ref[j + ts, :, sl]
        for t in range(ts):
            pool_new_ref[POOL_CTX - ts + t, :, sl] = new[t]
            ws = new[t]
            for j in range(1, w):
                ws = ws + ext(POOL_CTX + t - j)
            cnt = float(min(PAST_LEN + t + 1, w))
            dslab[g, pl.ds(t, gb, stride=ts), :] = ws / cnt - new[t]

    for s in range(CONV_SLABS):
        ybuf[:, s * LANES:(s + 1) * LANES] = yslab[s]
    for s in range(POOL_SLABS):
        dbuf[:, s * LANES:(s + 1) * LANES] = dslab[s]

    _mixer_acts(ybuf, dbuf, mixbuf, gmat_ref, gn_g_ref, gn_b_ref, pool_w_ref, pool_scale_ref)
    x1 = _layer_norm(_mixer_out(mixbuf, x, w_out_ref), ln1_g_ref[...], ln1_b_ref[...])
    x1_ref[...] = x1
    q_ref[...] = _bdot(x1, _unpack(xq_ref[...])) * ATTN_SCALE


def _sample_mixer_call(xs, sconv_tm, spool_tm, weights, dec_seq):
    n_tok = xs.shape[0]
    nb = n_tok // dec_seq
    gb = SAMPLE_MIX_BATCH
    rows = gb * dec_seq
    row_spec = pl.BlockSpec((rows, D_MODEL), lambda i: (i, 0))
    conv_spec = pl.BlockSpec((CONV_CTX, gb, D_CONV), lambda i: (0, i, 0))
    pool_spec = pl.BlockSpec((POOL_CTX, gb, D_POOL), lambda i: (0, i, 0))
    out_shape = [
        jax.ShapeDtypeStruct((n_tok, D_MODEL), F32),
        jax.ShapeDtypeStruct((n_tok, D_MODEL), F32),
        jax.ShapeDtypeStruct((CONV_CTX, nb, D_CONV), F32),
        jax.ShapeDtypeStruct((POOL_CTX, nb, D_POOL), F32),
    ]
    scratch = [
        pltpu.VMEM((CONV_SLABS + POOL_SLABS, rows, LANES), F32),
        pltpu.VMEM((CONV_SLABS, rows, LANES), F32),
        pltpu.VMEM((POOL_SLABS, rows, LANES), F32),
        pltpu.VMEM((rows, D_CONV), F32),
        pltpu.VMEM((rows, D_POOL), F32),
        pltpu.VMEM((rows, D_MODEL), BF16),
    ]
    return pl.pallas_call(
        functools.partial(_sample_mixer_kernel, dec_seq=dec_seq),
        grid=(nb // gb,),
        in_specs=[row_spec, conv_spec, pool_spec] + [_const_spec(w.shape) for w in weights],
        out_specs=[row_spec, row_spec, conv_spec, pool_spec],
        out_shape=out_shape,
        scratch_shapes=scratch,
        compiler_params=_params(1),
        name="sample_mixer",
    )(xs, sconv_tm, spool_tm, *weights)


def _load_head(ref, b, hd):
    parts = [ref[b, pl.ds(c * X_HEADS + hd, N_MEM, stride=KV_ROW_STRIDE), :] for c in range(HEAD_CHUNKS)]
    return jnp.concatenate(parts, axis=1).astype(BF16)


def _sample_attn_probs(q_ref, k_ref):
    q8 = q_ref[...]
    q = jnp.concatenate([q8] * (SAMPLE_ATTN_ROWS // q8.shape[0]), axis=0).astype(BF16)
    s = jnp.concatenate(
        [lax.dot_general(q[:, hd * X_HEAD_DIM:(hd + 1) * X_HEAD_DIM], _load_head(k_ref, b, hd),
                         (((1,), (1,)), ((), ())), preferred_element_type=F32)
         for b in range(k_ref.shape[0]) for hd in range(X_HEADS)], axis=0)
    e = jnp.exp(s - jnp.max(s, axis=-1, keepdims=True))
    return (e / jnp.sum(e, axis=-1, keepdims=True)).astype(BF16)


def _sample_attn_values(a, v_ref, o_ref, ts):
    for b in range(v_ref.shape[0]):
        for hd in range(X_HEADS):
            i = b * X_HEADS + hd
            o = jnp.dot(a[i * SAMPLE_ATTN_ROWS:(i + 1) * SAMPLE_ATTN_ROWS, :], _load_head(v_ref, b, hd),
                        preferred_element_type=F32)
            o_ref[b * ts:(b + 1) * ts, hd * X_HEAD_DIM:(hd + 1) * X_HEAD_DIM] = o[b * ts:(b + 1) * ts, :]


def _sample_tail_kernel(x1_ref, o_ref, xo_ref, w1_ref, b1_ref, w2_ref, vec_ref, y_ref, x2buf, x2bf, fbuf):
    vec = _views(vec_ref, VEC_FIELDS)
    ln2_g_ref, ln2_b_ref, b2_ref, ln3_g_ref, ln3_b_ref = (
        vec["ln2_g"], vec["ln2_b"], vec["b2"], vec["ln3_g"], vec["ln3_b"])
    c = pl.program_id(0)

    @pl.when(c == 0)
    def _():
        attn = _bdot(o_ref[...], _unpack(xo_ref[...]))
        x2 = _layer_norm(DN_ALPHA * x1_ref[...] + attn, ln2_g_ref[...], ln2_b_ref[...])
        x2buf[...] = x2
        x2bf[...] = x2.astype(BF16)
        fbuf[...] = jnp.broadcast_to(b2_ref[...], fbuf.shape)

    hdn = jnp.maximum(jnp.dot(x2bf[...], _unpack(w1_ref[...]), preferred_element_type=F32) + b1_ref[...], 0.0)
    fbuf[...] += _bdot(hdn * hdn, _unpack(w2_ref[...]))

    @pl.when(c == pl.num_programs(0) - 1)
    def _():
        y_ref[...] = _layer_norm(DN_ALPHA * x2buf[...] + fbuf[...], ln3_g_ref[...], ln3_b_ref[...])


def _sample_tail_call(x1, o, wb, w2_p, vecs):
    n_tok = x1.shape[0]
    fc = TAIL_FF_CHUNK
    assert fc == D_MODEL and VEC_FIELDS[0] == ("b1", D_FF) and WB_FIELDS[0] == ("xo", D_MODEL)
    in_specs = [
        _const_spec(x1.shape), _const_spec(o.shape),
        pl.BlockSpec((D_MODEL // 2, D_MODEL), lambda c: (0, 0), pipeline_mode=pl.Buffered(1)),
        pl.BlockSpec((D_MODEL // 2, fc), lambda c: (0, c + 1)),
        pl.BlockSpec((1, fc), lambda c: (0, c)),
        pl.BlockSpec((fc // 2, D_MODEL), lambda c: (c, 0)),
        _const_spec(vecs.shape),
    ]
    return pl.pallas_call(
        _sample_tail_kernel,
        grid=(D_FF // fc,),
        in_specs=in_specs,
        out_specs=pl.BlockSpec((n_tok, D_MODEL), lambda c: (0, 0)),
        out_shape=jax.ShapeDtypeStruct((n_tok, D_MODEL), F32),
        scratch_shapes=[pltpu.VMEM((n_tok, D_MODEL), F32), pltpu.VMEM((n_tok, D_MODEL), BF16),
                        pltpu.VMEM((n_tok, D_MODEL), F32)],
        compiler_params=_params(1),
        name="sample_tail",
    )(x1, o, wb, wb, vecs, w2_p, vecs)


def _group_indicator():
    gid = jnp.arange(GN_HALF) // CONV_HEAD_DIM
    return (gid[:, None] == gid[None, :]).astype(F32)


def _to_head_split_view(a):
    lead = a.shape[:-3]
    n = len(lead)
    a = a.reshape(lead + (N_MEM, X_HEADS, HEAD_CHUNKS, LANES))
    a = a.transpose(tuple(range(n)) + (n, n + 2, n + 1, n + 3))
    return a.reshape(lead + (KV_ROWS, LANES))


def _from_head_split_view(a):
    lead = a.shape[:-2]
    n = len(lead)
    a = a.reshape(lead + (N_MEM, HEAD_CHUNKS, X_HEADS, LANES))
    a = a.transpose(tuple(range(n)) + (n, n + 2, n + 1, n + 3))
    return a.reshape(lead + (N_MEM, X_HEADS, X_HEAD_DIM))


def kernel(x_prompt, x_sample, mem_prompt, cache_mem_k, cache_mem_v, state_conv, state_pool, w_in, b_in, conv_w, conv_b, gn_g, gn_b, pool_w, pool_scale, w_out, ln1_g, ln1_b, xq_w, xk_w, xv_w, xo_w, ln2_g, ln2_b, w1, b1, w2, b2, ln3_g, ln3_b):
    assert w_in.shape[0] == DEPTH == 1
    n_prompt, seq, _ = x_prompt.shape
    n_dec, dec_seq, _ = x_sample.shape
    assert seq % PROMPT_BLOCK == 0 and n_dec % SAMPLE_MIX_BATCH == 0

    named = dict(w_in=w_in, w_out=w_out, xq=xq_w, xo=xo_w, w1=w1, b1=b1, b_in=b_in, conv_b=conv_b, gn_g=gn_g,
                 gn_b=gn_b, pool_scale=pool_scale, ln1_g=ln1_g, ln1_b=ln1_b, ln2_g=ln2_g, ln2_b=ln2_b, b2=b2,
                 ln3_g=ln3_g, ln3_b=ln3_b)
    wa, wb, wkv, pool_w_p, gmat_p, w2_p, vecs = _cast_call(
        [[named[name][0] for name, _ in WA_FIELDS], [named[name][0] for name, _ in WB_FIELDS],
         [xk_w[0], xv_w[0]], [pool_w[0].reshape(len(POOL_WINDOWS) * POOL_GROUP, POOL_GROUP)],
         [_group_indicator()], [w2[0]]],
        [named[name] for name, _ in VEC_FIELDS])
    conv_taps = conv_w.transpose(1, 0, 2)

    xs = x_sample.reshape(n_dec * dec_seq, D_MODEL)
    x1_s, q_s, conv_s, pool_s = _sample_mixer_call(
        xs, state_conv[0].transpose(1, 0, 2), state_pool[0].transpose(1, 0, 2),
        (wa, pool_w_p, gmat_p, conv_taps, vecs), dec_seq)

    k_p, v_p, kt_p, vb_p = _kv_proj_call(mem_prompt, wkv)
    y_p, conv_p, pool_p, o_s = _prompt_call(
        x_prompt, kt_p, vb_p, q_s, _to_head_split_view(cache_mem_k[0]), _to_head_split_view(cache_mem_v[0]),
        dec_seq, (wa, wb, w2_p, pool_w_p, gmat_p, conv_taps, vecs))

    y_s = _sample_tail_call(x1_s, o_s, wb, w2_p, vecs)

    return (y_p, y_s.reshape(n_dec, dec_seq, D_MODEL),
            _from_head_split_view(k_p)[None], _from_head_split_view(v_p)[None],
            conv_p[None], conv_s.transpose(1, 0, 2)[None], pool_p[None], pool_s.transpose(1, 0, 2)[None])
```

```python
import functools

import jax
import jax.numpy as jnp
from jax import lax
from jax.experimental import pallas as pl
from jax.experimental.pallas import tpu as pltpu

F32 = jnp.float32
BF16 = jnp.bfloat16
U32 = jnp.uint32

LANES = 128
SUBLANES = 8
D_MODEL = 1024
D_CONV = 512
D_POOL = 512
D_IN = 2 * D_CONV + D_POOL
CONV_WIDTH = 31
CONV_CTX = CONV_WIDTH - 1
CONV_HEADS = 8
CONV_HEAD_DIM = D_CONV // CONV_HEADS
POOL_WINDOWS = (2, 4, 8, 16)
POOL_GROUP = D_POOL // len(POOL_WINDOWS)
POOL_CTX = max(POOL_WINDOWS) - 1
N_MEM = 256
X_HEADS = 4
X_HEAD_DIM = D_MODEL // X_HEADS
HEAD_CHUNKS = X_HEAD_DIM // LANES
KV_ROWS = N_MEM * X_HEADS * HEAD_CHUNKS
KV_ROW_STRIDE = X_HEADS * HEAD_CHUNKS
D_FF = 4 * D_MODEL
LN_EPS = 1e-5
DEPTH = 1
PAST_LEN = 16384
DN_ALPHA = (2.0 * DEPTH) ** 0.25
ATTN_SCALE = X_HEAD_DIM ** -0.5

assert POOL_GROUP == LANES and X_HEAD_DIM % LANES == 0

CONV_SLABS = D_CONV // LANES
POOL_SLABS = D_POOL // LANES
CONV_PAD = 32
POOL_PAD = 16
GN_HALF = 256

PROMPT_BLOCK = 256
CONV_ROWS = 32
SAMPLE_MIX_BATCH = 32
SAMPLE_ATTN_ROWS = 16
TAIL_FF_CHUNK = 1024
CAST_STEPS = 8
VMEM_LIMIT = 56 * 1024 * 1024


def _unpack(w_u32):
    return pltpu.bitcast(w_u32, BF16)


def _pack(w_bf16):
    return pltpu.bitcast(w_bf16, U32)


def _bdot(a, w):
    return jnp.dot(a.astype(BF16), w, preferred_element_type=F32)


def _layer_norm(x, g, b):
    mu = jnp.mean(x, axis=-1, keepdims=True)
    d = x - mu
    var = jnp.mean(d * d, axis=-1, keepdims=True)
    return d * lax.rsqrt(var + LN_EPS) * g + b


def _group_norm_swish(y, gmat, gn_g, gn_b):
    inv = 1.0 / CONV_HEAD_DIM
    yc = y - _bdot(y, gmat) * inv
    rest = _bdot(yc, gmat) * inv
    var = _bdot(yc * yc, gmat) * inv - rest * rest
    d = yc - rest
    n = d * lax.rsqrt(var + LN_EPS) * gn_g + gn_b
    return n * jax.nn.sigmoid(n)


def _mixer_acts(ybuf, dbuf, mixbuf, gmat_ref, gn_g_ref, gn_b_ref, pool_w_ref, pool_scale_ref):
    half = POOL_GROUP // 2
    for g in range(len(POOL_WINDOWS)):
        sl = slice(g * POOL_GROUP, (g + 1) * POOL_GROUP)
        pw = _unpack(pool_w_ref[g * half:(g + 1) * half, :])
        mixed = _bdot(dbuf[:, sl], pw) * pool_scale_ref[:, sl]
        mixbuf[:, D_CONV + g * POOL_GROUP:D_CONV + (g + 1) * POOL_GROUP] = mixed.astype(BF16)
    m = ybuf.shape[0]
    n_half = D_CONV // GN_HALF
    cols = [slice(j * GN_HALF, (j + 1) * GN_HALF) for j in range(n_half)]
    o = _group_norm_swish(
        jnp.concatenate([ybuf[:, sl] for sl in cols], axis=0), _unpack(gmat_ref[...]),
        jnp.concatenate([jnp.broadcast_to(gn_g_ref[:, sl], (m, GN_HALF)) for sl in cols], axis=0),
        jnp.concatenate([jnp.broadcast_to(gn_b_ref[:, sl], (m, GN_HALF)) for sl in cols], axis=0))
    for j, sl in enumerate(cols):
        mixbuf[:, sl] = o[j * m:(j + 1) * m, :].astype(BF16)


def _mixer_out(mixbuf, x, w_out_ref):
    return DN_ALPHA * x + jnp.dot(mixbuf[...], _unpack(w_out_ref[...]), preferred_element_type=F32)


def _const_spec(shape):
    nd = len(shape)
    return pl.BlockSpec(shape, lambda *_: (0,) * nd, pipeline_mode=pl.Buffered(1))


def _params(n_axes):
    return pltpu.CompilerParams(dimension_semantics=("arbitrary",) * n_axes, vmem_limit_bytes=VMEM_LIMIT)


VEC_FIELDS = (("b1", D_FF), ("b_in", D_IN), ("conv_b", D_CONV), ("gn_g", D_CONV), ("gn_b", D_CONV),
              ("pool_scale", D_POOL), ("ln1_g", D_MODEL), ("ln1_b", D_MODEL), ("ln2_g", D_MODEL),
              ("ln2_b", D_MODEL), ("b2", D_MODEL), ("ln3_g", D_MODEL), ("ln3_b", D_MODEL))
VEC_LEN = sum(n for _, n in VEC_FIELDS)
WA_FIELDS = (("w_in", D_IN), ("w_out", D_MODEL), ("xq", D_MODEL))
WB_FIELDS = (("xo", D_MODEL), ("w1", D_FF))


def _views(ref, fields):
    out, off = {}, 0
    for name, n in fields:
        out[name] = ref.at[:, off:off + n]
        off += n
    return out


def _cast_kernel(*refs, widths, n_vecs):
    n_mats = sum(len(g) for g in widths)
    srcs, vec_srcs = refs[:n_mats], refs[n_mats:n_mats + n_vecs]
    dsts, vec_dst = refs[n_mats + n_vecs:-1], refs[-1]
    i = 0
    for dst, group in zip(dsts, widths):
        off = 0
        for n in group:
            dst[:, off:off + n] = _pack(srcs[i][...].astype(BF16))
            off += n
            i += 1
    @pl.when(pl.program_id(0) == 0)
    def _():
        off = 0
        for src in vec_srcs:
            n = src.shape[1]
            vec_dst[:, off:off + n] = src[...]
            off += n


def _cast_call(groups, vectors):
    in_specs, out_specs, out_shape, widths = [], [], [], []
    for group in groups:
        k = group[0].shape[0]
        rows = k // CAST_STEPS
        assert rows * CAST_STEPS == k and rows % 16 == 0 and all(w.shape[0] == k for w in group)
        assert all(w.shape[1] % LANES == 0 for w in group)
        widths.append(tuple(w.shape[1] for w in group))
        total = sum(widths[-1])
        in_specs += [pl.BlockSpec((rows, w.shape[1]), lambda i: (i, 0)) for w in group]
        out_specs.append(pl.BlockSpec((rows // 2, total), lambda i: (i, 0)))
        out_shape.append(jax.ShapeDtypeStruct((k // 2, total), U32))
    assert all(v.shape[1] % LANES == 0 for v in vectors)
    vec_len = sum(v.shape[1] for v in vectors)
    in_specs += [pl.BlockSpec(v.shape, lambda i: (0, 0)) for v in vectors]
    out_specs.append(pl.BlockSpec((1, vec_len), lambda i: (0, 0)))
    out_shape.append(jax.ShapeDtypeStruct((1, vec_len), F32))
    return pl.pallas_call(
        functools.partial(_cast_kernel, widths=tuple(widths), n_vecs=len(vectors)), grid=(CAST_STEPS,),
        in_specs=in_specs, out_specs=out_specs, out_shape=out_shape, compiler_params=_params(1),
        name="cast_weights",
    )(*[w for group in groups for w in group], *vectors)


def _store_head_split(dst_ref, val):
    for hd in range(X_HEADS):
        for c in range(HEAD_CHUNKS):
            col = hd * X_HEAD_DIM + c * LANES
            dst_ref[0, pl.ds(c * X_HEADS + hd, N_MEM, stride=KV_ROW_STRIDE), :] = val[:, col:col + LANES]


def _kv_proj_kernel(mem_ref, wkv_ref, k_ref, v_ref, kt_ref, vb_ref):
    m = mem_ref[0].astype(BF16)
    k = jnp.dot(m, _unpack(wkv_ref[:, 0:D_MODEL]), preferred_element_type=F32)
    v = jnp.dot(m, _unpack(wkv_ref[:, D_MODEL:2 * D_MODEL]), preferred_element_type=F32)
    _store_head_split(k_ref, k)
    _store_head_split(v_ref, v)
    kt_ref[0] = _pack(k.T.astype(BF16))
    vb_ref[0] = _pack(v.astype(BF16))


def _kv_proj_call(mem, wkv):
    nb = mem.shape[0]
    blk = lambda s: pl.BlockSpec((1,) + s, lambda b: (b, 0, 0))
    return pl.pallas_call(
        _kv_proj_kernel,
        grid=(nb,),
        in_specs=[blk((N_MEM, D_MODEL)), _const_spec(wkv.shape)],
        out_specs=[blk((KV_ROWS, LANES)), blk((KV_ROWS, LANES)),
                   blk((D_MODEL // 2, N_MEM)), blk((N_MEM // 2, D_MODEL))],
        out_shape=[
            jax.ShapeDtypeStruct((nb, KV_ROWS, LANES), F32),
            jax.ShapeDtypeStruct((nb, KV_ROWS, LANES), F32),
            jax.ShapeDtypeStruct((nb, D_MODEL // 2, N_MEM), U32),
            jax.ShapeDtypeStruct((nb, N_MEM // 2, D_MODEL), U32),
        ],
        compiler_params=_params(1),
        name="kv_proj",
    )(mem, wkv)


def _prompt_kernel(x_ref, kt_ref, v_ref, qs_ref, kc_ref, vc_ref,
                   wa_ref, wb_ref, w2_ref, pool_w_ref, gmat_ref, conv_w_ref, vec_ref,
                   y_ref, conv_new_ref, pool_new_ref, os_ref,
                   ubuf, pbuf, ybuf, dbuf, mixbuf, obuf, x1buf, x1prev, x2buf, hbuf, *, nt):
    wa, wb, vec = _views(wa_ref, WA_FIELDS), _views(wb_ref, WB_FIELDS), _views(vec_ref, VEC_FIELDS)
    w_in_ref, w_out_ref, xq_ref, xo_ref, w1_ref = wa["w_in"], wa["w_out"], wa["xq"], wb["xo"], wb["w1"]
    b_in_ref, conv_b_ref, gn_g_ref, gn_b_ref = vec["b_in"], vec["conv_b"], vec["gn_g"], vec["gn_b"]
    pool_scale_ref, ln1_g_ref, ln1_b_ref = vec["pool_scale"], vec["ln1_g"], vec["ln1_b"]
    ln2_g_ref, ln2_b_ref, b1_ref, b2_ref = vec["ln2_g"], vec["ln2_b"], vec["b1"], vec["b2"]
    ln3_g_ref, ln3_b_ref = vec["ln3_g"], vec["ln3_b"]
    tm = PROMPT_BLOCK
    step = pl.program_id(0)
    n_blocks = pl.num_programs(0) - 1
    t = jnp.minimum(step, n_blocks - 1) % nt

    @pl.when(t == 0)
    def _():
        ubuf[:, 0:CONV_PAD, :] = jnp.zeros((CONV_SLABS, CONV_PAD, LANES), F32)
        pbuf[:, 0:POOL_PAD, :] = jnp.zeros((POOL_SLABS, POOL_PAD, LANES), F32)

    @pl.when(t > 0)
    def _():
        ubuf[:, 0:CONV_PAD, :] = ubuf[:, tm:tm + CONV_PAD, :]
        pbuf[:, 0:POOL_PAD, :] = pbuf[:, tm:tm + POOL_PAD, :]

    head = lambda hd: slice(hd * X_HEAD_DIM, (hd + 1) * X_HEAD_DIM)

    val = {}

    def mix_in():
        val["x"] = x_ref[0]
        h = _bdot(val["x"], _unpack(w_in_ref[...])) + b_in_ref[...]
        u = h[:, 0:D_CONV] * jax.nn.sigmoid(h[:, D_CONV:2 * D_CONV])
        for s in range(CONV_SLABS):
            ubuf[s, CONV_PAD:CONV_PAD + tm, :] = u[:, s * LANES:(s + 1) * LANES]
        for s in range(POOL_SLABS):
            pbuf[s, POOL_PAD:POOL_PAD + tm, :] = h[:, 2 * D_CONV + s * LANES:2 * D_CONV + (s + 1) * LANES]

    def att_scores():
        x1prev[...] = _layer_norm(x1buf[...], ln1_g_ref[...], ln1_b_ref[...])
        q = (_bdot(x1prev[...], _unpack(xq_ref[...])) * ATTN_SCALE).astype(BF16)
        val["scores"] = [
            jnp.dot(q[:, head(hd)], _unpack(kt_ref[0, hd * X_HEAD_DIM // 2:(hd + 1) * X_HEAD_DIM // 2, :]),
                    preferred_element_type=F32) for hd in range(X_HEADS)]

    def dec_probs():
        val["probs_dec"] = _sample_attn_probs(qs_ref, kc_ref)

    def att_values():
        probs = []
        for s in val["scores"]:
            e = jnp.exp(s - jnp.max(s, axis=-1, keepdims=True))
            probs.append((e / jnp.sum(e, axis=-1, keepdims=True)).astype(BF16))
        for hd in range(X_HEADS):
            obuf[:, head(hd)] = jnp.dot(probs[hd], _unpack(v_ref[0, :, head(hd)]),
                                        preferred_element_type=F32).astype(BF16)

    def mix_windows():
        base = CONV_PAD - CONV_CTX
        for s in range(CONV_SLABS):
            sl = slice(s * LANES, (s + 1) * LANES)
            for c in range(tm // CONV_ROWS):
                r0 = c * CONV_ROWS
                acc = jnp.broadcast_to(conv_b_ref[:, sl], (CONV_ROWS, LANES))
                for k in range(CONV_WIDTH):
                    acc = acc + ubuf[s, base + r0 + k:base + r0 + k + CONV_ROWS, :] * conv_w_ref[k, :, sl]
                ybuf[r0:r0 + CONV_ROWS, sl] = acc

        pos = t * tm + lax.broadcasted_iota(jnp.int32, (tm, POOL_GROUP), 0)
        for gi, w in enumerate(POOL_WINDOWS):
            cur = pbuf[gi, POOL_PAD:POOL_PAD + tm, :]
            ws = cur
            for j in range(1, w):
                ws = ws + pbuf[gi, POOL_PAD - j:POOL_PAD - j + tm, :]
            cnt = jnp.minimum(pos + 1, w).astype(F32)
            dbuf[:, gi * POOL_GROUP:(gi + 1) * POOL_GROUP] = ws / cnt - cur

    def att_out():
        val["attn"] = jnp.dot(obuf[...], _unpack(xo_ref[...]), preferred_element_type=F32)

    def dec_values():
        _sample_attn_values(val["probs_dec"], vc_ref, os_ref, os_ref.shape[0] // vc_ref.shape[0])

    def ffn_norm():
        x2buf[...] = _layer_norm(DN_ALPHA * x1prev[...] + val["attn"], ln2_g_ref[...], ln2_b_ref[...])

    def ffn_up():
        hdn = jnp.maximum(_bdot(x2buf[...], _unpack(w1_ref[...])) + b1_ref[...], 0.0)
        hbuf[...] = (hdn * hdn).astype(BF16)

    def mix_acts():
        _mixer_acts(ybuf, dbuf, mixbuf, gmat_ref, gn_g_ref, gn_b_ref, pool_w_ref, pool_scale_ref)

    def ffn_down():
        val["f"] = jnp.dot(hbuf[...], _unpack(w2_ref[...]), preferred_element_type=F32) + b2_ref[...]

    def mix_out():
        x1buf[...] = _mixer_out(mixbuf, val["x"], w_out_ref)

    def out_norm():
        y_ref[0] = _layer_norm(DN_ALPHA * x2buf[...] + val["f"], ln3_g_ref[...], ln3_b_ref[...])

    @pl.when(step == 0)
    def _():
        x1buf[...] = jnp.zeros((tm, D_MODEL), F32)

    for piece in (mix_in, att_scores, dec_probs, att_values, mix_windows, att_out, dec_values, ffn_norm, ffn_up,
                  mix_acts, ffn_down, mix_out, out_norm):
        piece()

    @pl.when(jnp.logical_and(t == nt - 1, step < n_blocks))
    def _():
        for s in range(CONV_SLABS):
            conv_new_ref[0, :, s * LANES:(s + 1) * LANES] = ubuf[s, CONV_PAD + tm - CONV_CTX:CONV_PAD + tm, :]
        for s in range(POOL_SLABS):
            pool_new_ref[0, :, s * LANES:(s + 1) * LANES] = pbuf[s, POOL_PAD + tm - POOL_CTX:POOL_PAD + tm, :]


def _prompt_call(x, kt, vb, q_dec, k_dec, v_dec, dec_seq, weights):
    nb, seq, _ = x.shape
    tm = PROMPT_BLOCK
    nt = seq // tm
    n_blocks = nb * nt
    n_dec = k_dec.shape[0]
    dec_per_step = -(-n_dec // n_blocks)
    dec_steps = n_dec // dec_per_step
    dec_rows = dec_per_step * dec_seq
    assert dec_steps * dec_per_step == n_dec and dec_steps <= n_blocks + 1
    assert dec_rows % SUBLANES == 0 and SAMPLE_ATTN_ROWS % dec_rows == 0
    cur = lambda g: jnp.minimum(g, n_blocks - 1)
    prev = lambda g: jnp.maximum(g - 1, 0)
    dec = lambda g: jnp.minimum(g, dec_steps - 1)
    in_specs = [
        pl.BlockSpec((1, tm, D_MODEL), lambda g: (cur(g) // nt, cur(g) % nt, 0)),
        pl.BlockSpec((1, D_MODEL // 2, N_MEM), lambda g: (prev(g) // nt, 0, 0)),
        pl.BlockSpec((1, N_MEM // 2, D_MODEL), lambda g: (prev(g) // nt, 0, 0)),
        pl.BlockSpec((dec_rows, D_MODEL), lambda g: (dec(g), 0)),
        pl.BlockSpec((dec_per_step, KV_ROWS, LANES), lambda g: (dec(g), 0, 0)),
        pl.BlockSpec((dec_per_step, KV_ROWS, LANES), lambda g: (dec(g), 0, 0)),
    ] + [_const_spec(w.shape) for w in weights]
    out_specs = [
        pl.BlockSpec((1, tm, D_MODEL), lambda g: (prev(g) // nt, prev(g) % nt, 0)),
        pl.BlockSpec((1, CONV_CTX, D_CONV), lambda g: (cur(g) // nt, 0, 0)),
        pl.BlockSpec((1, POOL_CTX, D_POOL), lambda g: (cur(g) // nt, 0, 0)),
        pl.BlockSpec((dec_rows, D_MODEL), lambda g: (dec(g), 0)),
    ]
    out_shape = [
        jax.ShapeDtypeStruct((nb, seq, D_MODEL), F32),
        jax.ShapeDtypeStruct((nb, CONV_CTX, D_CONV), F32),
        jax.ShapeDtypeStruct((nb, POOL_CTX, D_POOL), F32),
        jax.ShapeDtypeStruct(q_dec.shape, F32),
    ]
    scratch = [
        pltpu.VMEM((CONV_SLABS, CONV_PAD + tm, LANES), F32),
        pltpu.VMEM((POOL_SLABS, POOL_PAD + tm, LANES), F32),
        pltpu.VMEM((tm, D_CONV), F32),
        pltpu.VMEM((tm, D_POOL), F32),
        pltpu.VMEM((tm, D_MODEL), BF16),
        pltpu.VMEM((tm, D_MODEL), BF16),
        pltpu.VMEM((tm, D_MODEL), F32),
        pltpu.VMEM((tm, D_MODEL), F32),
        pltpu.VMEM((tm, D_MODEL), F32),
        pltpu.VMEM((tm, D_FF), BF16),
    ]
    return pl.pallas_call(
        functools.partial(_prompt_kernel, nt=nt),
        grid=(n_blocks + 1,),
        in_specs=in_specs,
        out_specs=out_specs,
        out_shape=out_shape,
        scratch_shapes=scratch,
        compiler_params=_params(1),
        name="prompt_layer",
    )(x, kt, vb, q_dec, k_dec, v_dec, *weights)


def _sample_mixer_kernel(x_ref, sconv_ref, spool_ref,
                         wa_ref, pool_w_ref, gmat_ref, conv_w_ref, vec_ref,
                         x1_ref, q_ref, conv_new_ref, pool_new_ref,
                         hbuf, yslab, dslab, ybuf, dbuf, mixbuf, *, dec_seq):
    wa, vec = _views(wa_ref, WA_FIELDS), _views(vec_ref, VEC_FIELDS)
    w_in_ref, w_out_ref, xq_ref = wa["w_in"], wa["w_out"], wa["xq"]
    b_in_ref, conv_b_ref, gn_g_ref, gn_b_ref = vec["b_in"], vec["conv_b"], vec["gn_g"], vec["gn_b"]
    pool_scale_ref, ln1_g_ref, ln1_b_ref = vec["pool_scale"], vec["ln1_g"], vec["ln1_b"]
    gb = SAMPLE_MIX_BATCH
    ts = dec_seq
    x = x_ref[...]
    h = _bdot(x, _unpack(w_in_ref[...])) + b_in_ref[...]
    u = h[:, 0:D_CONV] * jax.nn.sigmoid(h[:, D_CONV:2 * D_CONV])
    for s in range(CONV_SLABS):
        hbuf[s] = u[:, s * LANES:(s + 1) * LANES]
    for s in range(POOL_SLABS):
        hbuf[CONV_SLABS + s] = h[:, 2 * D_CONV + s * LANES:2 * D_CONV + (s + 1) * LANES]

    def step_rows(slab, t):
        return hbuf[slab, pl.ds(t, gb, stride=ts), :]

    for s in range(CONV_SLABS):
        sl = slice(s * LANES, (s + 1) * LANES)
        new = [step_rows(s, t) for t in range(ts)]
        ext = lambda j: sconv_ref[j, :, sl] if j < CONV_CTX else new[j - CONV_CTX]
        for j in range(CONV_CTX - ts):
            conv_new_ref[j, :, sl] = sconv_ref[j + ts, :, sl]
        for t in range(ts):
            conv_new_ref[CONV_CTX - ts + t, :, sl] = new[t]
            acc = jnp.broadcast_to(conv_b_ref[:, sl], (gb, LANES))
            for k in range(CONV_WIDTH):
                acc = acc + ext(t + k) * conv_w_ref[k, :, sl]
            yslab[s, pl.ds(t, gb, stride=ts), :] = acc

    for g, w in enumerate(POOL_WINDOWS):
        sl = slice(g * LANES, (g + 1) * LANES)
        new = [step_rows(CONV_SLABS + g, t) for t in range(ts)]
        ext = lambda j: spool_ref[j, :, sl] if j < POOL_CTX else new[j - POOL_CTX]
        for j in range(POOL_CTX - ts):
            pool_new_ref[j, :, sl] = spool_ref[j + ts, :, sl]
        for t in range(ts):
            pool_new_ref[POOL_CTX - ts + t, :, sl] = new[t]
            ws = new[t]
            for j in range(1, w):
                ws = ws + ext(POOL_CTX + t - j)
            cnt = float(min(PAST_LEN + t + 1, w))
            dslab[g, pl.ds(t, gb, stride=ts), :] = ws / cnt - new[t]

    for s in range(CONV_SLABS):
        ybuf[:, s * LANES:(s + 1) * LANES] = yslab[s]
    for s in range(POOL_SLABS):
        dbuf[:, s * LANES:(s + 1) * LANES] = dslab[s]

    _mixer_acts(ybuf, dbuf, mixbuf, gmat_ref, gn_g_ref, gn_b_ref, pool_w_ref, pool_scale_ref)
    x1 = _layer_norm(_mixer_out(mixbuf, x, w_out_ref), ln1_g_ref[...], ln1_b_ref[...])
    x1_ref[...] = x1
    q_ref[...] = _bdot(x1, _unpack(xq_ref[...])) * ATTN_SCALE


def _sample_mixer_call(xs, sconv_tm, spool_tm, weights, dec_seq):
    n_tok = xs.shape[0]
    nb = n_tok // dec_seq
    gb = SAMPLE_MIX_BATCH
    rows = gb * dec_seq
    row_spec = pl.BlockSpec((rows, D_MODEL), lambda i: (i, 0))
    conv_spec = pl.BlockSpec((CONV_CTX, gb, D_CONV), lambda i: (0, i, 0))
    pool_spec = pl.BlockSpec((POOL_CTX, gb, D_POOL), lambda i: (0, i, 0))
    out_shape = [
        jax.ShapeDtypeStruct((n_tok, D_MODEL), F32),
        jax.ShapeDtypeStruct((n_tok, D_MODEL), F32),
        jax.ShapeDtypeStruct((CONV_CTX, nb, D_CONV), F32),
        jax.ShapeDtypeStruct((POOL_CTX, nb, D_POOL), F32),
    ]
    scratch = [
        pltpu.VMEM((CONV_SLABS + POOL_SLABS, rows, LANES), F32),
        pltpu.VMEM((CONV_SLABS, rows, LANES), F32),
        pltpu.VMEM((POOL_SLABS, rows, LANES), F32),
        pltpu.VMEM((rows, D_CONV), F32),
        pltpu.VMEM((rows, D_POOL), F32),
        pltpu.VMEM((rows, D_MODEL), BF16),
    ]
    return pl.pallas_call(
        functools.partial(_sample_mixer_kernel, dec_seq=dec_seq),
        grid=(nb // gb,),
        in_specs=[row_spec, conv_spec, pool_spec] + [_const_spec(w.shape) for w in weights],
        out_specs=[row_spec, row_spec, conv_spec, pool_spec],
        out_shape=out_shape,
        scratch_shapes=scratch,
        compiler_params=_params(1),
        name="sample_mixer",
    )(xs, sconv_tm, spool_tm, *weights)


def _load_head(ref, b, hd):
    parts = [ref[b, pl.ds(c * X_HEADS + hd, N_MEM, stride=KV_ROW_STRIDE), :] for c in range(HEAD_CHUNKS)]
    return jnp.concatenate(parts, axis=1).astype(BF16)


def _sample_attn_probs(q_ref, k_ref):
    q8 = q_ref[...]
    q = jnp.concatenate([q8] * (SAMPLE_ATTN_ROWS // q8.shape[0]), axis=0).astype(BF16)
    s = jnp.concatenate(
        [lax.dot_general(q[:, hd * X_HEAD_DIM:(hd + 1) * X_HEAD_DIM], _load_head(k_ref, b, hd),
                         (((1,), (1,)), ((), ())), preferred_element_type=F32)
         for b in range(k_ref.shape[0]) for hd in range(X_HEADS)], axis=0)
    e = jnp.exp(s - jnp.max(s, axis=-1, keepdims=True))
    return (e / jnp.sum(e, axis=-1, keepdims=True)).astype(BF16)


def _sample_attn_values(a, v_ref, o_ref, ts):
    for b in range(v_ref.shape[0]):
        for hd in range(X_HEADS):
            i = b * X_HEADS + hd
            o = jnp.dot(a[i * SAMPLE_ATTN_ROWS:(i + 1) * SAMPLE_ATTN_ROWS, :], _load_head(v_ref, b, hd),
                        preferred_element_type=F32)
            o_ref[b * ts:(b + 1) * ts, hd * X_HEAD_DIM:(hd + 1) * X_HEAD_DIM] = o[b * ts:(b + 1) * ts, :]


def _sample_tail_kernel(x1_ref, o_ref, xo_ref, w1_ref, b1_ref, w2_ref, vec_ref, y_ref, x2buf, x2bf, fbuf):
    vec = _views(vec_ref, VEC_FIELDS)
    ln2_g_ref, ln2_b_ref, b2_ref, ln3_g_ref, ln3_b_ref = (
        vec["ln2_g"], vec["ln2_b"], vec["b2"], vec["ln3_g"], vec["ln3_b"])
    c = pl.program_id(0)

    @pl.when(c == 0)
    def _():
        attn = _bdot(o_ref[...], _unpack(xo_ref[...]))
        x2 = _layer_norm(DN_ALPHA * x1_ref[...] + attn, ln2_g_ref[...], ln2_b_ref[...])
        x2buf[...] = x2
        x2bf[...] = x2.astype(BF16)
        fbuf[...] = jnp.broadcast_to(b2_ref[...], fbuf.shape)

    hdn = jnp.maximum(jnp.dot(x2bf[...], _unpack(w1_ref[...]), preferred_element_type=F32) + b1_ref[...], 0.0)
    fbuf[...] += _bdot(hdn * hdn, _unpack(w2_ref[...]))

    @pl.when(c == pl.num_programs(0) - 1)
    def _():
        y_ref[...] = _layer_norm(DN_ALPHA * x2buf[...] + fbuf[...], ln3_g_ref[...], ln3_b_ref[...])


def _sample_tail_call(x1, o, wb, w2_p, vecs):
    n_tok = x1.shape[0]
    fc = TAIL_FF_CHUNK
    assert fc == D_MODEL and VEC_FIELDS[0] == ("b1", D_FF) and WB_FIELDS[0] == ("xo", D_MODEL)
    in_specs = [
        _const_spec(x1.shape), _const_spec(o.shape),
        pl.BlockSpec((D_MODEL // 2, D_MODEL), lambda c: (0, 0), pipeline_mode=pl.Buffered(1)),
        pl.BlockSpec((D_MODEL // 2, fc), lambda c: (0, c + 1)),
        pl.BlockSpec((1, fc), lambda c: (0, c)),
        pl.BlockSpec((fc // 2, D_MODEL), lambda c: (c, 0)),
        _const_spec(vecs.shape),
    ]
    return pl.pallas_call(
        _sample_tail_kernel,
        grid=(D_FF // fc,),
        in_specs=in_specs,
        out_specs=pl.BlockSpec((n_tok, D_MODEL), lambda c: (0, 0)),
        out_shape=jax.ShapeDtypeStruct((n_tok, D_MODEL), F32),
        scratch_shapes=[pltpu.VMEM((n_tok, D_MODEL), F32), pltpu.VMEM((n_tok, D_MODEL), BF16),
                        pltpu.VMEM((n_tok, D_MODEL), F32)],
        compiler_params=_params(1),
        name="sample_tail",
    )(x1, o, wb, wb, vecs, w2_p, vecs)


def _group_indicator():
    gid = jnp.arange(GN_HALF) // CONV_HEAD_DIM
    return (gid[:, None] == gid[None, :]).astype(F32)


def _to_head_split_view(a):
    lead = a.shape[:-3]
    n = len(lead)
    a = a.reshape(lead + (N_MEM, X_HEADS, HEAD_CHUNKS, LANES))
    a = a.transpose(tuple(range(n)) + (n, n + 2, n + 1, n + 3))
    return a.reshape(lead + (KV_ROWS, LANES))


def _from_head_split_view(a):
    lead = a.shape[:-2]
    n = len(lead)
    a = a.reshape(lead + (N_MEM, HEAD_CHUNKS, X_HEADS, LANES))
    a = a.transpose(tuple(range(n)) + (n, n + 2, n + 1, n + 3))
    return a.reshape(lead + (N_MEM, X_HEADS, X_HEAD_DIM))


def kernel(x_prompt, x_sample, mem_prompt, cache_mem_k, cache_mem_v, state_conv, state_pool, w_in, b_in, conv_w, conv_b, gn_g, gn_b, pool_w, pool_scale, w_out, ln1_g, ln1_b, xq_w, xk_w, xv_w, xo_w, ln2_g, ln2_b, w1, b1, w2, b2, ln3_g, ln3_b):
    assert w_in.shape[0] == DEPTH == 1
    n_prompt, seq, _ = x_prompt.shape
    n_dec, dec_seq, _ = x_sample.shape
    assert seq % PROMPT_BLOCK == 0 and n_dec % SAMPLE_MIX_BATCH == 0

    named = dict(w_in=w_in, w_out=w_out, xq=xq_w, xo=xo_w, w1=w1, b1=b1, b_in=b_in, conv_b=conv_b, gn_g=gn_g,
                 gn_b=gn_b, pool_scale=pool_scale, ln1_g=ln1_g, ln1_b=ln1_b, ln2_g=ln2_g, ln2_b=ln2_b, b2=b2,
                 ln3_g=ln3_g, ln3_b=ln3_b)
    wa, wb, wkv, pool_w_p, gmat_p, w2_p, vecs = _cast_call(
        [[named[name][0] for name, _ in WA_FIELDS], [named[name][0] for name, _ in WB_FIELDS],
         [xk_w[0], xv_w[0]], [pool_w[0].reshape(len(POOL_WINDOWS) * POOL_GROUP, POOL_GROUP)],
         [_group_indicator()], [w2[0]]],
        [named[name] for name, _ in VEC_FIELDS])
    conv_taps = conv_w.transpose(1, 0, 2)

    xs = x_sample.reshape(n_dec * dec_seq, D_MODEL)
    x1_s, q_s, conv_s, pool_s = _sample_mixer_call(
        xs, state_conv[0].transpose(1, 0, 2), state_pool[0].transpose(1, 0, 2),
        (wa, pool_w_p, gmat_p, conv_taps, vecs), dec_seq)

    k_p, v_p, kt_p, vb_p = _kv_proj_call(mem_prompt, wkv)
    y_p, conv_p, pool_p, o_s = _prompt_call(
        x_prompt, kt_p, vb_p, q_s, _to_head_split_view(cache_mem_k[0]), _to_head_split_view(cache_mem_v[0]),
        dec_seq, (wa, wb, w2_p, pool_w_p, gmat_p, conv_taps, vecs))

    y_s = _sample_tail_call(x1_s, o_s, wb, w2_p, vecs)

    return (y_p, y_s.reshape(n_dec, dec_seq, D_MODEL),
            _from_head_split_view(k_p)[None], _from_head_split_view(v_p)[None],
            conv_p[None], conv_s.transpose(1, 0, 2)[None], pool_p[None], pool_s.transpose(1, 0, 2)[None])
```

```python
import functools

import jax
import jax.numpy as jnp
from jax import lax
from jax.experimental import pallas as pl
from jax.experimental.pallas import tpu as pltpu

F32 = jnp.float32
BF16 = jnp.bfloat16
U32 = jnp.uint32

LANES = 128
SUBLANES = 8
D_MODEL = 1024
D_CONV = 512
D_POOL = 512
D_IN = 2 * D_CONV + D_POOL
CONV_WIDTH = 31
CONV_CTX = CONV_WIDTH - 1
CONV_HEADS = 8
CONV_HEAD_DIM = D_CONV // CONV_HEADS
POOL_WINDOWS = (2, 4, 8, 16)
POOL_GROUP = D_POOL // len(POOL_WINDOWS)
POOL_CTX = max(POOL_WINDOWS) - 1
N_MEM = 256
X_HEADS = 4
X_HEAD_DIM = D_MODEL // X_HEADS
HEAD_CHUNKS = X_HEAD_DIM // LANES
KV_ROWS = N_MEM * X_HEADS * HEAD_CHUNKS
KV_ROW_STRIDE = X_HEADS * HEAD_CHUNKS
D_FF = 4 * D_MODEL
LN_EPS = 1e-5
DEPTH = 1
PAST_LEN = 16384
DN_ALPHA = (2.0 * DEPTH) ** 0.25
ATTN_SCALE = X_HEAD_DIM ** -0.5

assert POOL_GROUP == LANES and X_HEAD_DIM % LANES == 0

CONV_SLABS = D_CONV // LANES
POOL_SLABS = D_POOL // LANES
CONV_PAD = 32
POOL_PAD = 16
GN_HALF = 256

PROMPT_BLOCK = 256
CONV_ROWS = 32
SAMPLE_MIX_BATCH = 32
SAMPLE_ATTN_ROWS = 16
TAIL_FF_CHUNK = 1024
CAST_STEPS = 8
VMEM_LIMIT = 56 * 1024 * 1024


def _unpack(w_u32):
    return pltpu.bitcast(w_u32, BF16)


def _pack(w_bf16):
    return pltpu.bitcast(w_bf16, U32)


def _bdot(a, w):
    return jnp.dot(a.astype(BF16), w, preferred_element_type=F32)


def _layer_norm(x, g, b):
    mu = jnp.mean(x, axis=-1, keepdims=True)
    d = x - mu
    var = jnp.mean(d * d, axis=-1, keepdims=True)
    return d * lax.rsqrt(var + LN_EPS) * g + b


def _group_norm_swish(y, gmat, gn_g, gn_b):
    inv = 1.0 / CONV_HEAD_DIM
    yc = y - _bdot(y, gmat) * inv
    rest = _bdot(yc, gmat) * inv
    var = _bdot(yc * yc, gmat) * inv - rest * rest
    d = yc - rest
    n = d * lax.rsqrt(var + LN_EPS) * gn_g + gn_b
    return n * jax.nn.sigmoid(n)


def _mixer_acts(ybuf, dbuf, mixbuf, gmat_ref, gn_g_ref, gn_b_ref):
    mixbuf[:, D_CONV:] = dbuf[...].astype(BF16)
    m = ybuf.shape[0]
    n_half = D_CONV // GN_HALF
    cols = [slice(j * GN_HALF, (j + 1) * GN_HALF) for j in range(n_half)]
    o = _group_norm_swish(
        jnp.concatenate([ybuf[:, sl] for sl in cols], axis=0), _unpack(gmat_ref[...]),
        jnp.concatenate([jnp.broadcast_to(gn_g_ref[:, sl], (m, GN_HALF)) for sl in cols], axis=0),
        jnp.concatenate([jnp.broadcast_to(gn_b_ref[:, sl], (m, GN_HALF)) for sl in cols], axis=0))
    for j, sl in enumerate(cols):
        mixbuf[:, sl] = o[j * m:(j + 1) * m, :].astype(BF16)


def _mixer_out(mixbuf, x, w_out_ref):
    return DN_ALPHA * x + jnp.dot(mixbuf[...], _unpack(w_out_ref[...]), preferred_element_type=F32)


def _const_spec(shape):
    nd = len(shape)
    return pl.BlockSpec(shape, lambda *_: (0,) * nd, pipeline_mode=pl.Buffered(1))


def _params(n_axes):
    return pltpu.CompilerParams(dimension_semantics=("arbitrary",) * n_axes, vmem_limit_bytes=VMEM_LIMIT)


VEC_FIELDS = (("b1", D_FF), ("b_in", D_IN), ("conv_b", D_CONV), ("gn_g", D_CONV), ("gn_b", D_CONV),
              ("ln1_g", D_MODEL), ("ln1_b", D_MODEL), ("ln2_g", D_MODEL),
              ("ln2_b", D_MODEL), ("b2", D_MODEL), ("ln3_g", D_MODEL), ("ln3_b", D_MODEL))
VEC_LEN = sum(n for _, n in VEC_FIELDS)
WA_FIELDS = (("w_in", D_IN), ("w_out", D_MODEL), ("xq", D_MODEL))
WB_FIELDS = (("xo", D_MODEL), ("w1", D_FF))


def _views(ref, fields):
    out, off = {}, 0
    for name, n in fields:
        out[name] = ref.at[:, off:off + n]
        off += n
    return out


def _cast_kernel(pool_w_ref, pool_scale_ref, *refs, widths, n_vecs, fold_index):
    n_mats = sum(len(g) for g in widths)
    srcs, vec_srcs = refs[:n_mats], refs[n_mats:n_mats + n_vecs]
    dsts, vec_dst = refs[n_mats + n_vecs:-1], refs[-1]
    step = pl.program_id(0)
    i = 0
    for dst, group in zip(dsts, widths):
        off = 0
        for n in group:
            w = srcs[i][...]
            if i == fold_index:
                folded = jnp.dot(pool_w_ref[0] * pool_scale_ref[...], w, precision=lax.Precision.HIGHEST,
                                 preferred_element_type=F32)
                w = jnp.where(step >= D_CONV // POOL_GROUP, folded, w)
            dst[:, off:off + n] = _pack(w.astype(BF16))
            off += n
            i += 1
    @pl.when(pl.program_id(0) == 0)
    def _():
        off = 0
        for src in vec_srcs:
            n = src.shape[1]
            vec_dst[:, off:off + n] = src[...]
            off += n


def _cast_call(groups, vectors, pool_w, pool_scale, fold_index):
    n_groups = pool_w.shape[0]
    first = D_CONV // POOL_GROUP
    assert D_MODEL // CAST_STEPS == POOL_GROUP and first + n_groups == CAST_STEPS
    grp = lambda i: jnp.clip(i - first, 0, n_groups - 1)
    in_specs = [pl.BlockSpec((1, POOL_GROUP, POOL_GROUP), lambda i: (grp(i), 0, 0)),
                pl.BlockSpec((1, POOL_GROUP), lambda i: (0, grp(i)))]
    out_specs, out_shape, widths = [], [], []
    for group in groups:
        k = group[0].shape[0]
        rows = k // CAST_STEPS
        assert rows * CAST_STEPS == k and rows % 16 == 0 and all(w.shape[0] == k for w in group)
        assert all(w.shape[1] % LANES == 0 for w in group)
        widths.append(tuple(w.shape[1] for w in group))
        total = sum(widths[-1])
        in_specs += [pl.BlockSpec((rows, w.shape[1]), lambda i: (i, 0)) for w in group]
        out_specs.append(pl.BlockSpec((rows // 2, total), lambda i: (i, 0)))
        out_shape.append(jax.ShapeDtypeStruct((k // 2, total), U32))
    assert all(v.shape[1] % LANES == 0 for v in vectors)
    vec_len = sum(v.shape[1] for v in vectors)
    in_specs += [pl.BlockSpec(v.shape, lambda i: (0, 0)) for v in vectors]
    out_specs.append(pl.BlockSpec((1, vec_len), lambda i: (0, 0)))
    out_shape.append(jax.ShapeDtypeStruct((1, vec_len), F32))
    return pl.pallas_call(
        functools.partial(_cast_kernel, widths=tuple(widths), n_vecs=len(vectors), fold_index=fold_index),
        grid=(CAST_STEPS,), in_specs=in_specs, out_specs=out_specs, out_shape=out_shape,
        compiler_params=_params(1), name="cast_weights",
    )(pool_w, pool_scale, *[w for group in groups for w in group], *vectors)


def _store_head_split(dst_ref, val):
    for hd in range(X_HEADS):
        for c in range(HEAD_CHUNKS):
            col = hd * X_HEAD_DIM + c * LANES
            dst_ref[0, pl.ds(c * X_HEADS + hd, N_MEM, stride=KV_ROW_STRIDE), :] = val[:, col:col + LANES]


def _kv_proj_kernel(mem_ref, wkv_ref, k_ref, v_ref, kt_ref, vb_ref):
    m = mem_ref[0].astype(BF16)
    k = jnp.dot(m, _unpack(wkv_ref[:, 0:D_MODEL]), preferred_element_type=F32)
    v = jnp.dot(m, _unpack(wkv_ref[:, D_MODEL:2 * D_MODEL]), preferred_element_type=F32)
    _store_head_split(k_ref, k)
    _store_head_split(v_ref, v)
    kt_ref[0] = _pack(k.T.astype(BF16))
    vb_ref[0] = _pack(v.astype(BF16))


def _kv_proj_call(mem, wkv):
    nb = mem.shape[0]
    blk = lambda s: pl.BlockSpec((1,) + s, lambda b: (b, 0, 0))
    return pl.pallas_call(
        _kv_proj_kernel,
        grid=(nb,),
        in_specs=[blk((N_MEM, D_MODEL)), _const_spec(wkv.shape)],
        out_specs=[blk((KV_ROWS, LANES)), blk((KV_ROWS, LANES)),
                   blk((D_MODEL // 2, N_MEM)), blk((N_MEM // 2, D_MODEL))],
        out_shape=[
            jax.ShapeDtypeStruct((nb, KV_ROWS, LANES), F32),
            jax.ShapeDtypeStruct((nb, KV_ROWS, LANES), F32),
            jax.ShapeDtypeStruct((nb, D_MODEL // 2, N_MEM), U32),
            jax.ShapeDtypeStruct((nb, N_MEM // 2, D_MODEL), U32),
        ],
        compiler_params=_params(1),
        name="kv_proj",
    )(mem, wkv)


def _prompt_kernel(x_ref, kt_ref, v_ref, qs_ref, kc_ref, vc_ref,
                   wa_ref, wb_ref, w2_ref, gmat_ref, conv_w_ref, vec_ref,
                   y_ref, conv_new_ref, pool_new_ref, os_ref,
                   ubuf, pbuf, ybuf, dbuf, mixbuf, obuf, x1buf, x1prev, x2buf, hbuf, *, nt):
    wa, wb, vec = _views(wa_ref, WA_FIELDS), _views(wb_ref, WB_FIELDS), _views(vec_ref, VEC_FIELDS)
    w_in_ref, w_out_ref, xq_ref, xo_ref, w1_ref = wa["w_in"], wa["w_out"], wa["xq"], wb["xo"], wb["w1"]
    b_in_ref, conv_b_ref, gn_g_ref, gn_b_ref = vec["b_in"], vec["conv_b"], vec["gn_g"], vec["gn_b"]
    ln1_g_ref, ln1_b_ref = vec["ln1_g"], vec["ln1_b"]
    ln2_g_ref, ln2_b_ref, b1_ref, b2_ref = vec["ln2_g"], vec["ln2_b"], vec["b1"], vec["b2"]
    ln3_g_ref, ln3_b_ref = vec["ln3_g"], vec["ln3_b"]
    tm = PROMPT_BLOCK
    step = pl.program_id(0)
    n_blocks = pl.num_programs(0) - 1
    t = jnp.minimum(step, n_blocks - 1) % nt

    @pl.when(t == 0)
    def _():
        ubuf[:, 0:CONV_PAD, :] = jnp.zeros((CONV_SLABS, CONV_PAD, LANES), F32)
        pbuf[:, 0:POOL_PAD, :] = jnp.zeros((POOL_SLABS, POOL_PAD, LANES), F32)

    @pl.when(t > 0)
    def _():
        ubuf[:, 0:CONV_PAD, :] = ubuf[:, tm:tm + CONV_PAD, :]
        pbuf[:, 0:POOL_PAD, :] = pbuf[:, tm:tm + POOL_PAD, :]

    head = lambda hd: slice(hd * X_HEAD_DIM, (hd + 1) * X_HEAD_DIM)

    val = {}

    def mix_in():
        val["x"] = x_ref[0]
        h = _bdot(val["x"], _unpack(w_in_ref[...])) + b_in_ref[...]
        u = h[:, 0:D_CONV] * jax.nn.sigmoid(h[:, D_CONV:2 * D_CONV])
        for s in range(CONV_SLABS):
            ubuf[s, CONV_PAD:CONV_PAD + tm, :] = u[:, s * LANES:(s + 1) * LANES]
        for s in range(POOL_SLABS):
            pbuf[s, POOL_PAD:POOL_PAD + tm, :] = h[:, 2 * D_CONV + s * LANES:2 * D_CONV + (s + 1) * LANES]

    def att_scores():
        x1prev[...] = _layer_norm(x1buf[...], ln1_g_ref[...], ln1_b_ref[...])
        q = (_bdot(x1prev[...], _unpack(xq_ref[...])) * ATTN_SCALE).astype(BF16)
        val["scores"] = [
            jnp.dot(q[:, head(hd)], _unpack(kt_ref[0, hd * X_HEAD_DIM // 2:(hd + 1) * X_HEAD_DIM // 2, :]),
                    preferred_element_type=F32) for hd in range(X_HEADS)]

    def dec_probs():
        val["probs_dec"] = _sample_attn_probs(qs_ref, kc_ref)

    def att_values():
        probs = []
        for s in val["scores"]:
            e = jnp.exp(s - jnp.max(s, axis=-1, keepdims=True))
            probs.append((e / jnp.sum(e, axis=-1, keepdims=True)).astype(BF16))
        for hd in range(X_HEADS):
            obuf[:, head(hd)] = jnp.dot(probs[hd], _unpack(v_ref[0, :, head(hd)]),
                                        preferred_element_type=F32).astype(BF16)

    def mix_windows():
        base = CONV_PAD - CONV_CTX
        for s in range(CONV_SLABS):
            sl = slice(s * LANES, (s + 1) * LANES)
            for c in range(tm // CONV_ROWS):
                r0 = c * CONV_ROWS
                acc = jnp.broadcast_to(conv_b_ref[:, sl], (CONV_ROWS, LANES))
                for k in range(CONV_WIDTH):
                    acc = acc + ubuf[s, base + r0 + k:base + r0 + k + CONV_ROWS, :] * conv_w_ref[k, :, sl]
                ybuf[r0:r0 + CONV_ROWS, sl] = acc

        pos = t * tm + lax.broadcasted_iota(jnp.int32, (tm, POOL_GROUP), 0)
        for gi, w in enumerate(POOL_WINDOWS):
            cur = pbuf[gi, POOL_PAD:POOL_PAD + tm, :]
            ws = cur
            for j in range(1, w):
                ws = ws + pbuf[gi, POOL_PAD - j:POOL_PAD - j + tm, :]
            cnt = jnp.minimum(pos + 1, w).astype(F32)
            dbuf[:, gi * POOL_GROUP:(gi + 1) * POOL_GROUP] = ws / cnt - cur

    def att_out():
        val["attn"] = jnp.dot(obuf[...], _unpack(xo_ref[...]), preferred_element_type=F32)

    def dec_values():
        _sample_attn_values(val["probs_dec"], vc_ref, os_ref, os_ref.shape[0] // vc_ref.shape[0])

    def ffn_norm():
        x2buf[...] = _layer_norm(DN_ALPHA * x1prev[...] + val["attn"], ln2_g_ref[...], ln2_b_ref[...])

    def ffn_up():
        hdn = jnp.maximum(_bdot(x2buf[...], _unpack(w1_ref[...])) + b1_ref[...], 0.0)
        hbuf[...] = (hdn * hdn).astype(BF16)

    def mix_acts():
        _mixer_acts(ybuf, dbuf, mixbuf, gmat_ref, gn_g_ref, gn_b_ref)

    def ffn_down():
        val["f"] = jnp.dot(hbuf[...], _unpack(w2_ref[...]), preferred_element_type=F32) + b2_ref[...]

    def mix_out():
        x1buf[...] = _mixer_out(mixbuf, val["x"], w_out_ref)

    def out_norm():
        y_ref[0] = _layer_norm(DN_ALPHA * x2buf[...] + val["f"], ln3_g_ref[...], ln3_b_ref[...])

    @pl.when(step == 0)
    def _():
        x1buf[...] = jnp.zeros((tm, D_MODEL), F32)

    for piece in (mix_in, att_scores, dec_probs, att_values, mix_windows, att_out, dec_values, ffn_norm, ffn_up,
                  mix_acts, ffn_down, mix_out, out_norm):
        piece()

    @pl.when(jnp.logical_and(t == nt - 1, step < n_blocks))
    def _():
        for s in range(CONV_SLABS):
            conv_new_ref[0, :, s * LANES:(s + 1) * LANES] = ubuf[s, CONV_PAD + tm - CONV_CTX:CONV_PAD + tm, :]
        for s in range(POOL_SLABS):
            pool_new_ref[0, :, s * LANES:(s + 1) * LANES] = pbuf[s, POOL_PAD + tm - POOL_CTX:POOL_PAD + tm, :]


def _prompt_call(x, kt, vb, q_dec, k_dec, v_dec, dec_seq, weights):
    nb, seq, _ = x.shape
    tm = PROMPT_BLOCK
    nt = seq // tm
    n_blocks = nb * nt
    n_dec = k_dec.shape[0]
    dec_per_step = -(-n_dec // n_blocks)
    dec_steps = n_dec // dec_per_step
    dec_rows = dec_per_step * dec_seq
    assert dec_steps * dec_per_step == n_dec and dec_steps <= n_blocks + 1
    assert dec_rows % SUBLANES == 0 and SAMPLE_ATTN_ROWS % dec_rows == 0
    cur = lambda g: jnp.minimum(g, n_blocks - 1)
    prev = lambda g: jnp.maximum(g - 1, 0)
    dec = lambda g: jnp.minimum(g, dec_steps - 1)
    in_specs = [
        pl.BlockSpec((1, tm, D_MODEL), lambda g: (cur(g) // nt, cur(g) % nt, 0)),
        pl.BlockSpec((1, D_MODEL // 2, N_MEM), lambda g: (prev(g) // nt, 0, 0)),
        pl.BlockSpec((1, N_MEM // 2, D_MODEL), lambda g: (prev(g) // nt, 0, 0)),
        pl.BlockSpec((dec_rows, D_MODEL), lambda g: (dec(g), 0)),
        pl.BlockSpec((dec_per_step, KV_ROWS, LANES), lambda g: (dec(g), 0, 0)),
        pl.BlockSpec((dec_per_step, KV_ROWS, LANES), lambda g: (dec(g), 0, 0)),
    ] + [_const_spec(w.shape) for w in weights]
    out_specs = [
        pl.BlockSpec((1, tm, D_MODEL), lambda g: (prev(g) // nt, prev(g) % nt, 0)),
        pl.BlockSpec((1, CONV_CTX, D_CONV), lambda g: (cur(g) // nt, 0, 0)),
        pl.BlockSpec((1, POOL_CTX, D_POOL), lambda g: (cur(g) // nt, 0, 0)),
        pl.BlockSpec((dec_rows, D_MODEL), lambda g: (dec(g), 0)),
    ]
    out_shape = [
        jax.ShapeDtypeStruct((nb, seq, D_MODEL), F32),
        jax.ShapeDtypeStruct((nb, CONV_CTX, D_CONV), F32),
        jax.ShapeDtypeStruct((nb, POOL_CTX, D_POOL), F32),
        jax.ShapeDtypeStruct(q_dec.shape, F32),
    ]
    scratch = [
        pltpu.VMEM((CONV_SLABS, CONV_PAD + tm, LANES), F32),
        pltpu.VMEM((POOL_SLABS, POOL_PAD + tm, LANES), F32),
        pltpu.VMEM((tm, D_CONV), F32),
        pltpu.VMEM((tm, D_POOL), F32),
        pltpu.VMEM((tm, D_MODEL), BF16),
        pltpu.VMEM((tm, D_MODEL), BF16),
        pltpu.VMEM((tm, D_MODEL), F32),
        pltpu.VMEM((tm, D_MODEL), F32),
        pltpu.VMEM((tm, D_MODEL), F32),
        pltpu.VMEM((tm, D_FF), BF16),
    ]
    return pl.pallas_call(
        functools.partial(_prompt_kernel, nt=nt),
        grid=(n_blocks + 1,),
        in_specs=in_specs,
        out_specs=out_specs,
        out_shape=out_shape,
        scratch_shapes=scratch,
        compiler_params=_params(1),
        name="prompt_layer",
    )(x, kt, vb, q_dec, k_dec, v_dec, *weights)


def _sample_mixer_kernel(x_ref, sconv_ref, spool_ref,
                         wa_ref, gmat_ref, conv_w_ref, vec_ref,
                         x1_ref, q_ref, conv_new_ref, pool_new_ref,
                         hbuf, yslab, dslab, ybuf, dbuf, mixbuf, *, dec_seq):
    wa, vec = _views(wa_ref, WA_FIELDS), _views(vec_ref, VEC_FIELDS)
    w_in_ref, w_out_ref, xq_ref = wa["w_in"], wa["w_out"], wa["xq"]
    b_in_ref, conv_b_ref, gn_g_ref, gn_b_ref = vec["b_in"], vec["conv_b"], vec["gn_g"], vec["gn_b"]
    ln1_g_ref, ln1_b_ref = vec["ln1_g"], vec["ln1_b"]
    gb = SAMPLE_MIX_BATCH
    ts = dec_seq
    x = x_ref[...]
    h = _bdot(x, _unpack(w_in_ref[...])) + b_in_ref[...]
    u = h[:, 0:D_CONV] * jax.nn.sigmoid(h[:, D_CONV:2 * D_CONV])
    for s in range(CONV_SLABS):
        hbuf[s] = u[:, s * LANES:(s + 1) * LANES]
    for s in range(POOL_SLABS):
        hbuf[CONV_SLABS + s] = h[:, 2 * D_CONV + s * LANES:2 * D_CONV + (s + 1) * LANES]

    def step_rows(slab, t):
        return hbuf[slab, pl.ds(t, gb, stride=ts), :]

    for s in range(CONV_SLABS):
        sl = slice(s * LANES, (s + 1) * LANES)
        new = [step_rows(s, t) for t in range(ts)]
        ext = lambda j: sconv_ref[j, :, sl] if j < CONV_CTX else new[j - CONV_CTX]
        for j in range(CONV_CTX - ts):
            conv_new_ref[j, :, sl] = sconv_ref[j + ts, :, sl]
        for t in range(ts):
            conv_new_ref[CONV_CTX - ts + t, :, sl] = new[t]
            acc = jnp.broadcast_to(conv_b_ref[:, sl], (gb, LANES))
            for k in range(CONV_WIDTH):
                acc = acc + ext(t + k) * conv_w_ref[k, :, sl]
            yslab[s, pl.ds(t, gb, stride=ts), :] = acc

    for g, w in enumerate(POOL_WINDOWS):
        sl = slice(g * LANES, (g + 1) * LANES)
        new = [step_rows(CONV_SLABS + g, t) for t in range(ts)]
        ext = lambda j: spool_ref[j, :, sl] if j < POOL_CTX else new[j - POOL_CTX]
        for j in range(POOL_CTX - ts):
            pool_new_ref[j, :, sl] = spool_ref[j + ts, :, sl]
        for t in range(ts):
            pool_new_ref[POOL_CTX - ts + t, :, sl] = new[t]
            ws = new[t]
            for j in range(1, w):
                ws = ws + ext(POOL_CTX + t - j)
            cnt = float(min(PAST_LEN + t + 1, w))
            dslab[g, pl.ds(t, gb, stride=ts), :] = ws / cnt - new[t]

    for s in range(CONV_SLABS):
        ybuf[:, s * LANES:(s + 1) * LANES] = yslab[s]
    for s in range(POOL_SLABS):
        dbuf[:, s * LANES:(s + 1) * LANES] = dslab[s]

    _mixer_acts(ybuf, dbuf, mixbuf, gmat_ref, gn_g_ref, gn_b_ref)
    x1 = _layer_norm(_mixer_out(mixbuf, x, w_out_ref), ln1_g_ref[...], ln1_b_ref[...])
    x1_ref[...] = x1
    q_ref[...] = _bdot(x1, _unpack(xq_ref[...])) * ATTN_SCALE


def _sample_mixer_call(xs, sconv_tm, spool_tm, weights, dec_seq):
    n_tok = xs.shape[0]
    nb = n_tok // dec_seq
    gb = SAMPLE_MIX_BATCH
    rows = gb * dec_seq
    row_spec = pl.BlockSpec((rows, D_MODEL), lambda i: (i, 0))
    conv_spec = pl.BlockSpec((CONV_CTX, gb, D_CONV), lambda i: (0, i, 0))
    pool_spec = pl.BlockSpec((POOL_CTX, gb, D_POOL), lambda i: (0, i, 0))
    out_shape = [
        jax.ShapeDtypeStruct((n_tok, D_MODEL), F32),
        jax.ShapeDtypeStruct((n_tok, D_MODEL), F32),
        jax.ShapeDtypeStruct((CONV_CTX, nb, D_CONV), F32),
        jax.ShapeDtypeStruct((POOL_CTX, nb, D_POOL), F32),
    ]
    scratch = [
        pltpu.VMEM((CONV_SLABS + POOL_SLABS, rows, LANES), F32),
        pltpu.VMEM((CONV_SLABS, rows, LANES), F32),
        pltpu.VMEM((POOL_SLABS, rows, LANES), F32),
        pltpu.VMEM((rows, D_CONV), F32),
        pltpu.VMEM((rows, D_POOL), F32),
        pltpu.VMEM((rows, D_MODEL), BF16),
    ]
    return pl.pallas_call(
        functools.partial(_sample_mixer_kernel, dec_seq=dec_seq),
        grid=(nb // gb,),
        in_specs=[row_spec, conv_spec, pool_spec] + [_const_spec(w.shape) for w in weights],
        out_specs=[row_spec, row_spec, conv_spec, pool_spec],
        out_shape=out_shape,
        scratch_shapes=scratch,
        compiler_params=_params(1),
        name="sample_mixer",
    )(xs, sconv_tm, spool_tm, *weights)


def _load_head(ref, b, hd):
    parts = [ref[b, pl.ds(c * X_HEADS + hd, N_MEM, stride=KV_ROW_STRIDE), :] for c in range(HEAD_CHUNKS)]
    return jnp.concatenate(parts, axis=1).astype(BF16)


def _sample_attn_probs(q_ref, k_ref):
    q8 = q_ref[...]
    q = jnp.concatenate([q8] * (SAMPLE_ATTN_ROWS // q8.shape[0]), axis=0).astype(BF16)
    s = jnp.concatenate(
        [lax.dot_general(q[:, hd * X_HEAD_DIM:(hd + 1) * X_HEAD_DIM], _load_head(k_ref, b, hd),
                         (((1,), (1,)), ((), ())), preferred_element_type=F32)
         for b in range(k_ref.shape[0]) for hd in range(X_HEADS)], axis=0)
    e = jnp.exp(s - jnp.max(s, axis=-1, keepdims=True))
    return (e / jnp.sum(e, axis=-1, keepdims=True)).astype(BF16)


def _sample_attn_values(a, v_ref, o_ref, ts):
    for b in range(v_ref.shape[0]):
        for hd in range(X_HEADS):
            i = b * X_HEADS + hd
            o = jnp.dot(a[i * SAMPLE_ATTN_ROWS:(i + 1) * SAMPLE_ATTN_ROWS, :], _load_head(v_ref, b, hd),
                        preferred_element_type=F32)
            o_ref[b * ts:(b + 1) * ts, hd * X_HEAD_DIM:(hd + 1) * X_HEAD_DIM] = o[b * ts:(b + 1) * ts, :]


def _sample_tail_kernel(x1_ref, o_ref, xo_ref, w1_ref, b1_ref, w2_ref, vec_ref, y_ref, x2buf, x2bf, fbuf):
    vec = _views(vec_ref, VEC_FIELDS)
    ln2_g_ref, ln2_b_ref, b2_ref, ln3_g_ref, ln3_b_ref = (
        vec["ln2_g"], vec["ln2_b"], vec["b2"], vec["ln3_g"], vec["ln3_b"])
    c = pl.program_id(0)

    @pl.when(c == 0)
    def _():
        attn = _bdot(o_ref[...], _unpack(xo_ref[...]))
        x2 = _layer_norm(DN_ALPHA * x1_ref[...] + attn, ln2_g_ref[...], ln2_b_ref[...])
        x2buf[...] = x2
        x2bf[...] = x2.astype(BF16)
        fbuf[...] = jnp.broadcast_to(b2_ref[...], fbuf.shape)

    hdn = jnp.maximum(jnp.dot(x2bf[...], _unpack(w1_ref[...]), preferred_element_type=F32) + b1_ref[...], 0.0)
    fbuf[...] += _bdot(hdn * hdn, _unpack(w2_ref[...]))

    @pl.when(c == pl.num_programs(0) - 1)
    def _():
        y_ref[...] = _layer_norm(DN_ALPHA * x2buf[...] + fbuf[...], ln3_g_ref[...], ln3_b_ref[...])


def _sample_tail_call(x1, o, wb, w2_p, vecs):
    n_tok = x1.shape[0]
    fc = TAIL_FF_CHUNK
    assert fc == D_MODEL and VEC_FIELDS[0] == ("b1", D_FF) and WB_FIELDS[0] == ("xo", D_MODEL)
    in_specs = [
        _const_spec(x1.shape), _const_spec(o.shape),
        pl.BlockSpec((D_MODEL // 2, D_MODEL), lambda c: (0, 0), pipeline_mode=pl.Buffered(1)),
        pl.BlockSpec((D_MODEL // 2, fc), lambda c: (0, c + 1)),
        pl.BlockSpec((1, fc), lambda c: (0, c)),
        pl.BlockSpec((fc // 2, D_MODEL), lambda c: (c, 0)),
        _const_spec(vecs.shape),
    ]
    return pl.pallas_call(
        _sample_tail_kernel,
        grid=(D_FF // fc,),
        in_specs=in_specs,
        out_specs=pl.BlockSpec((n_tok, D_MODEL), lambda c: (0, 0)),
        out_shape=jax.ShapeDtypeStruct((n_tok, D_MODEL), F32),
        scratch_shapes=[pltpu.VMEM((n_tok, D_MODEL), F32), pltpu.VMEM((n_tok, D_MODEL), BF16),
                        pltpu.VMEM((n_tok, D_MODEL), F32)],
        compiler_params=_params(1),
        name="sample_tail",
    )(x1, o, wb, wb, vecs, w2_p, vecs)


def _group_indicator():
    gid = jnp.arange(GN_HALF) // CONV_HEAD_DIM
    return (gid[:, None] == gid[None, :]).astype(F32)


def _to_head_split_view(a):
    lead = a.shape[:-3]
    n = len(lead)
    a = a.reshape(lead + (N_MEM, X_HEADS, HEAD_CHUNKS, LANES))
    a = a.transpose(tuple(range(n)) + (n, n + 2, n + 1, n + 3))
    return a.reshape(lead + (KV_ROWS, LANES))


def _from_head_split_view(a):
    lead = a.shape[:-2]
    n = len(lead)
    a = a.reshape(lead + (N_MEM, HEAD_CHUNKS, X_HEADS, LANES))
    a = a.transpose(tuple(range(n)) + (n, n + 2, n + 1, n + 3))
    return a.reshape(lead + (N_MEM, X_HEADS, X_HEAD_DIM))


def kernel(x_prompt, x_sample, mem_prompt, cache_mem_k, cache_mem_v, state_conv, state_pool, w_in, b_in, conv_w, conv_b, gn_g, gn_b, pool_w, pool_scale, w_out, ln1_g, ln1_b, xq_w, xk_w, xv_w, xo_w, ln2_g, ln2_b, w1, b1, w2, b2, ln3_g, ln3_b):
    assert w_in.shape[0] == DEPTH == 1
    n_prompt, seq, _ = x_prompt.shape
    n_dec, dec_seq, _ = x_sample.shape
    assert seq % PROMPT_BLOCK == 0 and n_dec % SAMPLE_MIX_BATCH == 0

    named = dict(w_in=w_in, w_out=w_out, xq=xq_w, xo=xo_w, w1=w1, b1=b1, b_in=b_in, conv_b=conv_b, gn_g=gn_g,
                 gn_b=gn_b, ln1_g=ln1_g, ln1_b=ln1_b, ln2_g=ln2_g, ln2_b=ln2_b, b2=b2,
                 ln3_g=ln3_g, ln3_b=ln3_b)
    wa, wb, wkv, gmat_p, w2_p, vecs = _cast_call(
        [[named[name][0] for name, _ in WA_FIELDS], [named[name][0] for name, _ in WB_FIELDS],
         [xk_w[0], xv_w[0]], [_group_indicator()], [w2[0]]],
        [named[name] for name, _ in VEC_FIELDS],
        pool_w[0], pool_scale, fold_index=[name for name, _ in WA_FIELDS].index("w_out"))
    conv_taps = conv_w.transpose(1, 0, 2)

    xs = x_sample.reshape(n_dec * dec_seq, D_MODEL)
    x1_s, q_s, conv_s, pool_s = _sample_mixer_call(
        xs, state_conv[0].transpose(1, 0, 2), state_pool[0].transpose(1, 0, 2),
        (wa, gmat_p, conv_taps, vecs), dec_seq)

    k_p, v_p, kt_p, vb_p = _kv_proj_call(mem_prompt, wkv)
    y_p, conv_p, pool_p, o_s = _prompt_call(
        x_prompt, kt_p, vb_p, q_s, _to_head_split_view(cache_mem_k[0]), _to_head_split_view(cache_mem_v[0]),
        dec_seq, (wa, wb, w2_p, gmat_p, conv_taps, vecs))

    y_s = _sample_tail_call(x1_s, o_s, wb, w2_p, vecs)

    return (y_p, y_s.reshape(n_dec, dec_seq, D_MODEL),
            _from_head_split_view(k_p)[None], _from_head_split_view(v_p)[None],
            conv_p[None], conv_s.transpose(1, 0, 2)[None], pool_p[None], pool_s.transpose(1, 0, 2)[None])
```

```python
import functools

import jax
import jax.numpy as jnp
from jax import lax
from jax.experimental import pallas as pl
from jax.experimental.pallas import tpu as pltpu

F32 = jnp.float32
BF16 = jnp.bfloat16
U32 = jnp.uint32

LANES = 128
SUBLANES = 8
D_MODEL = 1024
D_CONV = 512
D_POOL = 512
D_IN = 2 * D_CONV + D_POOL
CONV_WIDTH = 31
CONV_CTX = CONV_WIDTH - 1
CONV_HEADS = 8
CONV_HEAD_DIM = D_CONV // CONV_HEADS
POOL_WINDOWS = (2, 4, 8, 16)
POOL_GROUP = D_POOL // len(POOL_WINDOWS)
POOL_CTX = max(POOL_WINDOWS) - 1
N_MEM = 256
X_HEADS = 4
X_HEAD_DIM = D_MODEL // X_HEADS
HEAD_CHUNKS = X_HEAD_DIM // LANES
KV_ROWS = N_MEM * X_HEADS * HEAD_CHUNKS
KV_ROW_STRIDE = X_HEADS * HEAD_CHUNKS
D_FF = 4 * D_MODEL
LN_EPS = 1e-5
DEPTH = 1
PAST_LEN = 16384
DN_ALPHA = (2.0 * DEPTH) ** 0.25
ATTN_SCALE = X_HEAD_DIM ** -0.5

assert POOL_GROUP == LANES and X_HEAD_DIM % LANES == 0

CONV_SLABS = D_CONV // LANES
POOL_SLABS = D_POOL // LANES
CONV_PAD = 32
POOL_PAD = 16
GN_HALF = 256

PROMPT_BLOCK = 256
CONV_ROWS = 32
SAMPLE_MIX_BATCH = 32
SAMPLE_ATTN_ROWS = 16
TAIL_FF_CHUNK = 1024
CAST_STEPS = 8
VMEM_LIMIT = 56 * 1024 * 1024


def _unpack(w_u32):
    return pltpu.bitcast(w_u32, BF16)


def _pack(w_bf16):
    return pltpu.bitcast(w_bf16, U32)


def _bdot(a, w):
    return jnp.dot(a.astype(BF16), w, preferred_element_type=F32)


def _layer_norm(x, g, b):
    mu = jnp.mean(x, axis=-1, keepdims=True)
    d = x - mu
    var = jnp.mean(d * d, axis=-1, keepdims=True)
    return d * lax.rsqrt(var + LN_EPS) * g + b


def _group_norm_swish(y, gmat, gn_g, gn_b):
    inv = 1.0 / CONV_HEAD_DIM
    yc = y - _bdot(y, gmat) * inv
    rest = _bdot(yc, gmat) * inv
    var = _bdot(yc * yc, gmat) * inv - rest * rest
    d = yc - rest
    n = d * lax.rsqrt(var + LN_EPS) * gn_g + gn_b
    return n * jax.nn.sigmoid(n)


def _mixer_acts(ybuf, dbuf, mixbuf, gmat_ref, gn_g_ref, gn_b_ref):
    mixbuf[:, D_CONV:] = dbuf[...].astype(BF16)
    m = ybuf.shape[0]
    n_half = D_CONV // GN_HALF
    cols = [slice(j * GN_HALF, (j + 1) * GN_HALF) for j in range(n_half)]
    o = _group_norm_swish(
        jnp.concatenate([ybuf[:, sl] for sl in cols], axis=0), _unpack(gmat_ref[...]),
        jnp.concatenate([jnp.broadcast_to(gn_g_ref[:, sl], (m, GN_HALF)) for sl in cols], axis=0),
        jnp.concatenate([jnp.broadcast_to(gn_b_ref[:, sl], (m, GN_HALF)) for sl in cols], axis=0))
    for j, sl in enumerate(cols):
        mixbuf[:, sl] = o[j * m:(j + 1) * m, :].astype(BF16)


def _mixer_out(mixbuf, x, w_out_ref):
    return DN_ALPHA * x + jnp.dot(mixbuf[...], _unpack(w_out_ref[...]), preferred_element_type=F32)


def _const_spec(shape):
    nd = len(shape)
    return pl.BlockSpec(shape, lambda *_: (0,) * nd, pipeline_mode=pl.Buffered(1))


def _params(n_axes):
    return pltpu.CompilerParams(dimension_semantics=("arbitrary",) * n_axes, vmem_limit_bytes=VMEM_LIMIT)


VEC_FIELDS = (("b1", D_FF), ("b_in", D_IN), ("conv_b", D_CONV), ("gn_g", D_CONV), ("gn_b", D_CONV),
              ("ln1_g", D_MODEL), ("ln1_b", D_MODEL), ("ln2_g", D_MODEL),
              ("ln2_b", D_MODEL), ("b2", D_MODEL), ("ln3_g", D_MODEL), ("ln3_b", D_MODEL))
VEC_LEN = sum(n for _, n in VEC_FIELDS)
WA_FIELDS = (("w_in", D_IN), ("w_out", D_MODEL))
WKV_FIELDS = (("xk", D_MODEL), ("xv", D_MODEL), ("xq", D_MODEL))
WB_FIELDS = (("w1", D_FF), ("xo", D_MODEL))
WKV_BLOCKS = (("xk", 0), ("xv", 1), ("xq", 2))
WB_BLOCKS = (("w1", 0), ("xo", D_FF // D_MODEL))


def _views(ref, fields):
    out, off = {}, 0
    for name, n in fields:
        out[name] = ref.at[:, off:off + n]
        off += n
    return out


def _cast_kernel(pool_w_ref, pool_scale_ref, *refs, widths, n_vecs, fold_index):
    n_mats = sum(len(g) for g in widths)
    srcs, vec_srcs = refs[:n_mats], refs[n_mats:n_mats + n_vecs]
    dsts, vec_dst = refs[n_mats + n_vecs:-1], refs[-1]
    step = pl.program_id(0)
    i = 0
    for dst, group in zip(dsts, widths):
        off = 0
        for n in group:
            w = srcs[i][...]
            if i == fold_index:
                folded = jnp.dot(pool_w_ref[0] * pool_scale_ref[...], w, precision=lax.Precision.HIGHEST,
                                 preferred_element_type=F32)
                w = jnp.where(step >= D_CONV // POOL_GROUP, folded, w)
            dst[:, off:off + n] = _pack(w.astype(BF16))
            off += n
            i += 1
    @pl.when(pl.program_id(0) == 0)
    def _():
        off = 0
        for src in vec_srcs:
            n = src.shape[1]
            vec_dst[:, off:off + n] = src[...]
            off += n


def _cast_call(groups, vectors, pool_w, pool_scale, fold_index):
    n_groups = pool_w.shape[0]
    first = D_CONV // POOL_GROUP
    assert D_MODEL // CAST_STEPS == POOL_GROUP and first + n_groups == CAST_STEPS
    grp = lambda i: jnp.clip(i - first, 0, n_groups - 1)
    in_specs = [pl.BlockSpec((1, POOL_GROUP, POOL_GROUP), lambda i: (grp(i), 0, 0)),
                pl.BlockSpec((1, POOL_GROUP), lambda i: (0, grp(i)))]
    out_specs, out_shape, widths = [], [], []
    for group in groups:
        k = group[0].shape[0]
        rows = k // CAST_STEPS
        assert rows * CAST_STEPS == k and rows % 16 == 0 and all(w.shape[0] == k for w in group)
        assert all(w.shape[1] % LANES == 0 for w in group)
        widths.append(tuple(w.shape[1] for w in group))
        total = sum(widths[-1])
        in_specs += [pl.BlockSpec((rows, w.shape[1]), lambda i: (i, 0)) for w in group]
        out_specs.append(pl.BlockSpec((rows // 2, total), lambda i: (i, 0)))
        out_shape.append(jax.ShapeDtypeStruct((k // 2, total), U32))
    assert all(v.shape[1] % LANES == 0 for v in vectors)
    vec_len = sum(v.shape[1] for v in vectors)
    in_specs += [pl.BlockSpec(v.shape, lambda i: (0, 0)) for v in vectors]
    out_specs.append(pl.BlockSpec((1, vec_len), lambda i: (0, 0)))
    out_shape.append(jax.ShapeDtypeStruct((1, vec_len), F32))
    return pl.pallas_call(
        functools.partial(_cast_kernel, widths=tuple(widths), n_vecs=len(vectors), fold_index=fold_index),
        grid=(CAST_STEPS,), in_specs=in_specs, out_specs=out_specs, out_shape=out_shape,
        compiler_params=_params(1), name="cast_weights",
    )(pool_w, pool_scale, *[w for group in groups for w in group], *vectors)


def _store_head_split(dst_ref, val):
    for hd in range(X_HEADS):
        for c in range(HEAD_CHUNKS):
            col = hd * X_HEAD_DIM + c * LANES
            dst_ref[0, pl.ds(c * X_HEADS + hd, N_MEM, stride=KV_ROW_STRIDE), :] = val[:, col:col + LANES]


def _kv_proj_kernel(mem_ref, wkv_ref, xo_ref, k_ref, v_ref, qk_ref, vo_ref):
    wkv = _views(wkv_ref, WKV_FIELDS)
    m = mem_ref[0].astype(BF16)
    k = jnp.dot(m, _unpack(wkv["xk"][...]), preferred_element_type=F32)
    v = jnp.dot(m, _unpack(wkv["xv"][...]), preferred_element_type=F32)
    _store_head_split(k_ref, k)
    _store_head_split(v_ref, v)
    xq = _unpack(wkv["xq"][...])
    xo = _unpack(xo_ref[...])
    for hd in range(X_HEADS):
        sl = slice(hd * X_HEAD_DIM, (hd + 1) * X_HEAD_DIM)
        qk = jnp.dot(xq[:, sl], k[:, sl].T.astype(BF16), preferred_element_type=F32) * ATTN_SCALE
        qk_ref[0, :, hd * N_MEM:(hd + 1) * N_MEM] = _pack(qk.astype(BF16))
        vo = jnp.dot(v[:, sl].astype(BF16), xo[sl, :], preferred_element_type=F32)
        vo_ref[0, hd * N_MEM // 2:(hd + 1) * N_MEM // 2, :] = _pack(vo.astype(BF16))


def _kv_proj_call(mem, wkv, wb):
    nb = mem.shape[0]
    blk = lambda s: pl.BlockSpec((1,) + s, lambda b: (b, 0, 0))
    xo_block = dict(WB_BLOCKS)["xo"]
    return pl.pallas_call(
        _kv_proj_kernel,
        grid=(nb,),
        in_specs=[blk((N_MEM, D_MODEL)), _const_spec(wkv.shape),
                  pl.BlockSpec((D_MODEL // 2, D_MODEL), lambda b: (0, xo_block), pipeline_mode=pl.Buffered(1))],
        out_specs=[blk((KV_ROWS, LANES)), blk((KV_ROWS, LANES)),
                   blk((D_MODEL // 2, X_HEADS * N_MEM)), blk((X_HEADS * N_MEM // 2, D_MODEL))],
        out_shape=[
            jax.ShapeDtypeStruct((nb, KV_ROWS, LANES), F32),
            jax.ShapeDtypeStruct((nb, KV_ROWS, LANES), F32),
            jax.ShapeDtypeStruct((nb, D_MODEL // 2, X_HEADS * N_MEM), U32),
            jax.ShapeDtypeStruct((nb, X_HEADS * N_MEM // 2, D_MODEL), U32),
        ],
        compiler_params=_params(1),
        name="kv_proj",
    )(mem, wkv, wb)


def _prompt_kernel(x_ref, qk_ref, vo_ref, qs_ref, kc_ref, vc_ref,
                   wa_ref, w1_ref, w2_ref, gmat_ref, conv_w_ref, vec_ref,
                   y_ref, conv_new_ref, pool_new_ref, os_ref,
                   ubuf, pbuf, ybuf, dbuf, mixbuf, obuf, x1buf, x1prev, x2buf, hbuf, *, nt):
    wa, vec = _views(wa_ref, WA_FIELDS), _views(vec_ref, VEC_FIELDS)
    w_in_ref, w_out_ref = wa["w_in"], wa["w_out"]
    b_in_ref, conv_b_ref, gn_g_ref, gn_b_ref = vec["b_in"], vec["conv_b"], vec["gn_g"], vec["gn_b"]
    ln1_g_ref, ln1_b_ref = vec["ln1_g"], vec["ln1_b"]
    ln2_g_ref, ln2_b_ref, b1_ref, b2_ref = vec["ln2_g"], vec["ln2_b"], vec["b1"], vec["b2"]
    ln3_g_ref, ln3_b_ref = vec["ln3_g"], vec["ln3_b"]
    tm = PROMPT_BLOCK
    step = pl.program_id(0)
    n_blocks = pl.num_programs(0) - 1
    t = jnp.minimum(step, n_blocks - 1) % nt

    @pl.when(t == 0)
    def _():
        ubuf[:, 0:CONV_PAD, :] = jnp.zeros((CONV_SLABS, CONV_PAD, LANES), F32)
        pbuf[:, 0:POOL_PAD, :] = jnp.zeros((POOL_SLABS, POOL_PAD, LANES), F32)

    @pl.when(t > 0)
    def _():
        ubuf[:, 0:CONV_PAD, :] = ubuf[:, tm:tm + CONV_PAD, :]
        pbuf[:, 0:POOL_PAD, :] = pbuf[:, tm:tm + POOL_PAD, :]

    val = {}

    def mix_in():
        val["x"] = x_ref[0]
        h = _bdot(val["x"], _unpack(w_in_ref[...])) + b_in_ref[...]
        u = h[:, 0:D_CONV] * jax.nn.sigmoid(h[:, D_CONV:2 * D_CONV])
        for s in range(CONV_SLABS):
            ubuf[s, CONV_PAD:CONV_PAD + tm, :] = u[:, s * LANES:(s + 1) * LANES]
        for s in range(POOL_SLABS):
            pbuf[s, POOL_PAD:POOL_PAD + tm, :] = h[:, 2 * D_CONV + s * LANES:2 * D_CONV + (s + 1) * LANES]

    def att_scores():
        x1prev[...] = _layer_norm(x1buf[...], ln1_g_ref[...], ln1_b_ref[...])
        val["scores"] = _bdot(x1prev[...], _unpack(qk_ref[0]))

    def dec_probs():
        val["probs_dec"] = _sample_attn_probs(qs_ref, kc_ref)

    def att_values():
        for hd in range(X_HEADS):
            cols = slice(hd * N_MEM, (hd + 1) * N_MEM)
            s = val["scores"][:, cols]
            e = jnp.exp(s - jnp.max(s, axis=-1, keepdims=True))
            obuf[:, cols] = (e / jnp.sum(e, axis=-1, keepdims=True)).astype(BF16)

    def mix_windows():
        base = CONV_PAD - CONV_CTX
        for s in range(CONV_SLABS):
            sl = slice(s * LANES, (s + 1) * LANES)
            for c in range(tm // CONV_ROWS):
                r0 = c * CONV_ROWS
                acc = jnp.broadcast_to(conv_b_ref[:, sl], (CONV_ROWS, LANES))
                for k in range(CONV_WIDTH):
                    acc = acc + ubuf[s, base + r0 + k:base + r0 + k + CONV_ROWS, :] * conv_w_ref[k, :, sl]
                ybuf[r0:r0 + CONV_ROWS, sl] = acc

        pos = t * tm + lax.broadcasted_iota(jnp.int32, (tm, POOL_GROUP), 0)
        for gi, w in enumerate(POOL_WINDOWS):
            cur = pbuf[gi, POOL_PAD:POOL_PAD + tm, :]
            ws = cur
            for j in range(1, w):
                ws = ws + pbuf[gi, POOL_PAD - j:POOL_PAD - j + tm, :]
            cnt = jnp.minimum(pos + 1, w).astype(F32)
            dbuf[:, gi * POOL_GROUP:(gi + 1) * POOL_GROUP] = ws / cnt - cur

    def att_out():
        val["attn"] = jnp.dot(obuf[...], _unpack(vo_ref[0]), preferred_element_type=F32)

    def dec_values():
        _sample_attn_values(val["probs_dec"], vc_ref, os_ref, os_ref.shape[0] // vc_ref.shape[0])

    def ffn_norm():
        x2buf[...] = _layer_norm(DN_ALPHA * x1prev[...] + val["attn"], ln2_g_ref[...], ln2_b_ref[...])

    def ffn_up():
        hdn = jnp.maximum(_bdot(x2buf[...], _unpack(w1_ref[...])) + b1_ref[...], 0.0)
        hbuf[...] = (hdn * hdn).astype(BF16)

    def mix_acts():
        _mixer_acts(ybuf, dbuf, mixbuf, gmat_ref, gn_g_ref, gn_b_ref)

    def ffn_down():
        val["f"] = jnp.dot(hbuf[...], _unpack(w2_ref[...]), preferred_element_type=F32) + b2_ref[...]

    def mix_out():
        x1buf[...] = _mixer_out(mixbuf, val["x"], w_out_ref)

    def out_norm():
        y_ref[0] = _layer_norm(DN_ALPHA * x2buf[...] + val["f"], ln3_g_ref[...], ln3_b_ref[...])

    @pl.when(step == 0)
    def _():
        x1buf[...] = jnp.zeros((tm, D_MODEL), F32)

    for piece in (mix_in, att_scores, dec_probs, att_values, mix_windows, att_out, dec_values, ffn_norm, ffn_up,
                  mix_acts, ffn_down, mix_out, out_norm):
        piece()

    @pl.when(jnp.logical_and(t == nt - 1, step < n_blocks))
    def _():
        for s in range(CONV_SLABS):
            conv_new_ref[0, :, s * LANES:(s + 1) * LANES] = ubuf[s, CONV_PAD + tm - CONV_CTX:CONV_PAD + tm, :]
        for s in range(POOL_SLABS):
            pool_new_ref[0, :, s * LANES:(s + 1) * LANES] = pbuf[s, POOL_PAD + tm - POOL_CTX:POOL_PAD + tm, :]


def _prompt_call(x, qk, vo, q_dec, k_dec, v_dec, dec_seq, wa, wb, consts):
    assert dict(WB_BLOCKS)["w1"] == 0
    nb, seq, _ = x.shape
    tm = PROMPT_BLOCK
    nt = seq // tm
    n_blocks = nb * nt
    n_dec = k_dec.shape[0]
    dec_per_step = -(-n_dec // n_blocks)
    dec_steps = n_dec // dec_per_step
    dec_rows = dec_per_step * dec_seq
    assert dec_steps * dec_per_step == n_dec and dec_steps <= n_blocks + 1
    assert dec_rows % SUBLANES == 0 and SAMPLE_ATTN_ROWS % dec_rows == 0
    cur = lambda g: jnp.minimum(g, n_blocks - 1)
    prev = lambda g: jnp.maximum(g - 1, 0)
    dec = lambda g: jnp.minimum(g, dec_steps - 1)
    in_specs = [
        pl.BlockSpec((1, tm, D_MODEL), lambda g: (cur(g) // nt, cur(g) % nt, 0)),
        pl.BlockSpec((1,) + qk.shape[1:], lambda g: (prev(g) // nt, 0, 0)),
        pl.BlockSpec((1,) + vo.shape[1:], lambda g: (prev(g) // nt, 0, 0)),
        pl.BlockSpec((dec_rows, D_MODEL), lambda g: (dec(g), 0)),
        pl.BlockSpec((dec_per_step, KV_ROWS, LANES), lambda g: (dec(g), 0, 0)),
        pl.BlockSpec((dec_per_step, KV_ROWS, LANES), lambda g: (dec(g), 0, 0)),
        _const_spec(wa.shape),
        pl.BlockSpec((D_MODEL // 2, D_FF), lambda g: (0, 0), pipeline_mode=pl.Buffered(1)),
    ] + [_const_spec(w.shape) for w in consts]
    out_specs = [
        pl.BlockSpec((1, tm, D_MODEL), lambda g: (prev(g) // nt, prev(g) % nt, 0)),
        pl.BlockSpec((1, CONV_CTX, D_CONV), lambda g: (cur(g) // nt, 0, 0)),
        pl.BlockSpec((1, POOL_CTX, D_POOL), lambda g: (cur(g) // nt, 0, 0)),
        pl.BlockSpec((dec_rows, D_MODEL), lambda g: (dec(g), 0)),
    ]
    out_shape = [
        jax.ShapeDtypeStruct((nb, seq, D_MODEL), F32),
        jax.ShapeDtypeStruct((nb, CONV_CTX, D_CONV), F32),
        jax.ShapeDtypeStruct((nb, POOL_CTX, D_POOL), F32),
        jax.ShapeDtypeStruct(q_dec.shape, F32),
    ]
    scratch = [
        pltpu.VMEM((CONV_SLABS, CONV_PAD + tm, LANES), F32),
        pltpu.VMEM((POOL_SLABS, POOL_PAD + tm, LANES), F32),
        pltpu.VMEM((tm, D_CONV), F32),
        pltpu.VMEM((tm, D_POOL), F32),
        pltpu.VMEM((tm, D_MODEL), BF16),
        pltpu.VMEM((tm, D_MODEL), BF16),
        pltpu.VMEM((tm, D_MODEL), F32),
        pltpu.VMEM((tm, D_MODEL), F32),
        pltpu.VMEM((tm, D_MODEL), F32),
        pltpu.VMEM((tm, D_FF), BF16),
    ]
    return pl.pallas_call(
        functools.partial(_prompt_kernel, nt=nt),
        grid=(n_blocks + 1,),
        in_specs=in_specs,
        out_specs=out_specs,
        out_shape=out_shape,
        scratch_shapes=scratch,
        compiler_params=_params(1),
        name="prompt_layer",
    )(x, qk, vo, q_dec, k_dec, v_dec, wa, wb, *consts)


def _sample_mixer_kernel(x_ref, sconv_ref, spool_ref,
                         wa_ref, xq_ref, gmat_ref, conv_w_ref, vec_ref,
                         x1_ref, q_ref, conv_new_ref, pool_new_ref,
                         hbuf, yslab, dslab, ybuf, dbuf, mixbuf, *, dec_seq):
    wa, vec = _views(wa_ref, WA_FIELDS), _views(vec_ref, VEC_FIELDS)
    w_in_ref, w_out_ref = wa["w_in"], wa["w_out"]
    b_in_ref, conv_b_ref, gn_g_ref, gn_b_ref = vec["b_in"], vec["conv_b"], vec["gn_g"], vec["gn_b"]
    ln1_g_ref, ln1_b_ref = vec["ln1_g"], vec["ln1_b"]
    gb = SAMPLE_MIX_BATCH
    ts = dec_seq
    x = x_ref[...]
    h = _bdot(x, _unpack(w_in_ref[...])) + b_in_ref[...]
    u = h[:, 0:D_CONV] * jax.nn.sigmoid(h[:, D_CONV:2 * D_CONV])
    for s in range(CONV_SLABS):
        hbuf[s] = u[:, s * LANES:(s + 1) * LANES]
    for s in range(POOL_SLABS):
        hbuf[CONV_SLABS + s] = h[:, 2 * D_CONV + s * LANES:2 * D_CONV + (s + 1) * LANES]

    def step_rows(slab, t):
        return hbuf[slab, pl.ds(t, gb, stride=ts), :]

    for s in range(CONV_SLABS):
        sl = slice(s * LANES, (s + 1) * LANES)
        new = [step_rows(s, t) for t in range(ts)]
        ext = lambda j: sconv_ref[j, :, sl] if j < CONV_CTX else new[j - CONV_CTX]
        for j in range(CONV_CTX - ts):
            conv_new_ref[j, :, sl] = sconv_ref[j + ts, :, sl]
        for t in range(ts):
            conv_new_ref[CONV_CTX - ts + t, :, sl] = new[t]
            acc = jnp.broadcast_to(conv_b_ref[:, sl], (gb, LANES))
            for k in range(CONV_WIDTH):
                acc = acc + ext(t + k) * conv_w_ref[k, :, sl]
            yslab[s, pl.ds(t, gb, stride=ts), :] = acc

    for g, w in enumerate(POOL_WINDOWS):
        sl = slice(g * LANES, (g + 1) * LANES)
        new = [step_rows(CONV_SLABS + g, t) for t in range(ts)]
        ext = lambda j: spool_ref[j, :, sl] if j < POOL_CTX else new[j - POOL_CTX]
        for j in range(POOL_CTX - ts):
            pool_new_ref[j, :, sl] = spool_ref[j + ts, :, sl]
        for t in range(ts):
            pool_new_ref[POOL_CTX - ts + t, :, sl] = new[t]
            ws = new[t]
            for j in range(1, w):
                ws = ws + ext(POOL_CTX + t - j)
            cnt = float(min(PAST_LEN + t + 1, w))
            dslab[g, pl.ds(t, gb, stride=ts), :] = ws / cnt - new[t]

    for s in range(CONV_SLABS):
        ybuf[:, s * LANES:(s + 1) * LANES] = yslab[s]
    for s in range(POOL_SLABS):
        dbuf[:, s * LANES:(s + 1) * LANES] = dslab[s]

    _mixer_acts(ybuf, dbuf, mixbuf, gmat_ref, gn_g_ref, gn_b_ref)
    x1 = _layer_norm(_mixer_out(mixbuf, x, w_out_ref), ln1_g_ref[...], ln1_b_ref[...])
    x1_ref[...] = x1
    q_ref[...] = _bdot(x1, _unpack(xq_ref[...])) * ATTN_SCALE


def _sample_mixer_call(xs, sconv_tm, spool_tm, wa, wkv, consts, dec_seq):
    xq_block = dict(WKV_BLOCKS)["xq"]
    n_tok = xs.shape[0]
    nb = n_tok // dec_seq
    gb = SAMPLE_MIX_BATCH
    rows = gb * dec_seq
    row_spec = pl.BlockSpec((rows, D_MODEL), lambda i: (i, 0))
    conv_spec = pl.BlockSpec((CONV_CTX, gb, D_CONV), lambda i: (0, i, 0))
    pool_spec = pl.BlockSpec((POOL_CTX, gb, D_POOL), lambda i: (0, i, 0))
    out_shape = [
        jax.ShapeDtypeStruct((n_tok, D_MODEL), F32),
        jax.ShapeDtypeStruct((n_tok, D_MODEL), F32),
        jax.ShapeDtypeStruct((CONV_CTX, nb, D_CONV), F32),
        jax.ShapeDtypeStruct((POOL_CTX, nb, D_POOL), F32),
    ]
    scratch = [
        pltpu.VMEM((CONV_SLABS + POOL_SLABS, rows, LANES), F32),
        pltpu.VMEM((CONV_SLABS, rows, LANES), F32),
        pltpu.VMEM((POOL_SLABS, rows, LANES), F32),
        pltpu.VMEM((rows, D_CONV), F32),
        pltpu.VMEM((rows, D_POOL), F32),
        pltpu.VMEM((rows, D_MODEL), BF16),
    ]
    return pl.pallas_call(
        functools.partial(_sample_mixer_kernel, dec_seq=dec_seq),
        grid=(nb // gb,),
        in_specs=[row_spec, conv_spec, pool_spec, _const_spec(wa.shape),
                  pl.BlockSpec((D_MODEL // 2, D_MODEL), lambda i: (0, xq_block), pipeline_mode=pl.Buffered(1))]
        + [_const_spec(w.shape) for w in consts],
        out_specs=[row_spec, row_spec, conv_spec, pool_spec],
        out_shape=out_shape,
        scratch_shapes=scratch,
        compiler_params=_params(1),
        name="sample_mixer",
    )(xs, sconv_tm, spool_tm, wa, wkv, *consts)


def _load_head(ref, b, hd):
    parts = [ref[b, pl.ds(c * X_HEADS + hd, N_MEM, stride=KV_ROW_STRIDE), :] for c in range(HEAD_CHUNKS)]
    return jnp.concatenate(parts, axis=1).astype(BF16)


def _sample_attn_probs(q_ref, k_ref):
    q8 = q_ref[...]
    q = jnp.concatenate([q8] * (SAMPLE_ATTN_ROWS // q8.shape[0]), axis=0).astype(BF16)
    s = jnp.concatenate(
        [lax.dot_general(q[:, hd * X_HEAD_DIM:(hd + 1) * X_HEAD_DIM], _load_head(k_ref, b, hd),
                         (((1,), (1,)), ((), ())), preferred_element_type=F32)
         for b in range(k_ref.shape[0]) for hd in range(X_HEADS)], axis=0)
    e = jnp.exp(s - jnp.max(s, axis=-1, keepdims=True))
    return (e / jnp.sum(e, axis=-1, keepdims=True)).astype(BF16)


def _sample_attn_values(a, v_ref, o_ref, ts):
    for b in range(v_ref.shape[0]):
        for hd in range(X_HEADS):
            i = b * X_HEADS + hd
            o = jnp.dot(a[i * SAMPLE_ATTN_ROWS:(i + 1) * SAMPLE_ATTN_ROWS, :], _load_head(v_ref, b, hd),
                        preferred_element_type=F32)
            o_ref[b * ts:(b + 1) * ts, hd * X_HEAD_DIM:(hd + 1) * X_HEAD_DIM] = o[b * ts:(b + 1) * ts, :]


def _sample_tail_kernel(x1_ref, o_ref, xo_ref, w1_ref, b1_ref, w2_ref, vec_ref, y_ref, x2buf, x2bf, fbuf):
    vec = _views(vec_ref, VEC_FIELDS)
    ln2_g_ref, ln2_b_ref, b2_ref, ln3_g_ref, ln3_b_ref = (
        vec["ln2_g"], vec["ln2_b"], vec["b2"], vec["ln3_g"], vec["ln3_b"])
    c = pl.program_id(0)

    @pl.when(c == 0)
    def _():
        attn = _bdot(o_ref[...], _unpack(xo_ref[...]))
        x2 = _layer_norm(DN_ALPHA * x1_ref[...] + attn, ln2_g_ref[...], ln2_b_ref[...])
        x2buf[...] = x2
        x2bf[...] = x2.astype(BF16)
        fbuf[...] = jnp.broadcast_to(b2_ref[...], fbuf.shape)

    hdn = jnp.maximum(jnp.dot(x2bf[...], _unpack(w1_ref[...]), preferred_element_type=F32) + b1_ref[...], 0.0)
    fbuf[...] += _bdot(hdn * hdn, _unpack(w2_ref[...]))

    @pl.when(c == pl.num_programs(0) - 1)
    def _():
        y_ref[...] = _layer_norm(DN_ALPHA * x2buf[...] + fbuf[...], ln3_g_ref[...], ln3_b_ref[...])


def _sample_tail_call(x1, o, wb, w2_p, vecs):
    n_tok = x1.shape[0]
    fc = TAIL_FF_CHUNK
    blocks = dict(WB_BLOCKS)
    assert fc == D_MODEL and VEC_FIELDS[0] == ("b1", D_FF) and blocks["w1"] == 0
    in_specs = [
        _const_spec(x1.shape), _const_spec(o.shape),
        pl.BlockSpec((D_MODEL // 2, D_MODEL), lambda c: (0, blocks["xo"]), pipeline_mode=pl.Buffered(1)),
        pl.BlockSpec((D_MODEL // 2, fc), lambda c: (0, c)),
        pl.BlockSpec((1, fc), lambda c: (0, c)),
        pl.BlockSpec((fc // 2, D_MODEL), lambda c: (c, 0)),
        _const_spec(vecs.shape),
    ]
    return pl.pallas_call(
        _sample_tail_kernel,
        grid=(D_FF // fc,),
        in_specs=in_specs,
        out_specs=pl.BlockSpec((n_tok, D_MODEL), lambda c: (0, 0)),
        out_shape=jax.ShapeDtypeStruct((n_tok, D_MODEL), F32),
        scratch_shapes=[pltpu.VMEM((n_tok, D_MODEL), F32), pltpu.VMEM((n_tok, D_MODEL), BF16),
                        pltpu.VMEM((n_tok, D_MODEL), F32)],
        compiler_params=_params(1),
        name="sample_tail",
    )(x1, o, wb, wb, vecs, w2_p, vecs)


def _group_indicator():
    gid = jnp.arange(GN_HALF) // CONV_HEAD_DIM
    return (gid[:, None] == gid[None, :]).astype(F32)


def _to_head_split_view(a):
    lead = a.shape[:-3]
    n = len(lead)
    a = a.reshape(lead + (N_MEM, X_HEADS, HEAD_CHUNKS, LANES))
    a = a.transpose(tuple(range(n)) + (n, n + 2, n + 1, n + 3))
    return a.reshape(lead + (KV_ROWS, LANES))


def _from_head_split_view(a):
    lead = a.shape[:-2]
    n = len(lead)
    a = a.reshape(lead + (N_MEM, HEAD_CHUNKS, X_HEADS, LANES))
    a = a.transpose(tuple(range(n)) + (n, n + 2, n + 1, n + 3))
    return a.reshape(lead + (N_MEM, X_HEADS, X_HEAD_DIM))


def kernel(x_prompt, x_sample, mem_prompt, cache_mem_k, cache_mem_v, state_conv, state_pool, w_in, b_in, conv_w, conv_b, gn_g, gn_b, pool_w, pool_scale, w_out, ln1_g, ln1_b, xq_w, xk_w, xv_w, xo_w, ln2_g, ln2_b, w1, b1, w2, b2, ln3_g, ln3_b):
    assert w_in.shape[0] == DEPTH == 1
    n_prompt, seq, _ = x_prompt.shape
    n_dec, dec_seq, _ = x_sample.shape
    assert seq % PROMPT_BLOCK == 0 and n_dec % SAMPLE_MIX_BATCH == 0

    named = dict(w_in=w_in, w_out=w_out, xq=xq_w, xo=xo_w, w1=w1, b1=b1, b_in=b_in, conv_b=conv_b, gn_g=gn_g,
                 gn_b=gn_b, ln1_g=ln1_g, ln1_b=ln1_b, ln2_g=ln2_g, ln2_b=ln2_b, b2=b2,
                 ln3_g=ln3_g, ln3_b=ln3_b)
    named.update(xk=xk_w, xv=xv_w)
    wa, wb, wkv, gmat_p, w2_p, vecs = _cast_call(
        [[named[name][0] for name, _ in WA_FIELDS], [named[name][0] for name, _ in WB_FIELDS],
         [named[name][0] for name, _ in WKV_FIELDS], [_group_indicator()], [w2[0]]],
        [named[name] for name, _ in VEC_FIELDS],
        pool_w[0], pool_scale, fold_index=[name for name, _ in WA_FIELDS].index("w_out"))
    conv_taps = conv_w.transpose(1, 0, 2)

    xs = x_sample.reshape(n_dec * dec_seq, D_MODEL)
    x1_s, q_s, conv_s, pool_s = _sample_mixer_call(
        xs, state_conv[0].transpose(1, 0, 2), state_pool[0].transpose(1, 0, 2),
        wa, wkv, (gmat_p, conv_taps, vecs), dec_seq)

    k_p, v_p, qk_p, vo_p = _kv_proj_call(mem_prompt, wkv, wb)
    y_p, conv_p, pool_p, o_s = _prompt_call(
        x_prompt, qk_p, vo_p, q_s, _to_head_split_view(cache_mem_k[0]), _to_head_split_view(cache_mem_v[0]),
        dec_seq, wa, wb, (w2_p, gmat_p, conv_taps, vecs))

    y_s = _sample_tail_call(x1_s, o_s, wb, w2_p, vecs)

    return (y_p, y_s.reshape(n_dec, dec_seq, D_MODEL),
            _from_head_split_view(k_p)[None], _from_head_split_view(v_p)[None],
            conv_p[None], conv_s.transpose(1, 0, 2)[None], pool_p[None], pool_s.transpose(1, 0, 2)[None])
```

```python
import functools

import jax
import jax.numpy as jnp
from jax import lax
from jax.experimental import pallas as pl
from jax.experimental.pallas import tpu as pltpu

F32 = jnp.float32
BF16 = jnp.bfloat16
U32 = jnp.uint32

LANES = 128
SUBLANES = 8
D_MODEL = 1024
D_CONV = 512
D_POOL = 512
D_IN = 2 * D_CONV + D_POOL
CONV_WIDTH = 31
CONV_CTX = CONV_WIDTH - 1
CONV_HEADS = 8
CONV_HEAD_DIM = D_CONV // CONV_HEADS
POOL_WINDOWS = (2, 4, 8, 16)
POOL_GROUP = D_POOL // len(POOL_WINDOWS)
POOL_CTX = max(POOL_WINDOWS) - 1
N_MEM = 256
X_HEADS = 4
X_HEAD_DIM = D_MODEL // X_HEADS
HEAD_CHUNKS = X_HEAD_DIM // LANES
KV_ROWS = N_MEM * X_HEADS * HEAD_CHUNKS
KV_ROW_STRIDE = X_HEADS * HEAD_CHUNKS
D_FF = 4 * D_MODEL
LN_EPS = 1e-5
DEPTH = 1
PAST_LEN = 16384
DN_ALPHA = (2.0 * DEPTH) ** 0.25
ATTN_SCALE = X_HEAD_DIM ** -0.5

assert POOL_GROUP == LANES and X_HEAD_DIM % LANES == 0

CONV_SLABS = D_CONV // LANES
POOL_SLABS = D_POOL // LANES
CONV_PAD = 32
POOL_PAD = 16
GN_HALF = 256

PROMPT_BLOCK = 256
CONV_ROWS = 32
SAMPLE_MIX_BATCH = 32
SAMPLE_ATTN_ROWS = 16
TAIL_FF_CHUNK = 1024
CAST_STEPS = 8
VMEM_LIMIT = 56 * 1024 * 1024


def _unpack(w_u32):
    return pltpu.bitcast(w_u32, BF16)


def _pack(w_bf16):
    return pltpu.bitcast(w_bf16, U32)


def _bdot(a, w):
    return jnp.dot(a.astype(BF16), w, preferred_element_type=F32)


def _layer_norm(x, g, b):
    mu = jnp.mean(x, axis=-1, keepdims=True)
    d = x - mu
    var = jnp.mean(d * d, axis=-1, keepdims=True)
    return d * lax.rsqrt(var + LN_EPS) * g + b


def _group_norm_swish(y, gmat, gn_g, gn_b):
    inv = 1.0 / CONV_HEAD_DIM
    yc = y - _bdot(y, gmat) * inv
    rest = _bdot(yc, gmat) * inv
    var = _bdot(yc * yc, gmat) * inv - rest * rest
    d = yc - rest
    n = d * lax.rsqrt(var + LN_EPS) * gn_g + gn_b
    return n * jax.nn.sigmoid(n)


def _mixer_acts(ybuf, dbuf, mixbuf, gmat_ref, gn_g_ref, gn_b_ref):
    mixbuf[:, D_CONV:] = dbuf[...].astype(BF16)
    m = ybuf.shape[0]
    n_half = D_CONV // GN_HALF
    cols = [slice(j * GN_HALF, (j + 1) * GN_HALF) for j in range(n_half)]
    o = _group_norm_swish(
        jnp.concatenate([ybuf[:, sl] for sl in cols], axis=0), _unpack(gmat_ref[...]),
        jnp.concatenate([jnp.broadcast_to(gn_g_ref[:, sl], (m, GN_HALF)) for sl in cols], axis=0),
        jnp.concatenate([jnp.broadcast_to(gn_b_ref[:, sl], (m, GN_HALF)) for sl in cols], axis=0))
    for j, sl in enumerate(cols):
        mixbuf[:, sl] = o[j * m:(j + 1) * m, :].astype(BF16)


def _mixer_out(mixbuf, x, w_out_ref):
    return DN_ALPHA * x + jnp.dot(mixbuf[...], _unpack(w_out_ref[...]), preferred_element_type=F32)


def _const_spec(shape):
    nd = len(shape)
    return pl.BlockSpec(shape, lambda *_: (0,) * nd, pipeline_mode=pl.Buffered(1))


def _params(n_axes):
    return pltpu.CompilerParams(dimension_semantics=("arbitrary",) * n_axes, vmem_limit_bytes=VMEM_LIMIT)


VEC_FIELDS = (("b1", D_FF), ("b_in", D_IN), ("conv_b", D_CONV), ("gn_g", D_CONV), ("gn_b", D_CONV),
              ("ln1_g", D_MODEL), ("ln1_b", D_MODEL), ("ln2_g", D_MODEL),
              ("ln2_b", D_MODEL), ("b2", D_MODEL), ("ln3_g", D_MODEL), ("ln3_b", D_MODEL))
VEC_LEN = sum(n for _, n in VEC_FIELDS)
WA_FIELDS = (("w_in", D_IN), ("w_out", D_MODEL))
WKV_FIELDS = (("xk", D_MODEL), ("xv", D_MODEL), ("xq", D_MODEL))
WB_FIELDS = (("w1", D_FF), ("xo", D_MODEL))
WKV_BLOCKS = (("xk", 0), ("xv", 1), ("xq", 2))
WB_BLOCKS = (("w1", 0), ("xo", D_FF // D_MODEL))


def _views(ref, fields):
    out, off = {}, 0
    for name, n in fields:
        out[name] = ref.at[:, off:off + n]
        off += n
    return out


def _cast_kernel(pool_w_ref, pool_scale_ref, *refs, widths, n_vecs, fold_index):
    n_mats = sum(len(g) for g in widths)
    srcs, vec_srcs = refs[:n_mats], refs[n_mats:n_mats + n_vecs]
    dsts, vec_dst = refs[n_mats + n_vecs:-1], refs[-1]
    step = pl.program_id(0)
    i = 0
    for dst, group in zip(dsts, widths):
        off = 0
        for n in group:
            w = srcs[i][...]
            if i == fold_index:
                folded = jnp.dot(pool_w_ref[0] * pool_scale_ref[...], w, precision=lax.Precision.HIGHEST,
                                 preferred_element_type=F32)
                w = jnp.where(step >= D_CONV // POOL_GROUP, folded, w)
            dst[:, off:off + n] = _pack(w.astype(BF16))
            off += n
            i += 1
    @pl.when(pl.program_id(0) == 0)
    def _():
        off = 0
        for src in vec_srcs:
            n = src.shape[1]
            vec_dst[:, off:off + n] = src[...]
            off += n


def _cast_call(groups, vectors, pool_w, pool_scale, fold_index):
    n_groups = pool_w.shape[0]
    first = D_CONV // POOL_GROUP
    assert D_MODEL // CAST_STEPS == POOL_GROUP and first + n_groups == CAST_STEPS
    grp = lambda i: jnp.clip(i - first, 0, n_groups - 1)
    in_specs = [pl.BlockSpec((1, POOL_GROUP, POOL_GROUP), lambda i: (grp(i), 0, 0)),
                pl.BlockSpec((1, POOL_GROUP), lambda i: (0, grp(i)))]
    out_specs, out_shape, widths = [], [], []
    for group in groups:
        k = group[0].shape[0]
        rows = k // CAST_STEPS
        assert rows * CAST_STEPS == k and rows % 16 == 0 and all(w.shape[0] == k for w in group)
        assert all(w.shape[1] % LANES == 0 for w in group)
        widths.append(tuple(w.shape[1] for w in group))
        total = sum(widths[-1])
        in_specs += [pl.BlockSpec((rows, w.shape[1]), lambda i: (i, 0)) for w in group]
        out_specs.append(pl.BlockSpec((rows // 2, total), lambda i: (i, 0)))
        out_shape.append(jax.ShapeDtypeStruct((k // 2, total), U32))
    assert all(v.shape[1] % LANES == 0 for v in vectors)
    vec_len = sum(v.shape[1] for v in vectors)
    in_specs += [pl.BlockSpec(v.shape, lambda i: (0, 0)) for v in vectors]
    out_specs.append(pl.BlockSpec((1, vec_len), lambda i: (0, 0)))
    out_shape.append(jax.ShapeDtypeStruct((1, vec_len), F32))
    return pl.pallas_call(
        functools.partial(_cast_kernel, widths=tuple(widths), n_vecs=len(vectors), fold_index=fold_index),
        grid=(CAST_STEPS,), in_specs=in_specs, out_specs=out_specs, out_shape=out_shape,
        compiler_params=_params(1), name="cast_weights",
    )(pool_w, pool_scale, *[w for group in groups for w in group], *vectors)


def _store_head_split(dst_ref, val):
    for hd in range(X_HEADS):
        for c in range(HEAD_CHUNKS):
            col = hd * X_HEAD_DIM + c * LANES
            dst_ref[0, pl.ds(c * X_HEADS + hd, N_MEM, stride=KV_ROW_STRIDE), :] = val[:, col:col + LANES]


def _kv_proj_kernel(mem_ref, wkv_ref, xo_ref, k_ref, v_ref, qk_ref, vo_ref):
    wkv = _views(wkv_ref, WKV_FIELDS)
    m = mem_ref[0].astype(BF16)
    k = jnp.dot(m, _unpack(wkv["xk"][...]), preferred_element_type=F32)
    v = jnp.dot(m, _unpack(wkv["xv"][...]), preferred_element_type=F32)
    _store_head_split(k_ref, k)
    _store_head_split(v_ref, v)
    xq = _unpack(wkv["xq"][...])
    xo = _unpack(xo_ref[...])
    for hd in range(X_HEADS):
        sl = slice(hd * X_HEAD_DIM, (hd + 1) * X_HEAD_DIM)
        qk = jnp.dot(xq[:, sl], k[:, sl].T.astype(BF16), preferred_element_type=F32) * ATTN_SCALE
        qk_ref[0, :, hd * N_MEM:(hd + 1) * N_MEM] = _pack(qk.astype(BF16))
        vo = jnp.dot(v[:, sl].astype(BF16), xo[sl, :], preferred_element_type=F32)
        vo_ref[0, hd * N_MEM // 2:(hd + 1) * N_MEM // 2, :] = _pack(vo.astype(BF16))


def _kv_proj_call(mem, wkv, wb):
    nb = mem.shape[0]
    blk = lambda s: pl.BlockSpec((1,) + s, lambda b: (b, 0, 0))
    xo_block = dict(WB_BLOCKS)["xo"]
    return pl.pallas_call(
        _kv_proj_kernel,
        grid=(nb,),
        in_specs=[blk((N_MEM, D_MODEL)), _const_spec(wkv.shape),
                  pl.BlockSpec((D_MODEL // 2, D_MODEL), lambda b: (0, xo_block), pipeline_mode=pl.Buffered(1))],
        out_specs=[blk((KV_ROWS, LANES)), blk((KV_ROWS, LANES)),
                   blk((D_MODEL // 2, X_HEADS * N_MEM)), blk((X_HEADS * N_MEM // 2, D_MODEL))],
        out_shape=[
            jax.ShapeDtypeStruct((nb, KV_ROWS, LANES), F32),
            jax.ShapeDtypeStruct((nb, KV_ROWS, LANES), F32),
            jax.ShapeDtypeStruct((nb, D_MODEL // 2, X_HEADS * N_MEM), U32),
            jax.ShapeDtypeStruct((nb, X_HEADS * N_MEM // 2, D_MODEL), U32),
        ],
        compiler_params=_params(1),
        name="kv_proj",
    )(mem, wkv, wb)


def _prompt_kernel(x_ref, qk_ref, vo_ref, qs_ref, kc_ref, vc_ref,
                   wa_ref, w1_ref, w2_ref, gmat_ref, conv_w_ref, vec_ref,
                   y_ref, conv_new_ref, pool_new_ref, os_ref,
                   ubuf, pbuf, ybuf, dbuf, mixbuf, obuf, x1buf, x1prev, x2buf, hbuf, *, nt):
    wa, vec = _views(wa_ref, WA_FIELDS), _views(vec_ref, VEC_FIELDS)
    w_in_ref, w_out_ref = wa["w_in"], wa["w_out"]
    b_in_ref, conv_b_ref, gn_g_ref, gn_b_ref = vec["b_in"], vec["conv_b"], vec["gn_g"], vec["gn_b"]
    ln1_g_ref, ln1_b_ref = vec["ln1_g"], vec["ln1_b"]
    ln2_g_ref, ln2_b_ref, b1_ref, b2_ref = vec["ln2_g"], vec["ln2_b"], vec["b1"], vec["b2"]
    ln3_g_ref, ln3_b_ref = vec["ln3_g"], vec["ln3_b"]
    tm = PROMPT_BLOCK
    step = pl.program_id(0)
    n_blocks = pl.num_programs(0) - 1
    t = jnp.minimum(step, n_blocks - 1) % nt

    @pl.when(t == 0)
    def _():
        ubuf[:, 0:CONV_PAD, :] = jnp.zeros((CONV_SLABS, CONV_PAD, LANES), F32)
        pbuf[:, 0:POOL_PAD, :] = jnp.zeros((POOL_SLABS, POOL_PAD, LANES), F32)

    @pl.when(t > 0)
    def _():
        ubuf[:, 0:CONV_PAD, :] = ubuf[:, tm:tm + CONV_PAD, :]
        pbuf[:, 0:POOL_PAD, :] = pbuf[:, tm:tm + POOL_PAD, :]

    val = {}

    def mix_in():
        val["x"] = x_ref[0]
        h = _bdot(val["x"], _unpack(w_in_ref[...])) + b_in_ref[...]
        u = h[:, 0:D_CONV] * jax.nn.sigmoid(h[:, D_CONV:2 * D_CONV])
        for s in range(CONV_SLABS):
            ubuf[s, CONV_PAD:CONV_PAD + tm, :] = u[:, s * LANES:(s + 1) * LANES]
        for s in range(POOL_SLABS):
            pbuf[s, POOL_PAD:POOL_PAD + tm, :] = h[:, 2 * D_CONV + s * LANES:2 * D_CONV + (s + 1) * LANES]

    def att_scores():
        x1prev[...] = _layer_norm(x1buf[...], ln1_g_ref[...], ln1_b_ref[...])
        val["scores"] = _bdot(x1prev[...], _unpack(qk_ref[0]))

    def dec_probs():
        val["probs_dec"] = _sample_attn_probs(qs_ref, kc_ref)

    def att_values():
        for hd in range(X_HEADS):
            cols = slice(hd * N_MEM, (hd + 1) * N_MEM)
            s = val["scores"][:, cols]
            e = jnp.exp(s - jnp.max(s, axis=-1, keepdims=True))
            obuf[:, cols] = (e / jnp.sum(e, axis=-1, keepdims=True)).astype(BF16)

    def mix_windows():
        base = CONV_PAD - CONV_CTX
        for s in range(CONV_SLABS):
            sl = slice(s * LANES, (s + 1) * LANES)
            for c in range(tm // CONV_ROWS):
                r0 = c * CONV_ROWS
                acc = jnp.broadcast_to(conv_b_ref[:, sl], (CONV_ROWS, LANES))
                for k in range(CONV_WIDTH):
                    acc = acc + ubuf[s, base + r0 + k:base + r0 + k + CONV_ROWS, :] * conv_w_ref[k, :, sl]
                ybuf[r0:r0 + CONV_ROWS, sl] = acc

        pos = t * tm + lax.broadcasted_iota(jnp.int32, (tm, POOL_GROUP), 0)
        for gi, w in enumerate(POOL_WINDOWS):
            cur = pbuf[gi, POOL_PAD:POOL_PAD + tm, :]
            ws = cur
            for j in range(1, w):
                ws = ws + pbuf[gi, POOL_PAD - j:POOL_PAD - j + tm, :]
            cnt = jnp.minimum(pos + 1, w).astype(F32)
            dbuf[:, gi * POOL_GROUP:(gi + 1) * POOL_GROUP] = ws / cnt - cur

    def att_out():
        val["attn"] = jnp.dot(obuf[...], _unpack(vo_ref[0]), preferred_element_type=F32)

    def dec_values():
        _sample_attn_values(val["probs_dec"], vc_ref, os_ref, os_ref.shape[0] // vc_ref.shape[0])

    def ffn_norm():
        x2buf[...] = _layer_norm(DN_ALPHA * x1prev[...] + val["attn"], ln2_g_ref[...], ln2_b_ref[...])

    def ffn_up():
        hdn = jnp.maximum(_bdot(x2buf[...], _unpack(w1_ref[...])) + b1_ref[...], 0.0)
        hbuf[...] = (hdn * hdn).astype(BF16)

    def mix_acts():
        _mixer_acts(ybuf, dbuf, mixbuf, gmat_ref, gn_g_ref, gn_b_ref)

    def ffn_down():
        val["f"] = jnp.dot(hbuf[...], _unpack(w2_ref[...]), preferred_element_type=F32) + b2_ref[...]

    def mix_out():
        x1buf[...] = _mixer_out(mixbuf, val["x"], w_out_ref)

    def out_norm():
        y_ref[0] = _layer_norm(DN_ALPHA * x2buf[...] + val["f"], ln3_g_ref[...], ln3_b_ref[...])

    @pl.when(step == 0)
    def _():
        x1buf[...] = jnp.zeros((tm, D_MODEL), F32)

    step_order = (mix_in, att_scores, dec_probs, att_values, mix_windows, att_out, dec_values, ffn_norm, ffn_up,
                  mix_acts, ffn_down, mix_out, out_norm)
    for piece in step_order:
        piece()

    @pl.when(jnp.logical_and(t == nt - 1, step < n_blocks))
    def _():
        for s in range(CONV_SLABS):
            conv_new_ref[0, :, s * LANES:(s + 1) * LANES] = ubuf[s, CONV_PAD + tm - CONV_CTX:CONV_PAD + tm, :]
        for s in range(POOL_SLABS):
            pool_new_ref[0, :, s * LANES:(s + 1) * LANES] = pbuf[s, POOL_PAD + tm - POOL_CTX:POOL_PAD + tm, :]


def _prompt_call(x, qk, vo, q_dec, k_dec, v_dec, dec_seq, wa, wb, consts):
    assert dict(WB_BLOCKS)["w1"] == 0
    nb, seq, _ = x.shape
    tm = PROMPT_BLOCK
    nt = seq // tm
    n_blocks = nb * nt
    n_dec = k_dec.shape[0]
    dec_per_step = -(-n_dec // n_blocks)
    dec_steps = n_dec // dec_per_step
    dec_rows = dec_per_step * dec_seq
    assert dec_steps * dec_per_step == n_dec and dec_steps <= n_blocks + 1
    assert dec_rows % SUBLANES == 0 and SAMPLE_ATTN_ROWS % dec_rows == 0
    cur = lambda g: jnp.minimum(g, n_blocks - 1)
    prev = lambda g: jnp.maximum(g - 1, 0)
    dec = lambda g: jnp.minimum(g, dec_steps - 1)
    in_specs = [
        pl.BlockSpec((1, tm, D_MODEL), lambda g: (cur(g) // nt, cur(g) % nt, 0)),
        pl.BlockSpec((1,) + qk.shape[1:], lambda g: (prev(g) // nt, 0, 0)),
        pl.BlockSpec((1,) + vo.shape[1:], lambda g: (prev(g) // nt, 0, 0)),
        pl.BlockSpec((dec_rows, D_MODEL), lambda g: (dec(g), 0)),
        pl.BlockSpec((dec_per_step, KV_ROWS, LANES), lambda g: (dec(g), 0, 0)),
        pl.BlockSpec((dec_per_step, KV_ROWS, LANES), lambda g: (dec(g), 0, 0)),
        _const_spec(wa.shape),
        pl.BlockSpec((D_MODEL // 2, D_FF), lambda g: (0, 0), pipeline_mode=pl.Buffered(1)),
    ] + [_const_spec(w.shape) for w in consts]
    out_specs = [
        pl.BlockSpec((1, tm, D_MODEL), lambda g: (prev(g) // nt, prev(g) % nt, 0)),
        pl.BlockSpec((1, CONV_CTX, D_CONV), lambda g: (cur(g) // nt, 0, 0)),
        pl.BlockSpec((1, POOL_CTX, D_POOL), lambda g: (cur(g) // nt, 0, 0)),
        pl.BlockSpec((dec_rows, D_MODEL), lambda g: (dec(g), 0)),
    ]
    out_shape = [
        jax.ShapeDtypeStruct((nb, seq, D_MODEL), F32),
        jax.ShapeDtypeStruct((nb, CONV_CTX, D_CONV), F32),
        jax.ShapeDtypeStruct((nb, POOL_CTX, D_POOL), F32),
        jax.ShapeDtypeStruct(q_dec.shape, F32),
    ]
    scratch = [
        pltpu.VMEM((CONV_SLABS, CONV_PAD + tm, LANES), F32),
        pltpu.VMEM((POOL_SLABS, POOL_PAD + tm, LANES), F32),
        pltpu.VMEM((tm, D_CONV), F32),
        pltpu.VMEM((tm, D_POOL), F32),
        pltpu.VMEM((tm, D_MODEL), BF16),
        pltpu.VMEM((tm, D_MODEL), BF16),
        pltpu.VMEM((tm, D_MODEL), F32),
        pltpu.VMEM((tm, D_MODEL), F32),
        pltpu.VMEM((tm, D_MODEL), F32),
        pltpu.VMEM((tm, D_FF), BF16),
    ]
    return pl.pallas_call(
        functools.partial(_prompt_kernel, nt=nt),
        grid=(n_blocks + 1,),
        in_specs=in_specs,
        out_specs=out_specs,
        out_shape=out_shape,
        scratch_shapes=scratch,
        compiler_params=_params(1),
        name="prompt_layer",
    )(x, qk, vo, q_dec, k_dec, v_dec, wa, wb, *consts)


def _sample_mixer_kernel(x_ref, sconv_ref, spool_ref,
                         wa_ref, xq_ref, gmat_ref, conv_w_ref, vec_ref,
                         x1_ref, q_ref, conv_new_ref, pool_new_ref,
                         hbuf, yslab, dslab, ybuf, dbuf, mixbuf, *, dec_seq):
    wa, vec = _views(wa_ref, WA_FIELDS), _views(vec_ref, VEC_FIELDS)
    w_in_ref, w_out_ref = wa["w_in"], wa["w_out"]
    b_in_ref, conv_b_ref, gn_g_ref, gn_b_ref = vec["b_in"], vec["conv_b"], vec["gn_g"], vec["gn_b"]
    ln1_g_ref, ln1_b_ref = vec["ln1_g"], vec["ln1_b"]
    gb = SAMPLE_MIX_BATCH
    ts = dec_seq
    x = x_ref[...]
    h = _bdot(x, _unpack(w_in_ref[...])) + b_in_ref[...]
    u = h[:, 0:D_CONV] * jax.nn.sigmoid(h[:, D_CONV:2 * D_CONV])
    for s in range(CONV_SLABS):
        hbuf[s] = u[:, s * LANES:(s + 1) * LANES]
    for s in range(POOL_SLABS):
        hbuf[CONV_SLABS + s] = h[:, 2 * D_CONV + s * LANES:2 * D_CONV + (s + 1) * LANES]

    def step_rows(slab, t):
        return hbuf[slab, pl.ds(t, gb, stride=ts), :]

    for s in range(CONV_SLABS):
        sl = slice(s * LANES, (s + 1) * LANES)
        new = [step_rows(s, t) for t in range(ts)]
        ext = lambda j: sconv_ref[j, :, sl] if j < CONV_CTX else new[j - CONV_CTX]
        for j in range(CONV_CTX - ts):
            conv_new_ref[j, :, sl] = sconv_ref[j + ts, :, sl]
        for t in range(ts):
            conv_new_ref[CONV_CTX - ts + t, :, sl] = new[t]
            acc = jnp.broadcast_to(conv_b_ref[:, sl], (gb, LANES))
            for k in range(CONV_WIDTH):
                acc = acc + ext(t + k) * conv_w_ref[k, :, sl]
            yslab[s, pl.ds(t, gb, stride=ts), :] = acc

    for g, w in enumerate(POOL_WINDOWS):
        sl = slice(g * LANES, (g + 1) * LANES)
        new = [step_rows(CONV_SLABS + g, t) for t in range(ts)]
        ext = lambda j: spool_ref[j, :, sl] if j < POOL_CTX else new[j - POOL_CTX]
        for j in range(POOL_CTX - ts):
            pool_new_ref[j, :, sl] = spool_ref[j + ts, :, sl]
        for t in range(ts):
            pool_new_ref[POOL_CTX - ts + t, :, sl] = new[t]
            ws = new[t]
            for j in range(1, w):
                ws = ws + ext(POOL_CTX + t - j)
            cnt = float(min(PAST_LEN + t + 1, w))
            dslab[g, pl.ds(t, gb, stride=ts), :] = ws / cnt - new[t]

    for s in range(CONV_SLABS):
        ybuf[:, s * LANES:(s + 1) * LANES] = yslab[s]
    for s in range(POOL_SLABS):
        dbuf[:, s * LANES:(s + 1) * LANES] = dslab[s]

    _mixer_acts(ybuf, dbuf, mixbuf, gmat_ref, gn_g_ref, gn_b_ref)
    x1 = _layer_norm(_mixer_out(mixbuf, x, w_out_ref), ln1_g_ref[...], ln1_b_ref[...])
    x1_ref[...] = x1
    q_ref[...] = _bdot(x1, _unpack(xq_ref[...])) * ATTN_SCALE


def _sample_mixer_call(xs, sconv_tm, spool_tm, wa, wkv, consts, dec_seq):
    xq_block = dict(WKV_BLOCKS)["xq"]
    n_tok = xs.shape[0]
    nb = n_tok // dec_seq
    gb = SAMPLE_MIX_BATCH
    rows = gb * dec_seq
    row_spec = pl.BlockSpec((rows, D_MODEL), lambda i: (i, 0))
    conv_spec = pl.BlockSpec((CONV_CTX, gb, D_CONV), lambda i: (0, i, 0))
    pool_spec = pl.BlockSpec((POOL_CTX, gb, D_POOL), lambda i: (0, i, 0))
    out_shape = [
        jax.ShapeDtypeStruct((n_tok, D_MODEL), F32),
        jax.ShapeDtypeStruct((n_tok, D_MODEL), F32),
        jax.ShapeDtypeStruct((CONV_CTX, nb, D_CONV), F32),
        jax.ShapeDtypeStruct((POOL_CTX, nb, D_POOL), F32),
    ]
    scratch = [
        pltpu.VMEM((CONV_SLABS + POOL_SLABS, rows, LANES), F32),
        pltpu.VMEM((CONV_SLABS, rows, LANES), F32),
        pltpu.VMEM((POOL_SLABS, rows, LANES), F32),
        pltpu.VMEM((rows, D_CONV), F32),
        pltpu.VMEM((rows, D_POOL), F32),
        pltpu.VMEM((rows, D_MODEL), BF16),
    ]
    return pl.pallas_call(
        functools.partial(_sample_mixer_kernel, dec_seq=dec_seq),
        grid=(nb // gb,),
        in_specs=[row_spec, conv_spec, pool_spec, _const_spec(wa.shape),
                  pl.BlockSpec((D_MODEL // 2, D_MODEL), lambda i: (0, xq_block), pipeline_mode=pl.Buffered(1))]
        + [_const_spec(w.shape) for w in consts],
        out_specs=[row_spec, row_spec, conv_spec, pool_spec],
        out_shape=out_shape,
        scratch_shapes=scratch,
        compiler_params=_params(1),
        name="sample_mixer",
    )(xs, sconv_tm, spool_tm, wa, wkv, *consts)


def _load_head(ref, b, hd):
    parts = [ref[b, pl.ds(c * X_HEADS + hd, N_MEM, stride=KV_ROW_STRIDE), :] for c in range(HEAD_CHUNKS)]
    return jnp.concatenate(parts, axis=1).astype(BF16)


def _sample_attn_probs(q_ref, k_ref):
    q8 = q_ref[...]
    q = jnp.concatenate([q8] * (SAMPLE_ATTN_ROWS // q8.shape[0]), axis=0).astype(BF16)
    s = jnp.concatenate(
        [lax.dot_general(q[:, hd * X_HEAD_DIM:(hd + 1) * X_HEAD_DIM], _load_head(k_ref, b, hd),
                         (((1,), (1,)), ((), ())), preferred_element_type=F32)
         for b in range(k_ref.shape[0]) for hd in range(X_HEADS)], axis=0)
    e = jnp.exp(s - jnp.max(s, axis=-1, keepdims=True))
    return (e / jnp.sum(e, axis=-1, keepdims=True)).astype(BF16)


def _sample_attn_values(a, v_ref, o_ref, ts):
    for b in range(v_ref.shape[0]):
        for hd in range(X_HEADS):
            i = b * X_HEADS + hd
            o = jnp.dot(a[i * SAMPLE_ATTN_ROWS:(i + 1) * SAMPLE_ATTN_ROWS, :], _load_head(v_ref, b, hd),
                        preferred_element_type=F32)
            o_ref[b * ts:(b + 1) * ts, hd * X_HEAD_DIM:(hd + 1) * X_HEAD_DIM] = o[b * ts:(b + 1) * ts, :]


def _sample_tail_kernel(x1_ref, o_ref, xo_ref, w1_ref, b1_ref, w2_ref, vec_ref, y_ref, x2buf, x2bf, fbuf):
    vec = _views(vec_ref, VEC_FIELDS)
    ln2_g_ref, ln2_b_ref, b2_ref, ln3_g_ref, ln3_b_ref = (
        vec["ln2_g"], vec["ln2_b"], vec["b2"], vec["ln3_g"], vec["ln3_b"])
    c = pl.program_id(0)

    @pl.when(c == 0)
    def _():
        attn = _bdot(o_ref[...], _unpack(xo_ref[...]))
        x2 = _layer_norm(DN_ALPHA * x1_ref[...] + attn, ln2_g_ref[...], ln2_b_ref[...])
        x2buf[...] = x2
        x2bf[...] = x2.astype(BF16)
        fbuf[...] = jnp.broadcast_to(b2_ref[...], fbuf.shape)

    hdn = jnp.maximum(jnp.dot(x2bf[...], _unpack(w1_ref[...]), preferred_element_type=F32) + b1_ref[...], 0.0)
    fbuf[...] += _bdot(hdn * hdn, _unpack(w2_ref[...]))

    @pl.when(c == pl.num_programs(0) - 1)
    def _():
        y_ref[...] = _layer_norm(DN_ALPHA * x2buf[...] + fbuf[...], ln3_g_ref[...], ln3_b_ref[...])


def _sample_tail_call(x1, o, wb, w2_p, vecs):
    n_tok = x1.shape[0]
    fc = TAIL_FF_CHUNK
    blocks = dict(WB_BLOCKS)
    assert fc == D_MODEL and VEC_FIELDS[0] == ("b1", D_FF) and blocks["w1"] == 0
    in_specs = [
        _const_spec(x1.shape), _const_spec(o.shape),
        pl.BlockSpec((D_MODEL // 2, D_MODEL), lambda c: (0, blocks["xo"]), pipeline_mode=pl.Buffered(1)),
        pl.BlockSpec((D_MODEL // 2, fc), lambda c: (0, c)),
        pl.BlockSpec((1, fc), lambda c: (0, c)),
        pl.BlockSpec((fc // 2, D_MODEL), lambda c: (c, 0)),
        _const_spec(vecs.shape),
    ]
    return pl.pallas_call(
        _sample_tail_kernel,
        grid=(D_FF // fc,),
        in_specs=in_specs,
        out_specs=pl.BlockSpec((n_tok, D_MODEL), lambda c: (0, 0)),
        out_shape=jax.ShapeDtypeStruct((n_tok, D_MODEL), F32),
        scratch_shapes=[pltpu.VMEM((n_tok, D_MODEL), F32), pltpu.VMEM((n_tok, D_MODEL), BF16),
                        pltpu.VMEM((n_tok, D_MODEL), F32)],
        compiler_params=_params(1),
        name="sample_tail",
    )(x1, o, wb, wb, vecs, w2_p, vecs)


def _group_indicator():
    gid = jnp.arange(GN_HALF) // CONV_HEAD_DIM
    return (gid[:, None] == gid[None, :]).astype(F32)


def _to_head_split_view(a):
    lead = a.shape[:-3]
    n = len(lead)
    a = a.reshape(lead + (N_MEM, X_HEADS, HEAD_CHUNKS, LANES))
    a = a.transpose(tuple(range(n)) + (n, n + 2, n + 1, n + 3))
    return a.reshape(lead + (KV_ROWS, LANES))


def _from_head_split_view(a):
    lead = a.shape[:-2]
    n = len(lead)
    a = a.reshape(lead + (N_MEM, HEAD_CHUNKS, X_HEADS, LANES))
    a = a.transpose(tuple(range(n)) + (n, n + 2, n + 1, n + 3))
    return a.reshape(lead + (N_MEM, X_HEADS, X_HEAD_DIM))


def kernel(x_prompt, x_sample, mem_prompt, cache_mem_k, cache_mem_v, state_conv, state_pool, w_in, b_in, conv_w, conv_b, gn_g, gn_b, pool_w, pool_scale, w_out, ln1_g, ln1_b, xq_w, xk_w, xv_w, xo_w, ln2_g, ln2_b, w1, b1, w2, b2, ln3_g, ln3_b):
    assert w_in.shape[0] == DEPTH == 1
    n_prompt, seq, _ = x_prompt.shape
    n_dec, dec_seq, _ = x_sample.shape
    assert seq % PROMPT_BLOCK == 0 and n_dec % SAMPLE_MIX_BATCH == 0

    named = dict(w_in=w_in, w_out=w_out, xq=xq_w, xo=xo_w, w1=w1, b1=b1, b_in=b_in, conv_b=conv_b, gn_g=gn_g,
                 gn_b=gn_b, ln1_g=ln1_g, ln1_b=ln1_b, ln2_g=ln2_g, ln2_b=ln2_b, b2=b2,
                 ln3_g=ln3_g, ln3_b=ln3_b)
    named.update(xk=xk_w, xv=xv_w)
    wa, wb, wkv, gmat_p, w2_p, vecs = _cast_call(
        [[named[name][0] for name, _ in WA_FIELDS], [named[name][0] for name, _ in WB_FIELDS],
         [named[name][0] for name, _ in WKV_FIELDS], [_group_indicator()], [w2[0]]],
        [named[name] for name, _ in VEC_FIELDS],
        pool_w[0], pool_scale, fold_index=[name for name, _ in WA_FIELDS].index("w_out"))
    conv_taps = conv_w.transpose(1, 0, 2)

    xs = x_sample.reshape(n_dec * dec_seq, D_MODEL)
    x1_s, q_s, conv_s, pool_s = _sample_mixer_call(
        xs, state_conv[0].transpose(1, 0, 2), state_pool[0].transpose(1, 0, 2),
        wa, wkv, (gmat_p, conv_taps, vecs), dec_seq)

    k_p, v_p, qk_p, vo_p = _kv_proj_call(mem_prompt, wkv, wb)
    y_p, conv_p, pool_p, o_s = _prompt_call(
        x_prompt, qk_p, vo_p, q_s, _to_head_split_view(cache_mem_k[0]), _to_head_split_view(cache_mem_v[0]),
        dec_seq, wa, wb, (w2_p, gmat_p, conv_taps, vecs))

    y_s = _sample_tail_call(x1_s, o_s, wb, w2_p, vecs)

    return (y_p, y_s.reshape(n_dec, dec_seq, D_MODEL),
            _from_head_split_view(k_p)[None], _from_head_split_view(v_p)[None],
            conv_p[None], conv_s.transpose(1, 0, 2)[None], pool_p[None], pool_s.transpose(1, 0, 2)[None])
```

```python
import functools

import jax
import jax.numpy as jnp
from jax import lax
from jax.experimental import pallas as pl
from jax.experimental.pallas import tpu as pltpu

F32 = jnp.float32
BF16 = jnp.bfloat16
U32 = jnp.uint32

LANES = 128
SUBLANES = 8
D_MODEL = 1024
D_CONV = 512
D_POOL = 512
D_IN = 2 * D_CONV + D_POOL
CONV_WIDTH = 31
CONV_CTX = CONV_WIDTH - 1
CONV_HEADS = 8
CONV_HEAD_DIM = D_CONV // CONV_HEADS
POOL_WINDOWS = (2, 4, 8, 16)
POOL_GROUP = D_POOL // len(POOL_WINDOWS)
POOL_CTX = max(POOL_WINDOWS) - 1
N_MEM = 256
X_HEADS = 4
X_HEAD_DIM = D_MODEL // X_HEADS
HEAD_CHUNKS = X_HEAD_DIM // LANES
KV_ROWS = N_MEM * X_HEADS * HEAD_CHUNKS
KV_ROW_STRIDE = X_HEADS * HEAD_CHUNKS
D_FF = 4 * D_MODEL
LN_EPS = 1e-5
DEPTH = 1
PAST_LEN = 16384
DN_ALPHA = (2.0 * DEPTH) ** 0.25
ATTN_SCALE = X_HEAD_DIM ** -0.5

assert POOL_GROUP == LANES and X_HEAD_DIM % LANES == 0

CONV_SLABS = D_CONV // LANES
POOL_SLABS = D_POOL // LANES
CONV_PAD = 32
POOL_PAD = 16
GN_HALF = 256

PROMPT_BLOCK = 256
CONV_ROWS = 32
SAMPLE_MIX_BATCH = 32
SAMPLE_ATTN_ROWS = 16
TAIL_FF_CHUNK = 1024
CAST_STEPS = 8
VMEM_LIMIT = 56 * 1024 * 1024


def _unpack(w_u32):
    return pltpu.bitcast(w_u32, BF16)


def _pack(w_bf16):
    return pltpu.bitcast(w_bf16, U32)


def _bdot(a, w):
    return jnp.dot(a.astype(BF16), w, preferred_element_type=F32)


def _layer_norm(x, g, b):
    mu = jnp.mean(x, axis=-1, keepdims=True)
    d = x - mu
    var = jnp.mean(d * d, axis=-1, keepdims=True)
    return d * lax.rsqrt(var + LN_EPS) * g + b


def _group_norm_swish(y, gmat, gn_g, gn_b):
    inv = 1.0 / CONV_HEAD_DIM
    yc = y - _bdot(y, gmat) * inv
    rest = _bdot(yc, gmat) * inv
    var = jnp.maximum(_bdot(yc * yc, gmat) * inv - rest * rest, 0.0)
    d = yc - rest
    n = d * lax.rsqrt(var + LN_EPS) * gn_g + gn_b
    return n * jax.nn.sigmoid(n)


def _mixer_acts(ybuf, dbuf, mixbuf, gmat_ref, gn_g_ref, gn_b_ref):
    mixbuf[:, D_CONV:] = dbuf[...].astype(BF16)
    m = ybuf.shape[0]
    n_half = D_CONV // GN_HALF
    cols = [slice(j * GN_HALF, (j + 1) * GN_HALF) for j in range(n_half)]
    o = _group_norm_swish(
        jnp.concatenate([ybuf[:, sl] for sl in cols], axis=0), _unpack(gmat_ref[...]),
        jnp.concatenate([jnp.broadcast_to(gn_g_ref[:, sl], (m, GN_HALF)) for sl in cols], axis=0),
        jnp.concatenate([jnp.broadcast_to(gn_b_ref[:, sl], (m, GN_HALF)) for sl in cols], axis=0))
    for j, sl in enumerate(cols):
        mixbuf[:, sl] = o[j * m:(j + 1) * m, :].astype(BF16)


def _mixer_out(mixbuf, x, w_out_ref):
    return DN_ALPHA * x + jnp.dot(mixbuf[...], _unpack(w_out_ref[...]), preferred_element_type=F32)


def _const_spec(shape):
    nd = len(shape)
    return pl.BlockSpec(shape, lambda *_: (0,) * nd, pipeline_mode=pl.Buffered(1))


def _params(n_axes):
    return pltpu.CompilerParams(dimension_semantics=("arbitrary",) * n_axes, vmem_limit_bytes=VMEM_LIMIT)


VEC_FIELDS = (("b1", D_FF), ("b_in", D_IN), ("conv_b", D_CONV), ("gn_g", D_CONV), ("gn_b", D_CONV),
              ("ln1_g", D_MODEL), ("ln1_b", D_MODEL), ("ln2_g", D_MODEL),
              ("ln2_b", D_MODEL), ("b2", D_MODEL), ("ln3_g", D_MODEL), ("ln3_b", D_MODEL))
VEC_LEN = sum(n for _, n in VEC_FIELDS)
WA_FIELDS = (("w_in", D_IN), ("w_out", D_MODEL))
WKV_FIELDS = (("xk", D_MODEL), ("xv", D_MODEL), ("xq", D_MODEL))
WB_FIELDS = (("w1", D_FF), ("xo", D_MODEL))
WKV_BLOCKS = (("xk", 0), ("xv", 1), ("xq", 2))
WB_BLOCKS = (("w1", 0), ("xo", D_FF // D_MODEL))


def _views(ref, fields):
    out, off = {}, 0
    for name, n in fields:
        out[name] = ref.at[:, off:off + n]
        off += n
    return out


def _cast_kernel(pool_w_ref, pool_scale_ref, *refs, widths, n_vecs, fold_index):
    n_mats = sum(len(g) for g in widths)
    srcs, vec_srcs = refs[:n_mats], refs[n_mats:n_mats + n_vecs]
    dsts, vec_dst = refs[n_mats + n_vecs:-1], refs[-1]
    step = pl.program_id(0)
    i = 0
    for dst, group in zip(dsts, widths):
        off = 0
        for n in group:
            w = srcs[i][...]
            if i == fold_index:
                folded = jnp.dot(pool_w_ref[0] * pool_scale_ref[...], w, precision=lax.Precision.HIGHEST,
                                 preferred_element_type=F32)
                w = jnp.where(step >= D_CONV // POOL_GROUP, folded, w)
            dst[:, off:off + n] = _pack(w.astype(BF16))
            off += n
            i += 1
    @pl.when(pl.program_id(0) == 0)
    def _():
        off = 0
        for src in vec_srcs:
            n = src.shape[1]
            vec_dst[:, off:off + n] = src[...]
            off += n


def _cast_call(groups, vectors, pool_w, pool_scale, fold_index):
    n_groups = pool_w.shape[0]
    first = D_CONV // POOL_GROUP
    assert D_MODEL // CAST_STEPS == POOL_GROUP and first + n_groups == CAST_STEPS
    grp = lambda i: jnp.clip(i - first, 0, n_groups - 1)
    in_specs = [pl.BlockSpec((1, POOL_GROUP, POOL_GROUP), lambda i: (grp(i), 0, 0)),
                pl.BlockSpec((1, POOL_GROUP), lambda i: (0, grp(i)))]
    out_specs, out_shape, widths = [], [], []
    for group in groups:
        k = group[0].shape[0]
        rows = k // CAST_STEPS
        assert rows * CAST_STEPS == k and rows % 16 == 0 and all(w.shape[0] == k for w in group)
        assert all(w.shape[1] % LANES == 0 for w in group)
        widths.append(tuple(w.shape[1] for w in group))
        total = sum(widths[-1])
        in_specs += [pl.BlockSpec((rows, w.shape[1]), lambda i: (i, 0)) for w in group]
        out_specs.append(pl.BlockSpec((rows // 2, total), lambda i: (i, 0)))
        out_shape.append(jax.ShapeDtypeStruct((k // 2, total), U32))
    assert all(v.shape[1] % LANES == 0 for v in vectors)
    vec_len = sum(v.shape[1] for v in vectors)
    in_specs += [pl.BlockSpec(v.shape, lambda i: (0, 0)) for v in vectors]
    out_specs.append(pl.BlockSpec((1, vec_len), lambda i: (0, 0)))
    out_shape.append(jax.ShapeDtypeStruct((1, vec_len), F32))
    return pl.pallas_call(
        functools.partial(_cast_kernel, widths=tuple(widths), n_vecs=len(vectors), fold_index=fold_index),
        grid=(CAST_STEPS,), in_specs=in_specs, out_specs=out_specs, out_shape=out_shape,
        compiler_params=_params(1), name="cast_weights",
    )(pool_w, pool_scale, *[w for group in groups for w in group], *vectors)


def _store_head_split(dst_ref, val):
    for hd in range(X_HEADS):
        for c in range(HEAD_CHUNKS):
            col = hd * X_HEAD_DIM + c * LANES
            dst_ref[0, pl.ds(c * X_HEADS + hd, N_MEM, stride=KV_ROW_STRIDE), :] = val[:, col:col + LANES]


def _kv_proj_kernel(mem_ref, wkv_ref, xo_ref, k_ref, v_ref, qk_ref, vo_ref):
    wkv = _views(wkv_ref, WKV_FIELDS)
    m = mem_ref[0].astype(BF16)
    k = jnp.dot(m, _unpack(wkv["xk"][...]), preferred_element_type=F32)
    v = jnp.dot(m, _unpack(wkv["xv"][...]), preferred_element_type=F32)
    _store_head_split(k_ref, k)
    _store_head_split(v_ref, v)
    xq = _unpack(wkv["xq"][...])
    xo = _unpack(xo_ref[...])
    for hd in range(X_HEADS):
        sl = slice(hd * X_HEAD_DIM, (hd + 1) * X_HEAD_DIM)
        qk = jnp.dot(xq[:, sl], k[:, sl].T.astype(BF16), preferred_element_type=F32) * ATTN_SCALE
        qk_ref[0, :, hd * N_MEM:(hd + 1) * N_MEM] = _pack(qk.astype(BF16))
        vo = jnp.dot(v[:, sl].astype(BF16), xo[sl, :], preferred_element_type=F32)
        vo_ref[0, hd * N_MEM // 2:(hd + 1) * N_MEM // 2, :] = _pack(vo.astype(BF16))


def _kv_proj_call(mem, wkv, wb):
    nb = mem.shape[0]
    blk = lambda s: pl.BlockSpec((1,) + s, lambda b: (b, 0, 0))
    xo_block = dict(WB_BLOCKS)["xo"]
    return pl.pallas_call(
        _kv_proj_kernel,
        grid=(nb,),
        in_specs=[blk((N_MEM, D_MODEL)), _const_spec(wkv.shape),
                  pl.BlockSpec((D_MODEL // 2, D_MODEL), lambda b: (0, xo_block), pipeline_mode=pl.Buffered(1))],
        out_specs=[blk((KV_ROWS, LANES)), blk((KV_ROWS, LANES)),
                   blk((D_MODEL // 2, X_HEADS * N_MEM)), blk((X_HEADS * N_MEM // 2, D_MODEL))],
        out_shape=[
            jax.ShapeDtypeStruct((nb, KV_ROWS, LANES), F32),
            jax.ShapeDtypeStruct((nb, KV_ROWS, LANES), F32),
            jax.ShapeDtypeStruct((nb, D_MODEL // 2, X_HEADS * N_MEM), U32),
            jax.ShapeDtypeStruct((nb, X_HEADS * N_MEM // 2, D_MODEL), U32),
        ],
        compiler_params=_params(1),
        name="kv_proj",
    )(mem, wkv, wb)


def _prompt_kernel(x_ref, qk_ref, vo_ref, qs_ref, kc_ref, vc_ref,
                   wa_ref, w1_ref, w2_ref, gmat_ref, conv_w_ref, vec_ref,
                   y_ref, conv_new_ref, pool_new_ref, os_ref,
                   ubuf, pbuf, ybuf, dbuf, mixbuf, obuf, x1buf, x1prev, x2buf, hbuf, *, nt):
    wa, vec = _views(wa_ref, WA_FIELDS), _views(vec_ref, VEC_FIELDS)
    w_in_ref, w_out_ref = wa["w_in"], wa["w_out"]
    b_in_ref, conv_b_ref, gn_g_ref, gn_b_ref = vec["b_in"], vec["conv_b"], vec["gn_g"], vec["gn_b"]
    ln1_g_ref, ln1_b_ref = vec["ln1_g"], vec["ln1_b"]
    ln2_g_ref, ln2_b_ref, b1_ref, b2_ref = vec["ln2_g"], vec["ln2_b"], vec["b1"], vec["b2"]
    ln3_g_ref, ln3_b_ref = vec["ln3_g"], vec["ln3_b"]
    tm = PROMPT_BLOCK
    step = pl.program_id(0)
    n_blocks = pl.num_programs(0) - 1
    t = jnp.minimum(step, n_blocks - 1) % nt

    @pl.when(t == 0)
    def _():
        ubuf[:, 0:CONV_PAD, :] = jnp.zeros((CONV_SLABS, CONV_PAD, LANES), F32)
        pbuf[:, 0:POOL_PAD, :] = jnp.zeros((POOL_SLABS, POOL_PAD, LANES), F32)

    @pl.when(t > 0)
    def _():
        ubuf[:, 0:CONV_PAD, :] = ubuf[:, tm:tm + CONV_PAD, :]
        pbuf[:, 0:POOL_PAD, :] = pbuf[:, tm:tm + POOL_PAD, :]

    val = {}

    def mix_in():
        val["x"] = x_ref[0]
        h = _bdot(val["x"], _unpack(w_in_ref[...])) + b_in_ref[...]
        u = h[:, 0:D_CONV] * jax.nn.sigmoid(h[:, D_CONV:2 * D_CONV])
        for s in range(CONV_SLABS):
            ubuf[s, CONV_PAD:CONV_PAD + tm, :] = u[:, s * LANES:(s + 1) * LANES]
        for s in range(POOL_SLABS):
            pbuf[s, POOL_PAD:POOL_PAD + tm, :] = h[:, 2 * D_CONV + s * LANES:2 * D_CONV + (s + 1) * LANES]

    def att_scores():
        x1prev[...] = _layer_norm(x1buf[...], ln1_g_ref[...], ln1_b_ref[...])
        val["scores"] = _bdot(x1prev[...], _unpack(qk_ref[0]))

    def dec_probs():
        val["probs_dec"] = _sample_attn_probs(qs_ref, kc_ref)

    def att_values():
        for hd in range(X_HEADS):
            cols = slice(hd * N_MEM, (hd + 1) * N_MEM)
            s = val["scores"][:, cols]
            e = jnp.exp(s - jnp.max(s, axis=-1, keepdims=True))
            obuf[:, cols] = (e / jnp.sum(e, axis=-1, keepdims=True)).astype(BF16)

    def mix_windows():
        base = CONV_PAD - CONV_CTX
        for s in range(CONV_SLABS):
            sl = slice(s * LANES, (s + 1) * LANES)
            for c in range(tm // CONV_ROWS):
                r0 = c * CONV_ROWS
                acc = jnp.broadcast_to(conv_b_ref[:, sl], (CONV_ROWS, LANES))
                for k in range(CONV_WIDTH):
                    acc = acc + ubuf[s, base + r0 + k:base + r0 + k + CONV_ROWS, :] * conv_w_ref[k, :, sl]
                ybuf[r0:r0 + CONV_ROWS, sl] = acc

        pos = t * tm + lax.broadcasted_iota(jnp.int32, (tm, POOL_GROUP), 0)
        for gi, w in enumerate(POOL_WINDOWS):
            cur = pbuf[gi, POOL_PAD:POOL_PAD + tm, :]
            ws = cur
            for j in range(1, w):
                ws = ws + pbuf[gi, POOL_PAD - j:POOL_PAD - j + tm, :]
            cnt = jnp.minimum(pos + 1, w).astype(F32)
            dbuf[:, gi * POOL_GROUP:(gi + 1) * POOL_GROUP] = ws / cnt - cur

    def att_out():
        val["attn"] = jnp.dot(obuf[...], _unpack(vo_ref[0]), preferred_element_type=F32)

    def dec_values():
        _sample_attn_values(val["probs_dec"], vc_ref, os_ref, os_ref.shape[0] // vc_ref.shape[0])

    def ffn_norm():
        x2buf[...] = _layer_norm(DN_ALPHA * x1prev[...] + val["attn"], ln2_g_ref[...], ln2_b_ref[...])

    def ffn_up():
        hdn = jnp.maximum(_bdot(x2buf[...], _unpack(w1_ref[...])) + b1_ref[...], 0.0)
        hbuf[...] = (hdn * hdn).astype(BF16)

    def mix_acts():
        _mixer_acts(ybuf, dbuf, mixbuf, gmat_ref, gn_g_ref, gn_b_ref)

    def ffn_down():
        val["f"] = jnp.dot(hbuf[...], _unpack(w2_ref[...]), preferred_element_type=F32) + b2_ref[...]

    def mix_out():
        x1buf[...] = _mixer_out(mixbuf, val["x"], w_out_ref)

    def out_norm():
        y_ref[0] = _layer_norm(DN_ALPHA * x2buf[...] + val["f"], ln3_g_ref[...], ln3_b_ref[...])

    @pl.when(step == 0)
    def _():
        x1buf[...] = jnp.zeros((tm, D_MODEL), F32)

    step_order = (mix_in, att_scores, dec_probs, att_values, mix_windows, att_out, dec_values, ffn_norm, ffn_up,
                  mix_acts, ffn_down, mix_out, out_norm)
    for piece in step_order:
        piece()

    @pl.when(jnp.logical_and(t == nt - 1, step < n_blocks))
    def _():
        for s in range(CONV_SLABS):
            conv_new_ref[0, :, s * LANES:(s + 1) * LANES] = ubuf[s, CONV_PAD + tm - CONV_CTX:CONV_PAD + tm, :]
        for s in range(POOL_SLABS):
            pool_new_ref[0, :, s * LANES:(s + 1) * LANES] = pbuf[s, POOL_PAD + tm - POOL_CTX:POOL_PAD + tm, :]


def _prompt_call(x, qk, vo, q_dec, k_dec, v_dec, dec_seq, wa, wb, consts):
    assert dict(WB_BLOCKS)["w1"] == 0
    nb, seq, _ = x.shape
    tm = PROMPT_BLOCK
    nt = seq // tm
    n_blocks = nb * nt
    n_dec = k_dec.shape[0]
    dec_per_step = -(-n_dec // n_blocks)
    dec_steps = n_dec // dec_per_step
    dec_rows = dec_per_step * dec_seq
    assert dec_steps * dec_per_step == n_dec and dec_steps <= n_blocks + 1
    assert dec_rows % SUBLANES == 0 and SAMPLE_ATTN_ROWS % dec_rows == 0
    cur = lambda g: jnp.minimum(g, n_blocks - 1)
    prev = lambda g: jnp.maximum(g - 1, 0)
    dec = lambda g: jnp.minimum(g, dec_steps - 1)
    in_specs = [
        pl.BlockSpec((1, tm, D_MODEL), lambda g: (cur(g) // nt, cur(g) % nt, 0)),
        pl.BlockSpec((1,) + qk.shape[1:], lambda g: (prev(g) // nt, 0, 0)),
        pl.BlockSpec((1,) + vo.shape[1:], lambda g: (prev(g) // nt, 0, 0)),
        pl.BlockSpec((dec_rows, D_MODEL), lambda g: (dec(g), 0)),
        pl.BlockSpec((dec_per_step, KV_ROWS, LANES), lambda g: (dec(g), 0, 0)),
        pl.BlockSpec((dec_per_step, KV_ROWS, LANES), lambda g: (dec(g), 0, 0)),
        _const_spec(wa.shape),
        pl.BlockSpec((D_MODEL // 2, D_FF), lambda g: (0, 0), pipeline_mode=pl.Buffered(1)),
    ] + [_const_spec(w.shape) for w in consts]
    out_specs = [
        pl.BlockSpec((1, tm, D_MODEL), lambda g: (prev(g) // nt, prev(g) % nt, 0)),
        pl.BlockSpec((1, CONV_CTX, D_CONV), lambda g: (cur(g) // nt, 0, 0)),
        pl.BlockSpec((1, POOL_CTX, D_POOL), lambda g: (cur(g) // nt, 0, 0)),
        pl.BlockSpec((dec_rows, D_MODEL), lambda g: (dec(g), 0)),
    ]
    out_shape = [
        jax.ShapeDtypeStruct((nb, seq, D_MODEL), F32),
        jax.ShapeDtypeStruct((nb, CONV_CTX, D_CONV), F32),
        jax.ShapeDtypeStruct((nb, POOL_CTX, D_POOL), F32),
        jax.ShapeDtypeStruct(q_dec.shape, F32),
    ]
    scratch = [
        pltpu.VMEM((CONV_SLABS, CONV_PAD + tm, LANES), F32),
        pltpu.VMEM((POOL_SLABS, POOL_PAD + tm, LANES), F32),
        pltpu.VMEM((tm, D_CONV), F32),
        pltpu.VMEM((tm, D_POOL), F32),
        pltpu.VMEM((tm, D_MODEL), BF16),
        pltpu.VMEM((tm, D_MODEL), BF16),
        pltpu.VMEM((tm, D_MODEL), F32),
        pltpu.VMEM((tm, D_MODEL), F32),
        pltpu.VMEM((tm, D_MODEL), F32),
        pltpu.VMEM((tm, D_FF), BF16),
    ]
    return pl.pallas_call(
        functools.partial(_prompt_kernel, nt=nt),
        grid=(n_blocks + 1,),
        in_specs=in_specs,
        out_specs=out_specs,
        out_shape=out_shape,
        scratch_shapes=scratch,
        compiler_params=_params(1),
        name="prompt_layer",
    )(x, qk, vo, q_dec, k_dec, v_dec, wa, wb, *consts)


def _sample_mixer_kernel(x_ref, sconv_ref, spool_ref,
                         wa_ref, xq_ref, gmat_ref, conv_w_ref, vec_ref,
                         x1_ref, q_ref, conv_new_ref, pool_new_ref,
                         hbuf, yslab, dslab, ybuf, dbuf, mixbuf, *, dec_seq):
    wa, vec = _views(wa_ref, WA_FIELDS), _views(vec_ref, VEC_FIELDS)
    w_in_ref, w_out_ref = wa["w_in"], wa["w_out"]
    b_in_ref, conv_b_ref, gn_g_ref, gn_b_ref = vec["b_in"], vec["conv_b"], vec["gn_g"], vec["gn_b"]
    ln1_g_ref, ln1_b_ref = vec["ln1_g"], vec["ln1_b"]
    gb = SAMPLE_MIX_BATCH
    ts = dec_seq
    x = x_ref[...]
    h = _bdot(x, _unpack(w_in_ref[...])) + b_in_ref[...]
    u = h[:, 0:D_CONV] * jax.nn.sigmoid(h[:, D_CONV:2 * D_CONV])
    for s in range(CONV_SLABS):
        hbuf[s] = u[:, s * LANES:(s + 1) * LANES]
    for s in range(POOL_SLABS):
        hbuf[CONV_SLABS + s] = h[:, 2 * D_CONV + s * LANES:2 * D_CONV + (s + 1) * LANES]

    def step_rows(slab, t):
        return hbuf[slab, pl.ds(t, gb, stride=ts), :]

    for s in range(CONV_SLABS):
        sl = slice(s * LANES, (s + 1) * LANES)
        new = [step_rows(s, t) for t in range(ts)]
        ext = lambda j: sconv_ref[j, :, sl] if j < CONV_CTX else new[j - CONV_CTX]
        for j in range(CONV_CTX - ts):
            conv_new_ref[j, :, sl] = sconv_ref[j + ts, :, sl]
        for t in range(ts):
            conv_new_ref[CONV_CTX - ts + t, :, sl] = new[t]
            acc = jnp.broadcast_to(conv_b_ref[:, sl], (gb, LANES))
            for k in range(CONV_WIDTH):
                acc = acc + ext(t + k) * conv_w_ref[k, :, sl]
            yslab[s, pl.ds(t, gb, stride=ts), :] = acc

    for g, w in enumerate(POOL_WINDOWS):
        sl = slice(g * LANES, (g + 1) * LANES)
        new = [step_rows(CONV_SLABS + g, t) for t in range(ts)]
        ext = lambda j: spool_ref[j, :, sl] if j < POOL_CTX else new[j - POOL_CTX]
        for j in range(POOL_CTX - ts):
            pool_new_ref[j, :, sl] = spool_ref[j + ts, :, sl]
        for t in range(ts):
            pool_new_ref[POOL_CTX - ts + t, :, sl] = new[t]
            ws = new[t]
            for j in range(1, w):
                ws = ws + ext(POOL_CTX + t - j)
            cnt = float(min(PAST_LEN + t + 1, w))
            dslab[g, pl.ds(t, gb, stride=ts), :] = ws / cnt - new[t]

    for s in range(CONV_SLABS):
        ybuf[:, s * LANES:(s + 1) * LANES] = yslab[s]
    for s in range(POOL_SLABS):
        dbuf[:, s * LANES:(s + 1) * LANES] = dslab[s]

    _mixer_acts(ybuf, dbuf, mixbuf, gmat_ref, gn_g_ref, gn_b_ref)
    x1 = _layer_norm(_mixer_out(mixbuf, x, w_out_ref), ln1_g_ref[...], ln1_b_ref[...])
    x1_ref[...] = x1
    q_ref[...] = _bdot(x1, _unpack(xq_ref[...])) * ATTN_SCALE


def _sample_mixer_call(xs, sconv_tm, spool_tm, wa, wkv, consts, dec_seq):
    xq_block = dict(WKV_BLOCKS)["xq"]
    n_tok = xs.shape[0]
    nb = n_tok // dec_seq
    gb = SAMPLE_MIX_BATCH
    rows = gb * dec_seq
    row_spec = pl.BlockSpec((rows, D_MODEL), lambda i: (i, 0))
    conv_spec = pl.BlockSpec((CONV_CTX, gb, D_CONV), lambda i: (0, i, 0))
    pool_spec = pl.BlockSpec((POOL_CTX, gb, D_POOL), lambda i: (0, i, 0))
    out_shape = [
        jax.ShapeDtypeStruct((n_tok, D_MODEL), F32),
        jax.ShapeDtypeStruct((n_tok, D_MODEL), F32),
        jax.ShapeDtypeStruct((CONV_CTX, nb, D_CONV), F32),
        jax.ShapeDtypeStruct((POOL_CTX, nb, D_POOL), F32),
    ]
    scratch = [
        pltpu.VMEM((CONV_SLABS + POOL_SLABS, rows, LANES), F32),
        pltpu.VMEM((CONV_SLABS, rows, LANES), F32),
        pltpu.VMEM((POOL_SLABS, rows, LANES), F32),
        pltpu.VMEM((rows, D_CONV), F32),
        pltpu.VMEM((rows, D_POOL), F32),
        pltpu.VMEM((rows, D_MODEL), BF16),
    ]
    return pl.pallas_call(
        functools.partial(_sample_mixer_kernel, dec_seq=dec_seq),
        grid=(nb // gb,),
        in_specs=[row_spec, conv_spec, pool_spec, _const_spec(wa.shape),
                  pl.BlockSpec((D_MODEL // 2, D_MODEL), lambda i: (0, xq_block), pipeline_mode=pl.Buffered(1))]
        + [_const_spec(w.shape) for w in consts],
        out_specs=[row_spec, row_spec, conv_spec, pool_spec],
        out_shape=out_shape,
        scratch_shapes=scratch,
        compiler_params=_params(1),
        name="sample_mixer",
    )(xs, sconv_tm, spool_tm, wa, wkv, *consts)


def _load_head(ref, b, hd):
    parts = [ref[b, pl.ds(c * X_HEADS + hd, N_MEM, stride=KV_ROW_STRIDE), :] for c in range(HEAD_CHUNKS)]
    return jnp.concatenate(parts, axis=1).astype(BF16)


def _sample_attn_probs(q_ref, k_ref):
    q8 = q_ref[...]
    q = jnp.concatenate([q8] * (SAMPLE_ATTN_ROWS // q8.shape[0]), axis=0).astype(BF16)
    s = jnp.concatenate(
        [lax.dot_general(q[:, hd * X_HEAD_DIM:(hd + 1) * X_HEAD_DIM], _load_head(k_ref, b, hd),
                         (((1,), (1,)), ((), ())), preferred_element_type=F32)
         for b in range(k_ref.shape[0]) for hd in range(X_HEADS)], axis=0)
    e = jnp.exp(s - jnp.max(s, axis=-1, keepdims=True))
    return (e / jnp.sum(e, axis=-1, keepdims=True)).astype(BF16)


def _sample_attn_values(a, v_ref, o_ref, ts):
    for b in range(v_ref.shape[0]):
        for hd in range(X_HEADS):
            i = b * X_HEADS + hd
            o = jnp.dot(a[i * SAMPLE_ATTN_ROWS:(i + 1) * SAMPLE_ATTN_ROWS, :], _load_head(v_ref, b, hd),
                        preferred_element_type=F32)
            o_ref[b * ts:(b + 1) * ts, hd * X_HEAD_DIM:(hd + 1) * X_HEAD_DIM] = o[b * ts:(b + 1) * ts, :]


def _sample_tail_kernel(x1_ref, o_ref, xo_ref, w1_ref, b1_ref, w2_ref, vec_ref, y_ref, x2buf, x2bf, fbuf):
    vec = _views(vec_ref, VEC_FIELDS)
    ln2_g_ref, ln2_b_ref, b2_ref, ln3_g_ref, ln3_b_ref = (
        vec["ln2_g"], vec["ln2_b"], vec["b2"], vec["ln3_g"], vec["ln3_b"])
    c = pl.program_id(0)

    @pl.when(c == 0)
    def _():
        attn = _bdot(o_ref[...], _unpack(xo_ref[...]))
        x2 = _layer_norm(DN_ALPHA * x1_ref[...] + attn, ln2_g_ref[...], ln2_b_ref[...])
        x2buf[...] = x2
        x2bf[...] = x2.astype(BF16)
        fbuf[...] = jnp.broadcast_to(b2_ref[...], fbuf.shape)

    hdn = jnp.maximum(jnp.dot(x2bf[...], _unpack(w1_ref[...]), preferred_element_type=F32) + b1_ref[...], 0.0)
    fbuf[...] += _bdot(hdn * hdn, _unpack(w2_ref[...]))

    @pl.when(c == pl.num_programs(0) - 1)
    def _():
        y_ref[...] = _layer_norm(DN_ALPHA * x2buf[...] + fbuf[...], ln3_g_ref[...], ln3_b_ref[...])


def _sample_tail_call(x1, o, wb, w2_p, vecs):
    n_tok = x1.shape[0]
    fc = TAIL_FF_CHUNK
    blocks = dict(WB_BLOCKS)
    assert fc == D_MODEL and VEC_FIELDS[0] == ("b1", D_FF) and blocks["w1"] == 0
    in_specs = [
        _const_spec(x1.shape), _const_spec(o.shape),
        pl.BlockSpec((D_MODEL // 2, D_MODEL), lambda c: (0, blocks["xo"]), pipeline_mode=pl.Buffered(1)),
        pl.BlockSpec((D_MODEL // 2, fc), lambda c: (0, c)),
        pl.BlockSpec((1, fc), lambda c: (0, c)),
        pl.BlockSpec((fc // 2, D_MODEL), lambda c: (c, 0)),
        _const_spec(vecs.shape),
    ]
    return pl.pallas_call(
        _sample_tail_kernel,
        grid=(D_FF // fc,),
        in_specs=in_specs,
        out_specs=pl.BlockSpec((n_tok, D_MODEL), lambda c: (0, 0)),
        out_shape=jax.ShapeDtypeStruct((n_tok, D_MODEL), F32),
        scratch_shapes=[pltpu.VMEM((n_tok, D_MODEL), F32), pltpu.VMEM((n_tok, D_MODEL), BF16),
                        pltpu.VMEM((n_tok, D_MODEL), F32)],
        compiler_params=_params(1),
        name="sample_tail",
    )(x1, o, wb, wb, vecs, w2_p, vecs)


def _group_indicator():
    gid = jnp.arange(GN_HALF) // CONV_HEAD_DIM
    return (gid[:, None] == gid[None, :]).astype(F32)


def _to_head_split_view(a):
    lead = a.shape[:-3]
    n = len(lead)
    a = a.reshape(lead + (N_MEM, X_HEADS, HEAD_CHUNKS, LANES))
    a = a.transpose(tuple(range(n)) + (n, n + 2, n + 1, n + 3))
    return a.reshape(lead + (KV_ROWS, LANES))


def _from_head_split_view(a):
    lead = a.shape[:-2]
    n = len(lead)
    a = a.reshape(lead + (N_MEM, HEAD_CHUNKS, X_HEADS, LANES))
    a = a.transpose(tuple(range(n)) + (n, n + 2, n + 1, n + 3))
    return a.reshape(lead + (N_MEM, X_HEADS, X_HEAD_DIM))


def kernel(x_prompt, x_sample, mem_prompt, cache_mem_k, cache_mem_v, state_conv, state_pool, w_in, b_in, conv_w, conv_b, gn_g, gn_b, pool_w, pool_scale, w_out, ln1_g, ln1_b, xq_w, xk_w, xv_w, xo_w, ln2_g, ln2_b, w1, b1, w2, b2, ln3_g, ln3_b):
    assert w_in.shape[0] == DEPTH == 1
    n_prompt, seq, _ = x_prompt.shape
    n_dec, dec_seq, _ = x_sample.shape
    assert seq % PROMPT_BLOCK == 0 and n_dec % SAMPLE_MIX_BATCH == 0

    named = dict(w_in=w_in, w_out=w_out, xq=xq_w, xo=xo_w, w1=w1, b1=b1, b_in=b_in, conv_b=conv_b, gn_g=gn_g,
                 gn_b=gn_b, ln1_g=ln1_g, ln1_b=ln1_b, ln2_g=ln2_g, ln2_b=ln2_b, b2=b2,
                 ln3_g=ln3_g, ln3_b=ln3_b)
    named.update(xk=xk_w, xv=xv_w)
    wa, wb, wkv, gmat_p, w2_p, vecs = _cast_call(
        [[named[name][0] for name, _ in WA_FIELDS], [named[name][0] for name, _ in WB_FIELDS],
         [named[name][0] for name, _ in WKV_FIELDS], [_group_indicator()], [w2[0]]],
        [named[name] for name, _ in VEC_FIELDS],
        pool_w[0], pool_scale, fold_index=[name for name, _ in WA_FIELDS].index("w_out"))
    conv_taps = conv_w.transpose(1, 0, 2)

    xs = x_sample.reshape(n_dec * dec_seq, D_MODEL)
    x1_s, q_s, conv_s, pool_s = _sample_mixer_call(
        xs, state_conv[0].transpose(1, 0, 2), state_pool[0].transpose(1, 0, 2),
        wa, wkv, (gmat_p, conv_taps, vecs), dec_seq)

    k_p, v_p, qk_p, vo_p = _kv_proj_call(mem_prompt, wkv, wb)
    y_p, conv_p, pool_p, o_s = _prompt_call(
        x_prompt, qk_p, vo_p, q_s, _to_head_split_view(cache_mem_k[0]), _to_head_split_view(cache_mem_v[0]),
        dec_seq, wa, wb, (w2_p, gmat_p, conv_taps, vecs))

    y_s = _sample_tail_call(x1_s, o_s, wb, w2_p, vecs)

    return (y_p, y_s.reshape(n_dec, dec_seq, D_MODEL),
            _from_head_split_view(k_p)[None], _from_head_split_view(v_p)[None],
            conv_p[None], conv_s.transpose(1, 0, 2)[None], pool_p[None], pool_s.transpose(1, 0, 2)[None])
```

```python
import functools

import jax
import jax.numpy as jnp
from jax import lax
from jax.experimental import pallas as pl
from jax.experimental.pallas import tpu as pltpu

F32 = jnp.float32
BF16 = jnp.bfloat16
U32 = jnp.uint32

LANES = 128
SUBLANES = 8
D_MODEL = 1024
D_CONV = 512
D_POOL = 512
D_IN = 2 * D_CONV + D_POOL
CONV_WIDTH = 31
CONV_CTX = CONV_WIDTH - 1
CONV_HEADS = 8
CONV_HEAD_DIM = D_CONV // CONV_HEADS
POOL_WINDOWS = (2, 4, 8, 16)
POOL_GROUP = D_POOL // len(POOL_WINDOWS)
POOL_CTX = max(POOL_WINDOWS) - 1
N_MEM = 256
X_HEADS = 4
X_HEAD_DIM = D_MODEL // X_HEADS
HEAD_CHUNKS = X_HEAD_DIM // LANES
KV_ROWS = N_MEM * X_HEADS * HEAD_CHUNKS
KV_ROW_STRIDE = X_HEADS * HEAD_CHUNKS
D_FF = 4 * D_MODEL
LN_EPS = 1e-5
DEPTH = 1
PAST_LEN = 16384
DN_ALPHA = (2.0 * DEPTH) ** 0.25
ATTN_SCALE = X_HEAD_DIM ** -0.5

assert POOL_GROUP == LANES and X_HEAD_DIM % LANES == 0

CONV_SLABS = D_CONV // LANES
POOL_SLABS = D_POOL // LANES
CONV_PAD = 32
POOL_PAD = 16
GN_HALF = 256

PROMPT_BLOCK = 256
CONV_ROWS = 32
SAMPLE_MIX_BATCH = 64
SAMPLE_ATTN_ROWS = 16
TAIL_FF_CHUNK = 1024
CAST_STEPS = 8
VMEM_LIMIT = 56 * 1024 * 1024


def _unpack(w_u32):
    return pltpu.bitcast(w_u32, BF16)


def _pack(w_bf16):
    return pltpu.bitcast(w_bf16, U32)


def _bdot(a, w):
    return jnp.dot(a.astype(BF16), w, preferred_element_type=F32)


def _layer_norm(x, g, b):
    mu = jnp.mean(x, axis=-1, keepdims=True)
    d = x - mu
    var = jnp.mean(d * d, axis=-1, keepdims=True)
    return d * lax.rsqrt(var + LN_EPS) * g + b


def _group_norm_swish(y, gmat, gn_g, gn_b):
    inv = 1.0 / CONV_HEAD_DIM
    yc = y - _bdot(y, gmat) * inv
    rest = _bdot(yc, gmat) * inv
    var = jnp.maximum(_bdot(yc * yc, gmat) * inv - rest * rest, 0.0)
    d = yc - rest
    n = d * lax.rsqrt(var + LN_EPS) * gn_g + gn_b
    return n * jax.nn.sigmoid(n)


def _mixer_acts(ybuf, dbuf, mixbuf, gmat_ref, gn_g_ref, gn_b_ref):
    mixbuf[:, D_CONV:] = dbuf[...].astype(BF16)
    m = ybuf.shape[0]
    n_half = D_CONV // GN_HALF
    cols = [slice(j * GN_HALF, (j + 1) * GN_HALF) for j in range(n_half)]
    o = _group_norm_swish(
        jnp.concatenate([ybuf[:, sl] for sl in cols], axis=0), _unpack(gmat_ref[...]),
        jnp.concatenate([jnp.broadcast_to(gn_g_ref[:, sl], (m, GN_HALF)) for sl in cols], axis=0),
        jnp.concatenate([jnp.broadcast_to(gn_b_ref[:, sl], (m, GN_HALF)) for sl in cols], axis=0))
    for j, sl in enumerate(cols):
        mixbuf[:, sl] = o[j * m:(j + 1) * m, :].astype(BF16)


def _mixer_out(mixbuf, x, w_out_ref):
    return DN_ALPHA * x + jnp.dot(mixbuf[...], _unpack(w_out_ref[...]), preferred_element_type=F32)


def _const_spec(shape):
    nd = len(shape)
    return pl.BlockSpec(shape, lambda *_: (0,) * nd, pipeline_mode=pl.Buffered(1))


def _params(n_axes):
    return pltpu.CompilerParams(dimension_semantics=("arbitrary",) * n_axes, vmem_limit_bytes=VMEM_LIMIT)


VEC_FIELDS = (("b1", D_FF), ("b_in", D_IN), ("conv_b", D_CONV), ("gn_g", D_CONV), ("gn_b", D_CONV),
              ("ln1_g", D_MODEL), ("ln1_b", D_MODEL), ("ln2_g", D_MODEL),
              ("ln2_b", D_MODEL), ("b2", D_MODEL), ("ln3_g", D_MODEL), ("ln3_b", D_MODEL))
VEC_LEN = sum(n for _, n in VEC_FIELDS)
WA_FIELDS = (("w_in", D_IN), ("w_out", D_MODEL))
WKV_FIELDS = (("xk", D_MODEL), ("xv", D_MODEL), ("xq", D_MODEL))
WB_FIELDS = (("w1", D_FF), ("xo", D_MODEL))
WKV_BLOCKS = (("xk", 0), ("xv", 1), ("xq", 2))
WB_BLOCKS = (("w1", 0), ("xo", D_FF // D_MODEL))


def _views(ref, fields):
    out, off = {}, 0
    for name, n in fields:
        out[name] = ref.at[:, off:off + n]
        off += n
    return out


def _cast_kernel(pool_w_ref, pool_scale_ref, *refs, widths, n_vecs, fold_index):
    n_mats = sum(len(g) for g in widths)
    srcs, vec_srcs = refs[:n_mats], refs[n_mats:n_mats + n_vecs]
    dsts, vec_dst = refs[n_mats + n_vecs:-1], refs[-1]
    step = pl.program_id(0)
    i = 0
    for dst, group in zip(dsts, widths):
        off = 0
        for n in group:
            w = srcs[i][...]
            if i == fold_index:
                folded = jnp.dot(pool_w_ref[0] * pool_scale_ref[...], w, precision=lax.Precision.HIGHEST,
                                 preferred_element_type=F32)
                w = jnp.where(step >= D_CONV // POOL_GROUP, folded, w)
            dst[:, off:off + n] = _pack(w.astype(BF16))
            off += n
            i += 1
    @pl.when(pl.program_id(0) == 0)
    def _():
        off = 0
        for src in vec_srcs:
            n = src.shape[1]
            vec_dst[:, off:off + n] = src[...]
            off += n


def _cast_call(groups, vectors, pool_w, pool_scale, fold_index):
    n_groups = pool_w.shape[0]
    first = D_CONV // POOL_GROUP
    assert D_MODEL // CAST_STEPS == POOL_GROUP and first + n_groups == CAST_STEPS
    grp = lambda i: jnp.clip(i - first, 0, n_groups - 1)
    in_specs = [pl.BlockSpec((1, POOL_GROUP, POOL_GROUP), lambda i: (grp(i), 0, 0)),
                pl.BlockSpec((1, POOL_GROUP), lambda i: (0, grp(i)))]
    out_specs, out_shape, widths = [], [], []
    for group in groups:
        k = group[0].shape[0]
        rows = k // CAST_STEPS
        assert rows * CAST_STEPS == k and rows % 16 == 0 and all(w.shape[0] == k for w in group)
        assert all(w.shape[1] % LANES == 0 for w in group)
        widths.append(tuple(w.shape[1] for w in group))
        total = sum(widths[-1])
        in_specs += [pl.BlockSpec((rows, w.shape[1]), lambda i: (i, 0)) for w in group]
        out_specs.append(pl.BlockSpec((rows // 2, total), lambda i: (i, 0)))
        out_shape.append(jax.ShapeDtypeStruct((k // 2, total), U32))
    assert all(v.shape[1] % LANES == 0 for v in vectors)
    vec_len = sum(v.shape[1] for v in vectors)
    in_specs += [pl.BlockSpec(v.shape, lambda i: (0, 0)) for v in vectors]
    out_specs.append(pl.BlockSpec((1, vec_len), lambda i: (0, 0)))
    out_shape.append(jax.ShapeDtypeStruct((1, vec_len), F32))
    return pl.pallas_call(
        functools.partial(_cast_kernel, widths=tuple(widths), n_vecs=len(vectors), fold_index=fold_index),
        grid=(CAST_STEPS,), in_specs=in_specs, out_specs=out_specs, out_shape=out_shape,
        compiler_params=_params(1), name="cast_weights",
    )(pool_w, pool_scale, *[w for group in groups for w in group], *vectors)


def _store_head_split(dst_ref, val):
    for hd in range(X_HEADS):
        for c in range(HEAD_CHUNKS):
            col = hd * X_HEAD_DIM + c * LANES
            dst_ref[0, pl.ds(c * X_HEADS + hd, N_MEM, stride=KV_ROW_STRIDE), :] = val[:, col:col + LANES]


def _kv_proj_kernel(mem_ref, wkv_ref, xo_ref, k_ref, v_ref, qk_ref, vo_ref):
    wkv = _views(wkv_ref, WKV_FIELDS)
    m = mem_ref[0].astype(BF16)
    k = jnp.dot(m, _unpack(wkv["xk"][...]), preferred_element_type=F32)
    v = jnp.dot(m, _unpack(wkv["xv"][...]), preferred_element_type=F32)
    _store_head_split(k_ref, k)
    _store_head_split(v_ref, v)
    xq = _unpack(wkv["xq"][...])
    xo = _unpack(xo_ref[...])
    for hd in range(X_HEADS):
        sl = slice(hd * X_HEAD_DIM, (hd + 1) * X_HEAD_DIM)
        qk = jnp.dot(xq[:, sl], k[:, sl].T.astype(BF16), preferred_element_type=F32) * ATTN_SCALE
        qk_ref[0, :, hd * N_MEM:(hd + 1) * N_MEM] = _pack(qk.astype(BF16))
        vo = jnp.dot(v[:, sl].astype(BF16), xo[sl, :], preferred_element_type=F32)
        vo_ref[0, hd * N_MEM // 2:(hd + 1) * N_MEM // 2, :] = _pack(vo.astype(BF16))


def _kv_proj_call(mem, wkv, wb):
    nb = mem.shape[0]
    blk = lambda s: pl.BlockSpec((1,) + s, lambda b: (b, 0, 0))
    xo_block = dict(WB_BLOCKS)["xo"]
    return pl.pallas_call(
        _kv_proj_kernel,
        grid=(nb,),
        in_specs=[blk((N_MEM, D_MODEL)), _const_spec(wkv.shape),
                  pl.BlockSpec((D_MODEL // 2, D_MODEL), lambda b: (0, xo_block), pipeline_mode=pl.Buffered(1))],
        out_specs=[blk((KV_ROWS, LANES)), blk((KV_ROWS, LANES)),
                   blk((D_MODEL // 2, X_HEADS * N_MEM)), blk((X_HEADS * N_MEM // 2, D_MODEL))],
        out_shape=[
            jax.ShapeDtypeStruct((nb, KV_ROWS, LANES), F32),
            jax.ShapeDtypeStruct((nb, KV_ROWS, LANES), F32),
            jax.ShapeDtypeStruct((nb, D_MODEL // 2, X_HEADS * N_MEM), U32),
            jax.ShapeDtypeStruct((nb, X_HEADS * N_MEM // 2, D_MODEL), U32),
        ],
        compiler_params=_params(1),
        name="kv_proj",
    )(mem, wkv, wb)


def _prompt_kernel(x_ref, qk_ref, vo_ref, qs_ref, kc_ref, vc_ref,
                   wa_ref, w1_ref, w2_ref, gmat_ref, conv_w_ref, vec_ref,
                   y_ref, conv_new_ref, pool_new_ref, os_ref,
                   ubuf, pbuf, ybuf, dbuf, mixbuf, obuf, x1buf, x1prev, x2buf, hbuf, *, nt):
    wa, vec = _views(wa_ref, WA_FIELDS), _views(vec_ref, VEC_FIELDS)
    w_in_ref, w_out_ref = wa["w_in"], wa["w_out"]
    b_in_ref, conv_b_ref, gn_g_ref, gn_b_ref = vec["b_in"], vec["conv_b"], vec["gn_g"], vec["gn_b"]
    ln1_g_ref, ln1_b_ref = vec["ln1_g"], vec["ln1_b"]
    ln2_g_ref, ln2_b_ref, b1_ref, b2_ref = vec["ln2_g"], vec["ln2_b"], vec["b1"], vec["b2"]
    ln3_g_ref, ln3_b_ref = vec["ln3_g"], vec["ln3_b"]
    tm = PROMPT_BLOCK
    step = pl.program_id(0)
    n_blocks = pl.num_programs(0) - 1
    t = jnp.minimum(step, n_blocks - 1) % nt

    @pl.when(t == 0)
    def _():
        ubuf[:, 0:CONV_PAD, :] = jnp.zeros((CONV_SLABS, CONV_PAD, LANES), F32)
        pbuf[:, 0:POOL_PAD, :] = jnp.zeros((POOL_SLABS, POOL_PAD, LANES), F32)

    @pl.when(t > 0)
    def _():
        ubuf[:, 0:CONV_PAD, :] = ubuf[:, tm:tm + CONV_PAD, :]
        pbuf[:, 0:POOL_PAD, :] = pbuf[:, tm:tm + POOL_PAD, :]

    val = {}

    def mix_in():
        val["x"] = x_ref[0]
        h = _bdot(val["x"], _unpack(w_in_ref[...])) + b_in_ref[...]
        u = h[:, 0:D_CONV] * jax.nn.sigmoid(h[:, D_CONV:2 * D_CONV])
        for s in range(CONV_SLABS):
            ubuf[s, CONV_PAD:CONV_PAD + tm, :] = u[:, s * LANES:(s + 1) * LANES]
        for s in range(POOL_SLABS):
            pbuf[s, POOL_PAD:POOL_PAD + tm, :] = h[:, 2 * D_CONV + s * LANES:2 * D_CONV + (s + 1) * LANES]

    def att_scores():
        x1prev[...] = _layer_norm(x1buf[...], ln1_g_ref[...], ln1_b_ref[...])
        val["scores"] = _bdot(x1prev[...], _unpack(qk_ref[0]))

    def dec_probs():
        val["probs_dec"] = _sample_attn_probs(qs_ref, kc_ref)

    def att_values():
        for hd in range(X_HEADS):
            cols = slice(hd * N_MEM, (hd + 1) * N_MEM)
            s = val["scores"][:, cols]
            e = jnp.exp(s - jnp.max(s, axis=-1, keepdims=True))
            obuf[:, cols] = (e / jnp.sum(e, axis=-1, keepdims=True)).astype(BF16)

    def mix_windows():
        base = CONV_PAD - CONV_CTX
        for s in range(CONV_SLABS):
            sl = slice(s * LANES, (s + 1) * LANES)
            for c in range(tm // CONV_ROWS):
                r0 = c * CONV_ROWS
                acc = jnp.broadcast_to(conv_b_ref[:, sl], (CONV_ROWS, LANES))
                for k in range(CONV_WIDTH):
                    acc = acc + ubuf[s, base + r0 + k:base + r0 + k + CONV_ROWS, :] * conv_w_ref[k, :, sl]
                ybuf[r0:r0 + CONV_ROWS, sl] = acc

        pos = t * tm + lax.broadcasted_iota(jnp.int32, (tm, POOL_GROUP), 0)
        for gi, w in enumerate(POOL_WINDOWS):
            cur = pbuf[gi, POOL_PAD:POOL_PAD + tm, :]
            ws = cur
            for j in range(1, w):
                ws = ws + pbuf[gi, POOL_PAD - j:POOL_PAD - j + tm, :]
            cnt = jnp.minimum(pos + 1, w).astype(F32)
            dbuf[:, gi * POOL_GROUP:(gi + 1) * POOL_GROUP] = ws / cnt - cur

    def att_out():
        val["attn"] = jnp.dot(obuf[...], _unpack(vo_ref[0]), preferred_element_type=F32)

    def dec_values():
        _sample_attn_values(val["probs_dec"], vc_ref, os_ref, os_ref.shape[0] // vc_ref.shape[0])

    def ffn_norm():
        x2buf[...] = _layer_norm(DN_ALPHA * x1prev[...] + val["attn"], ln2_g_ref[...], ln2_b_ref[...])

    def ffn_up():
        hdn = jnp.maximum(_bdot(x2buf[...], _unpack(w1_ref[...])) + b1_ref[...], 0.0)
        hbuf[...] = (hdn * hdn).astype(BF16)

    def mix_acts():
        _mixer_acts(ybuf, dbuf, mixbuf, gmat_ref, gn_g_ref, gn_b_ref)

    def ffn_down():
        val["f"] = jnp.dot(hbuf[...], _unpack(w2_ref[...]), preferred_element_type=F32) + b2_ref[...]

    def mix_out():
        x1buf[...] = _mixer_out(mixbuf, val["x"], w_out_ref)

    def out_norm():
        y_ref[0] = _layer_norm(DN_ALPHA * x2buf[...] + val["f"], ln3_g_ref[...], ln3_b_ref[...])

    @pl.when(step == 0)
    def _():
        x1buf[...] = jnp.zeros((tm, D_MODEL), F32)

    step_order = (mix_in, att_scores, dec_probs, att_values, mix_windows, att_out, dec_values, ffn_norm, ffn_up,
                  mix_acts, ffn_down, mix_out, out_norm)
    for piece in step_order:
        piece()

    @pl.when(jnp.logical_and(t == nt - 1, step < n_blocks))
    def _():
        for s in range(CONV_SLABS):
            conv_new_ref[0, :, s * LANES:(s + 1) * LANES] = ubuf[s, CONV_PAD + tm - CONV_CTX:CONV_PAD + tm, :]
        for s in range(POOL_SLABS):
            pool_new_ref[0, :, s * LANES:(s + 1) * LANES] = pbuf[s, POOL_PAD + tm - POOL_CTX:POOL_PAD + tm, :]


def _prompt_call(x, qk, vo, q_dec, k_dec, v_dec, dec_seq, wa, wb, consts):
    assert dict(WB_BLOCKS)["w1"] == 0
    nb, seq, _ = x.shape
    tm = PROMPT_BLOCK
    nt = seq // tm
    n_blocks = nb * nt
    n_dec = k_dec.shape[0]
    dec_per_step = -(-n_dec // n_blocks)
    dec_steps = n_dec // dec_per_step
    dec_rows = dec_per_step * dec_seq
    assert dec_steps * dec_per_step == n_dec and dec_steps <= n_blocks + 1
    assert dec_rows % SUBLANES == 0 and SAMPLE_ATTN_ROWS % dec_rows == 0
    cur = lambda g: jnp.minimum(g, n_blocks - 1)
    prev = lambda g: jnp.maximum(g - 1, 0)
    dec = lambda g: jnp.minimum(g, dec_steps - 1)
    in_specs = [
        pl.BlockSpec((1, tm, D_MODEL), lambda g: (cur(g) // nt, cur(g) % nt, 0)),
        pl.BlockSpec((1,) + qk.shape[1:], lambda g: (prev(g) // nt, 0, 0)),
        pl.BlockSpec((1,) + vo.shape[1:], lambda g: (prev(g) // nt, 0, 0)),
        pl.BlockSpec((dec_rows, D_MODEL), lambda g: (dec(g), 0)),
        pl.BlockSpec((dec_per_step, KV_ROWS, LANES), lambda g: (dec(g), 0, 0)),
        pl.BlockSpec((dec_per_step, KV_ROWS, LANES), lambda g: (dec(g), 0, 0)),
        _const_spec(wa.shape),
        pl.BlockSpec((D_MODEL // 2, D_FF), lambda g: (0, 0), pipeline_mode=pl.Buffered(1)),
    ] + [_const_spec(w.shape) for w in consts]
    out_specs = [
        pl.BlockSpec((1, tm, D_MODEL), lambda g: (prev(g) // nt, prev(g) % nt, 0)),
        pl.BlockSpec((1, CONV_CTX, D_CONV), lambda g: (cur(g) // nt, 0, 0)),
        pl.BlockSpec((1, POOL_CTX, D_POOL), lambda g: (cur(g) // nt, 0, 0)),
        pl.BlockSpec((dec_rows, D_MODEL), lambda g: (dec(g), 0)),
    ]
    out_shape = [
        jax.ShapeDtypeStruct((nb, seq, D_MODEL), F32),
        jax.ShapeDtypeStruct((nb, CONV_CTX, D_CONV), F32),
        jax.ShapeDtypeStruct((nb, POOL_CTX, D_POOL), F32),
        jax.ShapeDtypeStruct(q_dec.shape, F32),
    ]
    scratch = [
        pltpu.VMEM((CONV_SLABS, CONV_PAD + tm, LANES), F32),
        pltpu.VMEM((POOL_SLABS, POOL_PAD + tm, LANES), F32),
        pltpu.VMEM((tm, D_CONV), F32),
        pltpu.VMEM((tm, D_POOL), F32),
        pltpu.VMEM((tm, D_MODEL), BF16),
        pltpu.VMEM((tm, D_MODEL), BF16),
        pltpu.VMEM((tm, D_MODEL), F32),
        pltpu.VMEM((tm, D_MODEL), F32),
        pltpu.VMEM((tm, D_MODEL), F32),
        pltpu.VMEM((tm, D_FF), BF16),
    ]
    return pl.pallas_call(
        functools.partial(_prompt_kernel, nt=nt),
        grid=(n_blocks + 1,),
        in_specs=in_specs,
        out_specs=out_specs,
        out_shape=out_shape,
        scratch_shapes=scratch,
        compiler_params=_params(1),
        name="prompt_layer",
    )(x, qk, vo, q_dec, k_dec, v_dec, wa, wb, *consts)


def _sample_mixer_kernel(x_ref, sconv_ref, spool_ref,
                         wa_ref, xq_ref, gmat_ref, conv_w_ref, vec_ref,
                         x1_ref, q_ref, conv_new_ref, pool_new_ref,
                         hbuf, yslab, dslab, ybuf, dbuf, mixbuf, *, dec_seq):
    wa, vec = _views(wa_ref, WA_FIELDS), _views(vec_ref, VEC_FIELDS)
    w_in_ref, w_out_ref = wa["w_in"], wa["w_out"]
    b_in_ref, conv_b_ref, gn_g_ref, gn_b_ref = vec["b_in"], vec["conv_b"], vec["gn_g"], vec["gn_b"]
    ln1_g_ref, ln1_b_ref = vec["ln1_g"], vec["ln1_b"]
    gb = SAMPLE_MIX_BATCH
    ts = dec_seq
    x = x_ref[...]
    h = _bdot(x, _unpack(w_in_ref[...])) + b_in_ref[...]
    u = h[:, 0:D_CONV] * jax.nn.sigmoid(h[:, D_CONV:2 * D_CONV])
    for s in range(CONV_SLABS):
        hbuf[s] = u[:, s * LANES:(s + 1) * LANES]
    for s in range(POOL_SLABS):
        hbuf[CONV_SLABS + s] = h[:, 2 * D_CONV + s * LANES:2 * D_CONV + (s + 1) * LANES]

    def step_rows(slab, t):
        return hbuf[slab, pl.ds(t, gb, stride=ts), :]

    for s in range(CONV_SLABS):
        sl = slice(s * LANES, (s + 1) * LANES)
        new = [step_rows(s, t) for t in range(ts)]
        ext = lambda j: sconv_ref[j, :, sl] if j < CONV_CTX else new[j - CONV_CTX]
        for j in range(CONV_CTX - ts):
            conv_new_ref[j, :, sl] = sconv_ref[j + ts, :, sl]
        for t in range(ts):
            conv_new_ref[CONV_CTX - ts + t, :, sl] = new[t]
            acc = jnp.broadcast_to(conv_b_ref[:, sl], (gb, LANES))
            for k in range(CONV_WIDTH):
                acc = acc + ext(t + k) * conv_w_ref[k, :, sl]
            yslab[s, pl.ds(t, gb, stride=ts), :] = acc

    for g, w in enumerate(POOL_WINDOWS):
        sl = slice(g * LANES, (g + 1) * LANES)
        new = [step_rows(CONV_SLABS + g, t) for t in range(ts)]
        ext = lambda j: spool_ref[j, :, sl] if j < POOL_CTX else new[j - POOL_CTX]
        for j in range(POOL_CTX - ts):
            pool_new_ref[j, :, sl] = spool_ref[j + ts, :, sl]
        for t in range(ts):
            pool_new_ref[POOL_CTX - ts + t, :, sl] = new[t]
            ws = new[t]
            for j in range(1, w):
                ws = ws + ext(POOL_CTX + t - j)
            cnt = float(min(PAST_LEN + t + 1, w))
            dslab[g, pl.ds(t, gb, stride=ts), :] = ws / cnt - new[t]

    for s in range(CONV_SLABS):
        ybuf[:, s * LANES:(s + 1) * LANES] = yslab[s]
    for s in range(POOL_SLABS):
        dbuf[:, s * LANES:(s + 1) * LANES] = dslab[s]

    _mixer_acts(ybuf, dbuf, mixbuf, gmat_ref, gn_g_ref, gn_b_ref)
    x1 = _layer_norm(_mixer_out(mixbuf, x, w_out_ref), ln1_g_ref[...], ln1_b_ref[...])
    x1_ref[...] = x1
    q_ref[...] = _bdot(x1, _unpack(xq_ref[...])) * ATTN_SCALE


def _sample_mixer_call(xs, sconv_tm, spool_tm, wa, wkv, consts, dec_seq):
    xq_block = dict(WKV_BLOCKS)["xq"]
    n_tok = xs.shape[0]
    nb = n_tok // dec_seq
    gb = SAMPLE_MIX_BATCH
    rows = gb * dec_seq
    row_spec = pl.BlockSpec((rows, D_MODEL), lambda i: (i, 0))
    conv_spec = pl.BlockSpec((CONV_CTX, gb, D_CONV), lambda i: (0, i, 0))
    pool_spec = pl.BlockSpec((POOL_CTX, gb, D_POOL), lambda i: (0, i, 0))
    out_shape = [
        jax.ShapeDtypeStruct((n_tok, D_MODEL), F32),
        jax.ShapeDtypeStruct((n_tok, D_MODEL), F32),
        jax.ShapeDtypeStruct((CONV_CTX, nb, D_CONV), F32),
        jax.ShapeDtypeStruct((POOL_CTX, nb, D_POOL), F32),
    ]
    scratch = [
        pltpu.VMEM((CONV_SLABS + POOL_SLABS, rows, LANES), F32),
        pltpu.VMEM((CONV_SLABS, rows, LANES), F32),
        pltpu.VMEM((POOL_SLABS, rows, LANES), F32),
        pltpu.VMEM((rows, D_CONV), F32),
        pltpu.VMEM((rows, D_POOL), F32),
        pltpu.VMEM((rows, D_MODEL), BF16),
    ]
    return pl.pallas_call(
        functools.partial(_sample_mixer_kernel, dec_seq=dec_seq),
        grid=(nb // gb,),
        in_specs=[row_spec, conv_spec, pool_spec, _const_spec(wa.shape),
                  pl.BlockSpec((D_MODEL // 2, D_MODEL), lambda i: (0, xq_block), pipeline_mode=pl.Buffered(1))]
        + [_const_spec(w.shape) for w in consts],
        out_specs=[row_spec, row_spec, conv_spec, pool_spec],
        out_shape=out_shape,
        scratch_shapes=scratch,
        compiler_params=_params(1),
        name="sample_mixer",
    )(xs, sconv_tm, spool_tm, wa, wkv, *consts)


def _load_head(ref, b, hd):
    parts = [ref[b, pl.ds(c * X_HEADS + hd, N_MEM, stride=KV_ROW_STRIDE), :] for c in range(HEAD_CHUNKS)]
    return jnp.concatenate(parts, axis=1).astype(BF16)


def _sample_attn_probs(q_ref, k_ref):
    q8 = q_ref[...]
    q = jnp.concatenate([q8] * (SAMPLE_ATTN_ROWS // q8.shape[0]), axis=0).astype(BF16)
    s = jnp.concatenate(
        [lax.dot_general(q[:, hd * X_HEAD_DIM:(hd + 1) * X_HEAD_DIM], _load_head(k_ref, b, hd),
                         (((1,), (1,)), ((), ())), preferred_element_type=F32)
         for b in range(k_ref.shape[0]) for hd in range(X_HEADS)], axis=0)
    e = jnp.exp(s - jnp.max(s, axis=-1, keepdims=True))
    return (e / jnp.sum(e, axis=-1, keepdims=True)).astype(BF16)


def _sample_attn_values(a, v_ref, o_ref, ts):
    for b in range(v_ref.shape[0]):
        for hd in range(X_HEADS):
            i = b * X_HEADS + hd
            o = jnp.dot(a[i * SAMPLE_ATTN_ROWS:(i + 1) * SAMPLE_ATTN_ROWS, :], _load_head(v_ref, b, hd),
                        preferred_element_type=F32)
            o_ref[b * ts:(b + 1) * ts, hd * X_HEAD_DIM:(hd + 1) * X_HEAD_DIM] = o[b * ts:(b + 1) * ts, :]


def _sample_tail_kernel(x1_ref, o_ref, xo_ref, w1_ref, b1_ref, w2_ref, vec_ref, y_ref, x2buf, x2bf, fbuf):
    vec = _views(vec_ref, VEC_FIELDS)
    ln2_g_ref, ln2_b_ref, b2_ref, ln3_g_ref, ln3_b_ref = (
        vec["ln2_g"], vec["ln2_b"], vec["b2"], vec["ln3_g"], vec["ln3_b"])
    c = pl.program_id(0)

    @pl.when(c == 0)
    def _():
        attn = _bdot(o_ref[...], _unpack(xo_ref[...]))
        x2 = _layer_norm(DN_ALPHA * x1_ref[...] + attn, ln2_g_ref[...], ln2_b_ref[...])
        x2buf[...] = x2
        x2bf[...] = x2.astype(BF16)
        fbuf[...] = jnp.broadcast_to(b2_ref[...], fbuf.shape)

    hdn = jnp.maximum(jnp.dot(x2bf[...], _unpack(w1_ref[...]), preferred_element_type=F32) + b1_ref[...], 0.0)
    fbuf[...] += _bdot(hdn * hdn, _unpack(w2_ref[...]))

    @pl.when(c == pl.num_programs(0) - 1)
    def _():
        y_ref[...] = _layer_norm(DN_ALPHA * x2buf[...] + fbuf[...], ln3_g_ref[...], ln3_b_ref[...])


def _sample_tail_call(x1, o, wb, w2_p, vecs):
    n_tok = x1.shape[0]
    fc = TAIL_FF_CHUNK
    blocks = dict(WB_BLOCKS)
    assert fc == D_MODEL and VEC_FIELDS[0] == ("b1", D_FF) and blocks["w1"] == 0
    in_specs = [
        _const_spec(x1.shape), _const_spec(o.shape),
        pl.BlockSpec((D_MODEL // 2, D_MODEL), lambda c: (0, blocks["xo"]), pipeline_mode=pl.Buffered(1)),
        pl.BlockSpec((D_MODEL // 2, fc), lambda c: (0, c)),
        pl.BlockSpec((1, fc), lambda c: (0, c)),
        pl.BlockSpec((fc // 2, D_MODEL), lambda c: (c, 0)),
        _const_spec(vecs.shape),
    ]
    return pl.pallas_call(
        _sample_tail_kernel,
        grid=(D_FF // fc,),
        in_specs=in_specs,
        out_specs=pl.BlockSpec((n_tok, D_MODEL), lambda c: (0, 0)),
        out_shape=jax.ShapeDtypeStruct((n_tok, D_MODEL), F32),
        scratch_shapes=[pltpu.VMEM((n_tok, D_MODEL), F32), pltpu.VMEM((n_tok, D_MODEL), BF16),
                        pltpu.VMEM((n_tok, D_MODEL), F32)],
        compiler_params=_params(1),
        name="sample_tail",
    )(x1, o, wb, wb, vecs, w2_p, vecs)


def _group_indicator():
    gid = jnp.arange(GN_HALF) // CONV_HEAD_DIM
    return (gid[:, None] == gid[None, :]).astype(F32)


def _to_head_split_view(a):
    lead = a.shape[:-3]
    n = len(lead)
    a = a.reshape(lead + (N_MEM, X_HEADS, HEAD_CHUNKS, LANES))
    a = a.transpose(tuple(range(n)) + (n, n + 2, n + 1, n + 3))
    return a.reshape(lead + (KV_ROWS, LANES))


def _from_head_split_view(a):
    lead = a.shape[:-2]
    n = len(lead)
    a = a.reshape(lead + (N_MEM, HEAD_CHUNKS, X_HEADS, LANES))
    a = a.transpose(tuple(range(n)) + (n, n + 2, n + 1, n + 3))
    return a.reshape(lead + (N_MEM, X_HEADS, X_HEAD_DIM))


def kernel(x_prompt, x_sample, mem_prompt, cache_mem_k, cache_mem_v, state_conv, state_pool, w_in, b_in, conv_w, conv_b, gn_g, gn_b, pool_w, pool_scale, w_out, ln1_g, ln1_b, xq_w, xk_w, xv_w, xo_w, ln2_g, ln2_b, w1, b1, w2, b2, ln3_g, ln3_b):
    assert w_in.shape[0] == DEPTH == 1
    n_prompt, seq, _ = x_prompt.shape
    n_dec, dec_seq, _ = x_sample.shape
    assert seq % PROMPT_BLOCK == 0 and n_dec % SAMPLE_MIX_BATCH == 0

    named = dict(w_in=w_in, w_out=w_out, xq=xq_w, xo=xo_w, w1=w1, b1=b1, b_in=b_in, conv_b=conv_b, gn_g=gn_g,
                 gn_b=gn_b, ln1_g=ln1_g, ln1_b=ln1_b, ln2_g=ln2_g, ln2_b=ln2_b, b2=b2,
                 ln3_g=ln3_g, ln3_b=ln3_b)
    named.update(xk=xk_w, xv=xv_w)
    wa, wb, wkv, gmat_p, w2_p, vecs = _cast_call(
        [[named[name][0] for name, _ in WA_FIELDS], [named[name][0] for name, _ in WB_FIELDS],
         [named[name][0] for name, _ in WKV_FIELDS], [_group_indicator()], [w2[0]]],
        [named[name] for name, _ in VEC_FIELDS],
        pool_w[0], pool_scale, fold_index=[name for name, _ in WA_FIELDS].index("w_out"))
    conv_taps = conv_w.transpose(1, 0, 2)

    xs = x_sample.reshape(n_dec * dec_seq, D_MODEL)
    x1_s, q_s, conv_s, pool_s = _sample_mixer_call(
        xs, state_conv[0].transpose(1, 0, 2), state_pool[0].transpose(1, 0, 2),
        wa, wkv, (gmat_p, conv_taps, vecs), dec_seq)

    k_p, v_p, qk_p, vo_p = _kv_proj_call(mem_prompt, wkv, wb)
    y_p, conv_p, pool_p, o_s = _prompt_call(
        x_prompt, qk_p, vo_p, q_s, _to_head_split_view(cache_mem_k[0]), _to_head_split_view(cache_mem_v[0]),
        dec_seq, wa, wb, (w2_p, gmat_p, conv_taps, vecs))

    y_s = _sample_tail_call(x1_s, o_s, wb, w2_p, vecs)

    return (y_p, y_s.reshape(n_dec, dec_seq, D_MODEL),
            _from_head_split_view(k_p)[None], _from_head_split_view(v_p)[None],
            conv_p[None], conv_s.transpose(1, 0, 2)[None], pool_p[None], pool_s.transpose(1, 0, 2)[None])
```

```python
import functools

import jax
import jax.numpy as jnp
from jax import lax
from jax.experimental import pallas as pl
from jax.experimental.pallas import tpu as pltpu

F32 = jnp.float32
BF16 = jnp.bfloat16
U32 = jnp.uint32

LANES = 128
SUBLANES = 8
D_MODEL = 1024
D_CONV = 512
D_POOL = 512
D_IN = 2 * D_CONV + D_POOL
CONV_WIDTH = 31
CONV_CTX = CONV_WIDTH - 1
CONV_HEADS = 8
CONV_HEAD_DIM = D_CONV // CONV_HEADS
POOL_WINDOWS = (2, 4, 8, 16)
POOL_GROUP = D_POOL // len(POOL_WINDOWS)
POOL_CTX = max(POOL_WINDOWS) - 1
N_MEM = 256
X_HEADS = 4
X_HEAD_DIM = D_MODEL // X_HEADS
HEAD_CHUNKS = X_HEAD_DIM // LANES
KV_ROWS = N_MEM * X_HEADS * HEAD_CHUNKS
KV_ROW_STRIDE = X_HEADS * HEAD_CHUNKS
D_FF = 4 * D_MODEL
LN_EPS = 1e-5
DEPTH = 1
PAST_LEN = 16384
DN_ALPHA = (2.0 * DEPTH) ** 0.25
ATTN_SCALE = X_HEAD_DIM ** -0.5

assert POOL_GROUP == LANES and X_HEAD_DIM % LANES == 0

CONV_SLABS = D_CONV // LANES
POOL_SLABS = D_POOL // LANES
CONV_PAD = 32
POOL_PAD = 16
GN_HALF = 256

PROMPT_BLOCK = 256
CONV_ROWS = 32
SAMPLE_MIX_BATCH = 64
SAMPLE_ATTN_ROWS = 16
TAIL_FF_CHUNK = 1024
CAST_STEPS = 8
VMEM_LIMIT = 56 * 1024 * 1024


def _unpack(w_u32):
    return pltpu.bitcast(w_u32, BF16)


def _pack(w_bf16):
    return pltpu.bitcast(w_bf16, U32)


def _bdot(a, w):
    return jnp.dot(a.astype(BF16), w, preferred_element_type=F32)


def _layer_norm(x, g, b):
    mu = jnp.mean(x, axis=-1, keepdims=True)
    d = x - mu
    var = jnp.mean(d * d, axis=-1, keepdims=True)
    return d * lax.rsqrt(var + LN_EPS) * g + b


def _group_norm_swish(y, gmat, gn_g, gn_b):
    inv = 1.0 / CONV_HEAD_DIM
    yc = y - _bdot(y, gmat) * inv
    rest = _bdot(yc, gmat) * inv
    var = jnp.maximum(_bdot(yc * yc, gmat) * inv - rest * rest, 0.0)
    d = yc - rest
    n = d * lax.rsqrt(var + LN_EPS) * gn_g + gn_b
    return n * jax.nn.sigmoid(n)


def _mixer_acts(ybuf, dbuf, mixbuf, gmat_ref, gn_g_ref, gn_b_ref):
    mixbuf[:, D_CONV:] = dbuf[...].astype(BF16)
    m = ybuf.shape[0]
    n_half = D_CONV // GN_HALF
    cols = [slice(j * GN_HALF, (j + 1) * GN_HALF) for j in range(n_half)]
    o = _group_norm_swish(
        jnp.concatenate([ybuf[:, sl] for sl in cols], axis=0), _unpack(gmat_ref[...]),
        jnp.concatenate([jnp.broadcast_to(gn_g_ref[:, sl], (m, GN_HALF)) for sl in cols], axis=0),
        jnp.concatenate([jnp.broadcast_to(gn_b_ref[:, sl], (m, GN_HALF)) for sl in cols], axis=0))
    for j, sl in enumerate(cols):
        mixbuf[:, sl] = o[j * m:(j + 1) * m, :].astype(BF16)


def _mixer_out(mixbuf, x, w_out_ref):
    return DN_ALPHA * x + jnp.dot(mixbuf[...], _unpack(w_out_ref[...]), preferred_element_type=F32)


def _const_spec(shape):
    nd = len(shape)
    return pl.BlockSpec(shape, lambda *_: (0,) * nd, pipeline_mode=pl.Buffered(1))


def _params(n_axes):
    return pltpu.CompilerParams(dimension_semantics=("arbitrary",) * n_axes, vmem_limit_bytes=VMEM_LIMIT)


VEC_FIELDS = (("b1", D_FF), ("b_in", D_IN), ("conv_b", D_CONV), ("gn_g", D_CONV), ("gn_b", D_CONV),
              ("ln1_g", D_MODEL), ("ln1_b", D_MODEL), ("ln2_g", D_MODEL),
              ("ln2_b", D_MODEL), ("b2", D_MODEL), ("ln3_g", D_MODEL), ("ln3_b", D_MODEL))
VEC_LEN = sum(n for _, n in VEC_FIELDS)
WA_FIELDS = (("w_in", D_IN), ("w_out", D_MODEL))
WKV_FIELDS = (("xk", D_MODEL), ("xv", D_MODEL), ("xq", D_MODEL))
WB_FIELDS = (("w1", D_FF), ("xo", D_MODEL))
WKV_BLOCKS = (("xk", 0), ("xv", 1), ("xq", 2))
WB_BLOCKS = (("w1", 0), ("xo", D_FF // D_MODEL))


def _views(ref, fields):
    out, off = {}, 0
    for name, n in fields:
        out[name] = ref.at[:, off:off + n]
        off += n
    return out


def _cast_kernel(pool_w_ref, pool_scale_ref, *refs, widths, n_vecs, fold_index):
    n_mats = sum(len(g) for g in widths)
    srcs, vec_srcs = refs[:n_mats], refs[n_mats:n_mats + n_vecs]
    dsts, vec_dst = refs[n_mats + n_vecs:-1], refs[-1]
    step = pl.program_id(0)
    i = 0
    for dst, group in zip(dsts, widths):
        off = 0
        for n in group:
            w = srcs[i][...]
            if i == fold_index:
                folded = jnp.dot(pool_w_ref[0] * pool_scale_ref[...], w, precision=lax.Precision.HIGHEST,
                                 preferred_element_type=F32)
                w = jnp.where(step >= D_CONV // POOL_GROUP, folded, w)
            dst[:, off:off + n] = _pack(w.astype(BF16))
            off += n
            i += 1
    @pl.when(pl.program_id(0) == 0)
    def _():
        off = 0
        for src in vec_srcs:
            n = src.shape[1]
            vec_dst[:, off:off + n] = src[...]
            off += n


def _cast_call(groups, vectors, pool_w, pool_scale, fold_index):
    n_groups = pool_w.shape[0]
    first = D_CONV // POOL_GROUP
    assert D_MODEL // CAST_STEPS == POOL_GROUP and first + n_groups == CAST_STEPS
    grp = lambda i: jnp.clip(i - first, 0, n_groups - 1)
    in_specs = [pl.BlockSpec((1, POOL_GROUP, POOL_GROUP), lambda i: (grp(i), 0, 0)),
                pl.BlockSpec((1, POOL_GROUP), lambda i: (0, grp(i)))]
    out_specs, out_shape, widths = [], [], []
    for group in groups:
        k = group[0].shape[0]
        rows = k // CAST_STEPS
        assert rows * CAST_STEPS == k and rows % 16 == 0 and all(w.shape[0] == k for w in group)
        assert all(w.shape[1] % LANES == 0 for w in group)
        widths.append(tuple(w.shape[1] for w in group))
        total = sum(widths[-1])
        in_specs += [pl.BlockSpec((rows, w.shape[1]), lambda i: (i, 0)) for w in group]
        out_specs.append(pl.BlockSpec((rows // 2, total), lambda i: (i, 0)))
        out_shape.append(jax.ShapeDtypeStruct((k // 2, total), U32))
    assert all(v.shape[1] % LANES == 0 for v in vectors)
    vec_len = sum(v.shape[1] for v in vectors)
    in_specs += [pl.BlockSpec(v.shape, lambda i: (0, 0)) for v in vectors]
    out_specs.append(pl.BlockSpec((1, vec_len), lambda i: (0, 0)))
    out_shape.append(jax.ShapeDtypeStruct((1, vec_len), F32))
    return pl.pallas_call(
        functools.partial(_cast_kernel, widths=tuple(widths), n_vecs=len(vectors), fold_index=fold_index),
        grid=(CAST_STEPS,), in_specs=in_specs, out_specs=out_specs, out_shape=out_shape,
        compiler_params=_params(1), name="cast_weights",
    )(pool_w, pool_scale, *[w for group in groups for w in group], *vectors)


def _store_head_split(dst_ref, val):
    for hd in range(X_HEADS):
        for c in range(HEAD_CHUNKS):
            col = hd * X_HEAD_DIM + c * LANES
            dst_ref[0, pl.ds(c * X_HEADS + hd, N_MEM, stride=KV_ROW_STRIDE), :] = val[:, col:col + LANES]


def _kv_proj_kernel(mem_ref, wkv_ref, xo_ref, k_ref, v_ref, qk_ref, vo_ref):
    wkv = _views(wkv_ref, WKV_FIELDS)
    m = mem_ref[0].astype(BF16)
    k = jnp.dot(m, _unpack(wkv["xk"][...]), preferred_element_type=F32)
    v = jnp.dot(m, _unpack(wkv["xv"][...]), preferred_element_type=F32)
    _store_head_split(k_ref, k)
    _store_head_split(v_ref, v)
    xq = _unpack(wkv["xq"][...])
    xo = _unpack(xo_ref[...])
    for hd in range(X_HEADS):
        sl = slice(hd * X_HEAD_DIM, (hd + 1) * X_HEAD_DIM)
        qk = jnp.dot(xq[:, sl], k[:, sl].T.astype(BF16), preferred_element_type=F32) * ATTN_SCALE
        qk_ref[0, :, hd * N_MEM:(hd + 1) * N_MEM] = _pack(qk.astype(BF16))
        vo = jnp.dot(v[:, sl].astype(BF16), xo[sl, :], preferred_element_type=F32)
        vo_ref[0, hd * N_MEM // 2:(hd + 1) * N_MEM // 2, :] = _pack(vo.astype(BF16))


def _kv_proj_call(mem, wkv, wb):
    nb = mem.shape[0]
    blk = lambda s: pl.BlockSpec((1,) + s, lambda b: (b, 0, 0))
    xo_block = dict(WB_BLOCKS)["xo"]
    return pl.pallas_call(
        _kv_proj_kernel,
        grid=(nb,),
        in_specs=[blk((N_MEM, D_MODEL)), _const_spec(wkv.shape),
                  pl.BlockSpec((D_MODEL // 2, D_MODEL), lambda b: (0, xo_block), pipeline_mode=pl.Buffered(1))],
        out_specs=[blk((KV_ROWS, LANES)), blk((KV_ROWS, LANES)),
                   blk((D_MODEL // 2, X_HEADS * N_MEM)), blk((X_HEADS * N_MEM // 2, D_MODEL))],
        out_shape=[
            jax.ShapeDtypeStruct((nb, KV_ROWS, LANES), F32),
            jax.ShapeDtypeStruct((nb, KV_ROWS, LANES), F32),
            jax.ShapeDtypeStruct((nb, D_MODEL // 2, X_HEADS * N_MEM), U32),
            jax.ShapeDtypeStruct((nb, X_HEADS * N_MEM // 2, D_MODEL), U32),
        ],
        compiler_params=_params(1),
        name="kv_proj",
    )(mem, wkv, wb)


PIPE_STAGES = 3

def _prompt_kernel(x_ref, qk_ref, vo_ref, qs_ref, kc_ref, vc_ref,
                   wa_ref, w1_ref, w2_ref, gmat_ref, conv_w_ref, vec_ref,
                   y_ref, conv_new_ref, pool_new_ref, os_ref,
                   ubuf, pbuf, ybuf, dbuf, mixbuf, obuf, x1buf, x1prev, x2buf, hbuf, z2buf, *, nt):
    wa, vec = _views(wa_ref, WA_FIELDS), _views(vec_ref, VEC_FIELDS)
    w_in_ref, w_out_ref = wa["w_in"], wa["w_out"]
    b_in_ref, conv_b_ref, gn_g_ref, gn_b_ref = vec["b_in"], vec["conv_b"], vec["gn_g"], vec["gn_b"]
    ln1_g_ref, ln1_b_ref = vec["ln1_g"], vec["ln1_b"]
    ln2_g_ref, ln2_b_ref, b1_ref, b2_ref = vec["ln2_g"], vec["ln2_b"], vec["b1"], vec["b2"]
    ln3_g_ref, ln3_b_ref = vec["ln3_g"], vec["ln3_b"]
    tm = PROMPT_BLOCK
    step = pl.program_id(0)
    n_blocks = pl.num_programs(0) - (PIPE_STAGES - 1)
    t = jnp.minimum(step, n_blocks - 1) % nt

    @pl.when(t == 0)
    def _():
        ubuf[:, 0:CONV_PAD, :] = jnp.zeros((CONV_SLABS, CONV_PAD, LANES), F32)
        pbuf[:, 0:POOL_PAD, :] = jnp.zeros((POOL_SLABS, POOL_PAD, LANES), F32)

    @pl.when(t > 0)
    def _():
        ubuf[:, 0:CONV_PAD, :] = ubuf[:, tm:tm + CONV_PAD, :]
        pbuf[:, 0:POOL_PAD, :] = pbuf[:, tm:tm + POOL_PAD, :]

    val = {}

    def mix_in():
        val["x"] = x_ref[0]
        h = _bdot(val["x"], _unpack(w_in_ref[...])) + b_in_ref[...]
        u = h[:, 0:D_CONV] * jax.nn.sigmoid(h[:, D_CONV:2 * D_CONV])
        for s in range(CONV_SLABS):
            ubuf[s, CONV_PAD:CONV_PAD + tm, :] = u[:, s * LANES:(s + 1) * LANES]
        for s in range(POOL_SLABS):
            pbuf[s, POOL_PAD:POOL_PAD + tm, :] = h[:, 2 * D_CONV + s * LANES:2 * D_CONV + (s + 1) * LANES]

    def att_scores():
        x1prev[...] = _layer_norm(x1buf[...], ln1_g_ref[...], ln1_b_ref[...])
        x2buf[...] = _layer_norm(z2buf[...], ln2_g_ref[...], ln2_b_ref[...])
        val["scores"] = _bdot(x1prev[...], _unpack(qk_ref[0]))

    def dec_probs():
        val["probs_dec"] = _sample_attn_probs(qs_ref, kc_ref)

    def att_values():
        for hd in range(X_HEADS):
            cols = slice(hd * N_MEM, (hd + 1) * N_MEM)
            s = val["scores"][:, cols]
            e = jnp.exp(s - jnp.max(s, axis=-1, keepdims=True))
            obuf[:, cols] = (e / jnp.sum(e, axis=-1, keepdims=True)).astype(BF16)

    def mix_windows():
        base = CONV_PAD - CONV_CTX
        for s in range(CONV_SLABS):
            sl = slice(s * LANES, (s + 1) * LANES)
            for c in range(tm // CONV_ROWS):
                r0 = c * CONV_ROWS
                acc = jnp.broadcast_to(conv_b_ref[:, sl], (CONV_ROWS, LANES))
                for k in range(CONV_WIDTH):
                    acc = acc + ubuf[s, base + r0 + k:base + r0 + k + CONV_ROWS, :] * conv_w_ref[k, :, sl]
                ybuf[r0:r0 + CONV_ROWS, sl] = acc

        pos = t * tm + lax.broadcasted_iota(jnp.int32, (tm, POOL_GROUP), 0)
        for gi, w in enumerate(POOL_WINDOWS):
            cur = pbuf[gi, POOL_PAD:POOL_PAD + tm, :]
            ws = cur
            for j in range(1, w):
                ws = ws + pbuf[gi, POOL_PAD - j:POOL_PAD - j + tm, :]
            cnt = jnp.minimum(pos + 1, w).astype(F32)
            dbuf[:, gi * POOL_GROUP:(gi + 1) * POOL_GROUP] = ws / cnt - cur

    def att_out():
        val["attn"] = jnp.dot(obuf[...], _unpack(vo_ref[0]), preferred_element_type=F32)

    def dec_values():
        _sample_attn_values(val["probs_dec"], vc_ref, os_ref, os_ref.shape[0] // vc_ref.shape[0])

    def att_residual():
        z2buf[...] = DN_ALPHA * x1prev[...] + val["attn"]

    def ffn_up():
        hdn = jnp.maximum(_bdot(x2buf[...], _unpack(w1_ref[...])) + b1_ref[...], 0.0)
        hbuf[...] = (hdn * hdn).astype(BF16)

    def mix_acts():
        _mixer_acts(ybuf, dbuf, mixbuf, gmat_ref, gn_g_ref, gn_b_ref)

    def ffn_down():
        val["f"] = jnp.dot(hbuf[...], _unpack(w2_ref[...]), preferred_element_type=F32) + b2_ref[...]

    def mix_out():
        x1buf[...] = _mixer_out(mixbuf, val["x"], w_out_ref)

    def out_norm():
        y_ref[0] = _layer_norm(DN_ALPHA * x2buf[...] + val["f"], ln3_g_ref[...], ln3_b_ref[...])

    @pl.when(step == 0)
    def _():
        x1buf[...] = jnp.zeros((tm, D_MODEL), F32)
        z2buf[...] = jnp.zeros((tm, D_MODEL), F32)

    step_order = (mix_in, att_scores, dec_probs, att_values, mix_windows, att_out, dec_values, att_residual, ffn_up,
                  mix_acts, ffn_down, mix_out, out_norm)
    for piece in step_order:
        piece()

    @pl.when(jnp.logical_and(t == nt - 1, step < n_blocks))
    def _():
        for s in range(CONV_SLABS):
            conv_new_ref[0, :, s * LANES:(s + 1) * LANES] = ubuf[s, CONV_PAD + tm - CONV_CTX:CONV_PAD + tm, :]
        for s in range(POOL_SLABS):
            pool_new_ref[0, :, s * LANES:(s + 1) * LANES] = pbuf[s, POOL_PAD + tm - POOL_CTX:POOL_PAD + tm, :]


def _prompt_call(x, qk, vo, q_dec, k_dec, v_dec, dec_seq, wa, wb, consts):
    assert dict(WB_BLOCKS)["w1"] == 0
    nb, seq, _ = x.shape
    tm = PROMPT_BLOCK
    nt = seq // tm
    n_blocks = nb * nt
    n_dec = k_dec.shape[0]
    dec_per_step = -(-n_dec // n_blocks)
    dec_steps = n_dec // dec_per_step
    dec_rows = dec_per_step * dec_seq
    n_steps = n_blocks + PIPE_STAGES - 1
    assert dec_steps * dec_per_step == n_dec and dec_steps <= n_steps
    assert dec_rows % SUBLANES == 0 and SAMPLE_ATTN_ROWS % dec_rows == 0
    clamp = lambda i: jnp.clip(i, 0, n_blocks - 1)
    cur = lambda g: clamp(g)
    att = lambda g: clamp(g - 1)
    prev = lambda g: clamp(g - 2)
    dec = lambda g: jnp.minimum(g, dec_steps - 1)
    in_specs = [
        pl.BlockSpec((1, tm, D_MODEL), lambda g: (cur(g) // nt, cur(g) % nt, 0)),
        pl.BlockSpec((1,) + qk.shape[1:], lambda g: (att(g) // nt, 0, 0)),
        pl.BlockSpec((1,) + vo.shape[1:], lambda g: (att(g) // nt, 0, 0)),
        pl.BlockSpec((dec_rows, D_MODEL), lambda g: (dec(g), 0)),
        pl.BlockSpec((dec_per_step, KV_ROWS, LANES), lambda g: (dec(g), 0, 0)),
        pl.BlockSpec((dec_per_step, KV_ROWS, LANES), lambda g: (dec(g), 0, 0)),
        _const_spec(wa.shape),
        pl.BlockSpec((D_MODEL // 2, D_FF), lambda g: (0, 0), pipeline_mode=pl.Buffered(1)),
    ] + [_const_spec(w.shape) for w in consts]
    out_specs = [
        pl.BlockSpec((1, tm, D_MODEL), lambda g: (prev(g) // nt, prev(g) % nt, 0)),
        pl.BlockSpec((1, CONV_CTX, D_CONV), lambda g: (cur(g) // nt, 0, 0)),
        pl.BlockSpec((1, POOL_CTX, D_POOL), lambda g: (cur(g) // nt, 0, 0)),
        pl.BlockSpec((dec_rows, D_MODEL), lambda g: (dec(g), 0)),
    ]
    out_shape = [
        jax.ShapeDtypeStruct((nb, seq, D_MODEL), F32),
        jax.ShapeDtypeStruct((nb, CONV_CTX, D_CONV), F32),
        jax.ShapeDtypeStruct((nb, POOL_CTX, D_POOL), F32),
        jax.ShapeDtypeStruct(q_dec.shape, F32),
    ]
    scratch = [
        pltpu.VMEM((CONV_SLABS, CONV_PAD + tm, LANES), F32),
        pltpu.VMEM((POOL_SLABS, POOL_PAD + tm, LANES), F32),
        pltpu.VMEM((tm, D_CONV), F32),
        pltpu.VMEM((tm, D_POOL), F32),
        pltpu.VMEM((tm, D_MODEL), BF16),
        pltpu.VMEM((tm, D_MODEL), BF16),
        pltpu.VMEM((tm, D_MODEL), F32),
        pltpu.VMEM((tm, D_MODEL), F32),
        pltpu.VMEM((tm, D_MODEL), F32),
        pltpu.VMEM((tm, D_FF), BF16),
        pltpu.VMEM((tm, D_MODEL), F32),
    ]
    return pl.pallas_call(
        functools.partial(_prompt_kernel, nt=nt),
        grid=(n_steps,),
        in_specs=in_specs,
        out_specs=out_specs,
        out_shape=out_shape,
        scratch_shapes=scratch,
        compiler_params=_params(1),
        name="prompt_layer",
    )(x, qk, vo, q_dec, k_dec, v_dec, wa, wb, *consts)


def _sample_mixer_kernel(x_ref, sconv_ref, spool_ref,
                         wa_ref, xq_ref, gmat_ref, conv_w_ref, vec_ref,
                         x1_ref, q_ref, conv_new_ref, pool_new_ref,
                         hbuf, yslab, dslab, ybuf, dbuf, mixbuf, *, dec_seq):
    wa, vec = _views(wa_ref, WA_FIELDS), _views(vec_ref, VEC_FIELDS)
    w_in_ref, w_out_ref = wa["w_in"], wa["w_out"]
    b_in_ref, conv_b_ref, gn_g_ref, gn_b_ref = vec["b_in"], vec["conv_b"], vec["gn_g"], vec["gn_b"]
    ln1_g_ref, ln1_b_ref = vec["ln1_g"], vec["ln1_b"]
    gb = SAMPLE_MIX_BATCH
    ts = dec_seq
    x = x_ref[...]
    h = _bdot(x, _unpack(w_in_ref[...])) + b_in_ref[...]
    u = h[:, 0:D_CONV] * jax.nn.sigmoid(h[:, D_CONV:2 * D_CONV])
    for s in range(CONV_SLABS):
        hbuf[s] = u[:, s * LANES:(s + 1) * LANES]
    for s in range(POOL_SLABS):
        hbuf[CONV_SLABS + s] = h[:, 2 * D_CONV + s * LANES:2 * D_CONV + (s + 1) * LANES]

    def step_rows(slab, t):
        return hbuf[slab, pl.ds(t, gb, stride=ts), :]

    for s in range(CONV_SLABS):
        sl = slice(s * LANES, (s + 1) * LANES)
        new = [step_rows(s, t) for t in range(ts)]
        ext = lambda j: sconv_ref[j, :, sl] if j < CONV_CTX else new[j - CONV_CTX]
        for j in range(CONV_CTX - ts):
            conv_new_ref[j, :, sl] = sconv_ref[j + ts, :, sl]
        for t in range(ts):
            conv_new_ref[CONV_CTX - ts + t, :, sl] = new[t]
            acc = jnp.broadcast_to(conv_b_ref[:, sl], (gb, LANES))
            for k in range(CONV_WIDTH):
                acc = acc + ext(t + k) * conv_w_ref[k, :, sl]
            yslab[s, pl.ds(t, gb, stride=ts), :] = acc

    for g, w in enumerate(POOL_WINDOWS):
        sl = slice(g * LANES, (g + 1) * LANES)
        new = [step_rows(CONV_SLABS + g, t) for t in range(ts)]
        ext = lambda j: spool_ref[j, :, sl] if j < POOL_CTX else new[j - POOL_CTX]
        for j in range(POOL_CTX - ts):
            pool_new_ref[j, :, sl] = spool_ref[j + ts, :, sl]
        for t in range(ts):
            pool_new_ref[POOL_CTX - ts + t, :, sl] = new[t]
            ws = new[t]
            for j in range(1, w):
                ws = ws + ext(POOL_CTX + t - j)
            cnt = float(min(PAST_LEN + t + 1, w))
            dslab[g, pl.ds(t, gb, stride=ts), :] = ws / cnt - new[t]

    for s in range(CONV_SLABS):
        ybuf[:, s * LANES:(s + 1) * LANES] = yslab[s]
    for s in range(POOL_SLABS):
        dbuf[:, s * LANES:(s + 1) * LANES] = dslab[s]

    _mixer_acts(ybuf, dbuf, mixbuf, gmat_ref, gn_g_ref, gn_b_ref)
    x1 = _layer_norm(_mixer_out(mixbuf, x, w_out_ref), ln1_g_ref[...], ln1_b_ref[...])
    x1_ref[...] = x1
    q_ref[...] = _bdot(x1, _unpack(xq_ref[...])) * ATTN_SCALE


def _sample_mixer_call(xs, sconv_tm, spool_tm, wa, wkv, consts, dec_seq):
    xq_block = dict(WKV_BLOCKS)["xq"]
    n_tok = xs.shape[0]
    nb = n_tok // dec_seq
    gb = SAMPLE_MIX_BATCH
    rows = gb * dec_seq
    row_spec = pl.BlockSpec((rows, D_MODEL), lambda i: (i, 0))
    conv_spec = pl.BlockSpec((CONV_CTX, gb, D_CONV), lambda i: (0, i, 0))
    pool_spec = pl.BlockSpec((POOL_CTX, gb, D_POOL), lambda i: (0, i, 0))
    out_shape = [
        jax.ShapeDtypeStruct((n_tok, D_MODEL), F32),
        jax.ShapeDtypeStruct((n_tok, D_MODEL), F32),
        jax.ShapeDtypeStruct((CONV_CTX, nb, D_CONV), F32),
        jax.ShapeDtypeStruct((POOL_CTX, nb, D_POOL), F32),
    ]
    scratch = [
        pltpu.VMEM((CONV_SLABS + POOL_SLABS, rows, LANES), F32),
        pltpu.VMEM((CONV_SLABS, rows, LANES), F32),
        pltpu.VMEM((POOL_SLABS, rows, LANES), F32),
        pltpu.VMEM((rows, D_CONV), F32),
        pltpu.VMEM((rows, D_POOL), F32),
        pltpu.VMEM((rows, D_MODEL), BF16),
    ]
    return pl.pallas_call(
        functools.partial(_sample_mixer_kernel, dec_seq=dec_seq),
        grid=(nb // gb,),
        in_specs=[row_spec, conv_spec, pool_spec, _const_spec(wa.shape),
                  pl.BlockSpec((D_MODEL // 2, D_MODEL), lambda i: (0, xq_block), pipeline_mode=pl.Buffered(1))]
        + [_const_spec(w.shape) for w in consts],
        out_specs=[row_spec, row_spec, conv_spec, pool_spec],
        out_shape=out_shape,
        scratch_shapes=scratch,
        compiler_params=_params(1),
        name="sample_mixer",
    )(xs, sconv_tm, spool_tm, wa, wkv, *consts)


def _load_head(ref, b, hd):
    parts = [ref[b, pl.ds(c * X_HEADS + hd, N_MEM, stride=KV_ROW_STRIDE), :] for c in range(HEAD_CHUNKS)]
    return jnp.concatenate(parts, axis=1).astype(BF16)


def _sample_attn_probs(q_ref, k_ref):
    q8 = q_ref[...]
    q = jnp.concatenate([q8] * (SAMPLE_ATTN_ROWS // q8.shape[0]), axis=0).astype(BF16)
    s = jnp.concatenate(
        [lax.dot_general(q[:, hd * X_HEAD_DIM:(hd + 1) * X_HEAD_DIM], _load_head(k_ref, b, hd),
                         (((1,), (1,)), ((), ())), preferred_element_type=F32)
         for b in range(k_ref.shape[0]) for hd in range(X_HEADS)], axis=0)
    e = jnp.exp(s - jnp.max(s, axis=-1, keepdims=True))
    return (e / jnp.sum(e, axis=-1, keepdims=True)).astype(BF16)


def _sample_attn_values(a, v_ref, o_ref, ts):
    for b in range(v_ref.shape[0]):
        for hd in range(X_HEADS):
            i = b * X_HEADS + hd
            o = jnp.dot(a[i * SAMPLE_ATTN_ROWS:(i + 1) * SAMPLE_ATTN_ROWS, :], _load_head(v_ref, b, hd),
                        preferred_element_type=F32)
            o_ref[b * ts:(b + 1) * ts, hd * X_HEAD_DIM:(hd + 1) * X_HEAD_DIM] = o[b * ts:(b + 1) * ts, :]


def _sample_tail_kernel(x1_ref, o_ref, xo_ref, w1_ref, b1_ref, w2_ref, vec_ref, y_ref, x2buf, x2bf, fbuf):
    vec = _views(vec_ref, VEC_FIELDS)
    ln2_g_ref, ln2_b_ref, b2_ref, ln3_g_ref, ln3_b_ref = (
        vec["ln2_g"], vec["ln2_b"], vec["b2"], vec["ln3_g"], vec["ln3_b"])
    c = pl.program_id(0)

    @pl.when(c == 0)
    def _():
        attn = _bdot(o_ref[...], _unpack(xo_ref[...]))
        x2 = _layer_norm(DN_ALPHA * x1_ref[...] + attn, ln2_g_ref[...], ln2_b_ref[...])
        x2buf[...] = x2
        x2bf[...] = x2.astype(BF16)
        fbuf[...] = jnp.broadcast_to(b2_ref[...], fbuf.shape)

    hdn = jnp.maximum(jnp.dot(x2bf[...], _unpack(w1_ref[...]), preferred_element_type=F32) + b1_ref[...], 0.0)
    fbuf[...] += _bdot(hdn * hdn, _unpack(w2_ref[...]))

    @pl.when(c == pl.num_programs(0) - 1)
    def _():
        y_ref[...] = _layer_norm(DN_ALPHA * x2buf[...] + fbuf[...], ln3_g_ref[...], ln3_b_ref[...])


def _sample_tail_call(x1, o, wb, w2_p, vecs):
    n_tok = x1.shape[0]
    fc = TAIL_FF_CHUNK
    blocks = dict(WB_BLOCKS)
    assert fc == D_MODEL and VEC_FIELDS[0] == ("b1", D_FF) and blocks["w1"] == 0
    in_specs = [
        _const_spec(x1.shape), _const_spec(o.shape),
        pl.BlockSpec((D_MODEL // 2, D_MODEL), lambda c: (0, blocks["xo"]), pipeline_mode=pl.Buffered(1)),
        pl.BlockSpec((D_MODEL // 2, fc), lambda c: (0, c)),
        pl.BlockSpec((1, fc), lambda c: (0, c)),
        pl.BlockSpec((fc // 2, D_MODEL), lambda c: (c, 0)),
        _const_spec(vecs.shape),
    ]
    return pl.pallas_call(
        _sample_tail_kernel,
        grid=(D_FF // fc,),
        in_specs=in_specs,
        out_specs=pl.BlockSpec((n_tok, D_MODEL), lambda c: (0, 0)),
        out_shape=jax.ShapeDtypeStruct((n_tok, D_MODEL), F32),
        scratch_shapes=[pltpu.VMEM((n_tok, D_MODEL), F32), pltpu.VMEM((n_tok, D_MODEL), BF16),
                        pltpu.VMEM((n_tok, D_MODEL), F32)],
        compiler_params=_params(1),
        name="sample_tail",
    )(x1, o, wb, wb, vecs, w2_p, vecs)


def _group_indicator():
    gid = jnp.arange(GN_HALF) // CONV_HEAD_DIM
    return (gid[:, None] == gid[None, :]).astype(F32)


def _to_head_split_view(a):
    lead = a.shape[:-3]
    n = len(lead)
    a = a.reshape(lead + (N_MEM, X_HEADS, HEAD_CHUNKS, LANES))
    a = a.transpose(tuple(range(n)) + (n, n + 2, n + 1, n + 3))
    return a.reshape(lead + (KV_ROWS, LANES))


def _from_head_split_view(a):
    lead = a.shape[:-2]
    n = len(lead)
    a = a.reshape(lead + (N_MEM, HEAD_CHUNKS, X_HEADS, LANES))
    a = a.transpose(tuple(range(n)) + (n, n + 2, n + 1, n + 3))
    return a.reshape(lead + (N_MEM, X_HEADS, X_HEAD_DIM))


def kernel(x_prompt, x_sample, mem_prompt, cache_mem_k, cache_mem_v, state_conv, state_pool, w_in, b_in, conv_w, conv_b, gn_g, gn_b, pool_w, pool_scale, w_out, ln1_g, ln1_b, xq_w, xk_w, xv_w, xo_w, ln2_g, ln2_b, w1, b1, w2, b2, ln3_g, ln3_b):
    assert w_in.shape[0] == DEPTH == 1
    n_prompt, seq, _ = x_prompt.shape
    n_dec, dec_seq, _ = x_sample.shape
    assert seq % PROMPT_BLOCK == 0 and n_dec % SAMPLE_MIX_BATCH == 0

    named = dict(w_in=w_in, w_out=w_out, xq=xq_w, xo=xo_w, w1=w1, b1=b1, b_in=b_in, conv_b=conv_b, gn_g=gn_g,
                 gn_b=gn_b, ln1_g=ln1_g, ln1_b=ln1_b, ln2_g=ln2_g, ln2_b=ln2_b, b2=b2,
                 ln3_g=ln3_g, ln3_b=ln3_b)
    named.update(xk=xk_w, xv=xv_w)
    wa, wb, wkv, gmat_p, w2_p, vecs = _cast_call(
        [[named[name][0] for name, _ in WA_FIELDS], [named[name][0] for name, _ in WB_FIELDS],
         [named[name][0] for name, _ in WKV_FIELDS], [_group_indicator()], [w2[0]]],
        [named[name] for name, _ in VEC_FIELDS],
        pool_w[0], pool_scale, fold_index=[name for name, _ in WA_FIELDS].index("w_out"))
    conv_taps = conv_w.transpose(1, 0, 2)

    xs = x_sample.reshape(n_dec * dec_seq, D_MODEL)
    x1_s, q_s, conv_s, pool_s = _sample_mixer_call(
        xs, state_conv[0].transpose(1, 0, 2), state_pool[0].transpose(1, 0, 2),
        wa, wkv, (gmat_p, conv_taps, vecs), dec_seq)

    k_p, v_p, qk_p, vo_p = _kv_proj_call(mem_prompt, wkv, wb)
    y_p, conv_p, pool_p, o_s = _prompt_call(
        x_prompt, qk_p, vo_p, q_s, _to_head_split_view(cache_mem_k[0]), _to_head_split_view(cache_mem_v[0]),
        dec_seq, wa, wb, (w2_p, gmat_p, conv_taps, vecs))

    y_s = _sample_tail_call(x1_s, o_s, wb, w2_p, vecs)

    return (y_p, y_s.reshape(n_dec, dec_seq, D_MODEL),
            _from_head_split_view(k_p)[None], _from_head_split_view(v_p)[None],
            conv_p[None], conv_s.transpose(1, 0, 2)[None], pool_p[None], pool_s.transpose(1, 0, 2)[None])
```

```python
import functools

import jax
import jax.numpy as jnp
from jax import lax
from jax.experimental import pallas as pl
from jax.experimental.pallas import tpu as pltpu

F32 = jnp.float32
BF16 = jnp.bfloat16
U32 = jnp.uint32

LANES = 128
SUBLANES = 8
D_MODEL = 1024
D_CONV = 512
D_POOL = 512
D_IN = 2 * D_CONV + D_POOL
CONV_WIDTH = 31
CONV_CTX = CONV_WIDTH - 1
CONV_HEADS = 8
CONV_HEAD_DIM = D_CONV // CONV_HEADS
POOL_WINDOWS = (2, 4, 8, 16)
POOL_GROUP = D_POOL // len(POOL_WINDOWS)
POOL_CTX = max(POOL_WINDOWS) - 1
N_MEM = 256
X_HEADS = 4
X_HEAD_DIM = D_MODEL // X_HEADS
HEAD_CHUNKS = X_HEAD_DIM // LANES
KV_ROWS = N_MEM * X_HEADS * HEAD_CHUNKS
KV_ROW_STRIDE = X_HEADS * HEAD_CHUNKS
D_FF = 4 * D_MODEL
LN_EPS = 1e-5
DEPTH = 1
PAST_LEN = 16384
DN_ALPHA = (2.0 * DEPTH) ** 0.25
ATTN_SCALE = X_HEAD_DIM ** -0.5

assert POOL_GROUP == LANES and X_HEAD_DIM % LANES == 0

CONV_SLABS = D_CONV // LANES
POOL_SLABS = D_POOL // LANES
CONV_PAD = 32
POOL_PAD = 16
GN_HALF = 256

PROMPT_BLOCK = 256
CONV_ROWS = 32
SAMPLE_MIX_BATCH = 64
SAMPLE_ATTN_ROWS = 16
TAIL_FF_CHUNK = 1024
CAST_STEPS = 8
VMEM_LIMIT = 56 * 1024 * 1024


def _unpack(w_u32):
    return pltpu.bitcast(w_u32, BF16)


def _pack(w_bf16):
    return pltpu.bitcast(w_bf16, U32)


def _bdot(a, w):
    return jnp.dot(a.astype(BF16), w, preferred_element_type=F32)


def _layer_norm(x, g, b):
    mu = jnp.mean(x, axis=-1, keepdims=True)
    d = x - mu
    var = jnp.mean(d * d, axis=-1, keepdims=True)
    return d * lax.rsqrt(var + LN_EPS) * g + b


def _group_norm_swish(y, gmat, gn_g, gn_b):
    inv = 1.0 / CONV_HEAD_DIM
    yc = y - _bdot(y, gmat) * inv
    rest = _bdot(yc, gmat) * inv
    var = jnp.maximum(_bdot(yc * yc, gmat) * inv - rest * rest, 0.0)
    d = yc - rest
    n = d * lax.rsqrt(var + LN_EPS) * gn_g + gn_b
    return n * jax.nn.sigmoid(n)


def _mixer_acts(ybuf, dbuf, mixbuf, gmat_ref, gn_g_ref, gn_b_ref):
    mixbuf[:, D_CONV:] = dbuf[...].astype(BF16)
    m = ybuf.shape[0]
    n_half = D_CONV // GN_HALF
    cols = [slice(j * GN_HALF, (j + 1) * GN_HALF) for j in range(n_half)]
    o = _group_norm_swish(
        jnp.concatenate([ybuf[:, sl] for sl in cols], axis=0), _unpack(gmat_ref[...]),
        jnp.concatenate([jnp.broadcast_to(gn_g_ref[:, sl], (m, GN_HALF)) for sl in cols], axis=0),
        jnp.concatenate([jnp.broadcast_to(gn_b_ref[:, sl], (m, GN_HALF)) for sl in cols], axis=0))
    for j, sl in enumerate(cols):
        mixbuf[:, sl] = o[j * m:(j + 1) * m, :].astype(BF16)


def _mixer_out(mixbuf, x, w_out_ref):
    return DN_ALPHA * x + jnp.dot(mixbuf[...], _unpack(w_out_ref[...]), preferred_element_type=F32)


def _const_spec(shape):
    nd = len(shape)
    return pl.BlockSpec(shape, lambda *_: (0,) * nd, pipeline_mode=pl.Buffered(1))


def _params(n_axes):
    return pltpu.CompilerParams(dimension_semantics=("arbitrary",) * n_axes, vmem_limit_bytes=VMEM_LIMIT)


VEC_FIELDS = (("b1", D_FF), ("b_in", D_IN), ("conv_b", D_CONV), ("gn_g", D_CONV), ("gn_b", D_CONV),
              ("ln1_g", D_MODEL), ("ln1_b", D_MODEL), ("ln2_g", D_MODEL),
              ("ln2_b", D_MODEL), ("b2", D_MODEL), ("ln3_g", D_MODEL), ("ln3_b", D_MODEL))
VEC_LEN = sum(n for _, n in VEC_FIELDS)
WA_FIELDS = (("w_in", D_IN), ("w_out", D_MODEL))
WKV_FIELDS = (("xk", D_MODEL), ("xv", D_MODEL), ("xq", D_MODEL))
WB_FIELDS = (("w1", D_FF), ("xo", D_MODEL))
WKV_BLOCKS = (("xk", 0), ("xv", 1), ("xq", 2))
WB_BLOCKS = (("w1", 0), ("xo", D_FF // D_MODEL))


def _views(ref, fields):
    out, off = {}, 0
    for name, n in fields:
        out[name] = ref.at[:, off:off + n]
        off += n
    return out


def _cast_kernel(pool_w_ref, pool_scale_ref, *refs, widths, n_vecs, fold_index):
    n_mats = sum(len(g) for g in widths)
    srcs, vec_srcs = refs[:n_mats], refs[n_mats:n_mats + n_vecs]
    dsts, vec_dst = refs[n_mats + n_vecs:-1], refs[-1]
    step = pl.program_id(0)
    i = 0
    for dst, group in zip(dsts, widths):
        off = 0
        for n in group:
            w = srcs[i][...]
            if i == fold_index:
                folded = jnp.dot(pool_w_ref[0] * pool_scale_ref[...], w, precision=lax.Precision.HIGHEST,
                                 preferred_element_type=F32)
                w = jnp.where(step >= D_CONV // POOL_GROUP, folded, w)
            dst[:, off:off + n] = _pack(w.astype(BF16))
            off += n
            i += 1
    @pl.when(pl.program_id(0) == 0)
    def _():
        off = 0
        for src in vec_srcs:
            n = src.shape[1]
            vec_dst[:, off:off + n] = src[...]
            off += n


def _cast_call(groups, vectors, pool_w, pool_scale, fold_index):
    n_groups = pool_w.shape[0]
    first = D_CONV // POOL_GROUP
    assert D_MODEL // CAST_STEPS == POOL_GROUP and first + n_groups == CAST_STEPS
    grp = lambda i: jnp.clip(i - first, 0, n_groups - 1)
    in_specs = [pl.BlockSpec((1, POOL_GROUP, POOL_GROUP), lambda i: (grp(i), 0, 0)),
                pl.BlockSpec((1, POOL_GROUP), lambda i: (0, grp(i)))]
    out_specs, out_shape, widths = [], [], []
    for group in groups:
        k = group[0].shape[0]
        rows = k // CAST_STEPS
        assert rows * CAST_STEPS == k and rows % 16 == 0 and all(w.shape[0] == k for w in group)
        assert all(w.shape[1] % LANES == 0 for w in group)
        widths.append(tuple(w.shape[1] for w in group))
        total = sum(widths[-1])
        in_specs += [pl.BlockSpec((rows, w.shape[1]), lambda i: (i, 0)) for w in group]
        out_specs.append(pl.BlockSpec((rows // 2, total), lambda i: (i, 0)))
        out_shape.append(jax.ShapeDtypeStruct((k // 2, total), U32))
    assert all(v.shape[1] % LANES == 0 for v in vectors)
    vec_len = sum(v.shape[1] for v in vectors)
    in_specs += [pl.BlockSpec(v.shape, lambda i: (0, 0)) for v in vectors]
    out_specs.append(pl.BlockSpec((1, vec_len), lambda i: (0, 0)))
    out_shape.append(jax.ShapeDtypeStruct((1, vec_len), F32))
    return pl.pallas_call(
        functools.partial(_cast_kernel, widths=tuple(widths), n_vecs=len(vectors), fold_index=fold_index),
        grid=(CAST_STEPS,), in_specs=in_specs, out_specs=out_specs, out_shape=out_shape,
        compiler_params=_params(1), name="cast_weights",
    )(pool_w, pool_scale, *[w for group in groups for w in group], *vectors)


def _store_head_split(dst_ref, val):
    for hd in range(X_HEADS):
        for c in range(HEAD_CHUNKS):
            col = hd * X_HEAD_DIM + c * LANES
            dst_ref[0, pl.ds(c * X_HEADS + hd, N_MEM, stride=KV_ROW_STRIDE), :] = val[:, col:col + LANES]


def _kv_proj_kernel(mem_ref, wkv_ref, xo_ref, k_ref, v_ref, qk_ref, vo_ref):
    wkv = _views(wkv_ref, WKV_FIELDS)
    m = mem_ref[0].astype(BF16)
    k = jnp.dot(m, _unpack(wkv["xk"][...]), preferred_element_type=F32)
    v = jnp.dot(m, _unpack(wkv["xv"][...]), preferred_element_type=F32)
    _store_head_split(k_ref, k)
    _store_head_split(v_ref, v)
    xq = _unpack(wkv["xq"][...])
    xo = _unpack(xo_ref[...])
    for hd in range(X_HEADS):
        sl = slice(hd * X_HEAD_DIM, (hd + 1) * X_HEAD_DIM)
        qk = jnp.dot(xq[:, sl], k[:, sl].T.astype(BF16), preferred_element_type=F32) * ATTN_SCALE
        qk_ref[0, :, hd * N_MEM:(hd + 1) * N_MEM] = _pack(qk.astype(BF16))
        vo = jnp.dot(v[:, sl].astype(BF16), xo[sl, :], preferred_element_type=F32)
        vo_ref[0, hd * N_MEM // 2:(hd + 1) * N_MEM // 2, :] = _pack(vo.astype(BF16))


def _kv_proj_call(mem, wkv, wb):
    nb = mem.shape[0]
    blk = lambda s: pl.BlockSpec((1,) + s, lambda b: (b, 0, 0))
    xo_block = dict(WB_BLOCKS)["xo"]
    return pl.pallas_call(
        _kv_proj_kernel,
        grid=(nb,),
        in_specs=[blk((N_MEM, D_MODEL)), _const_spec(wkv.shape),
                  pl.BlockSpec((D_MODEL // 2, D_MODEL), lambda b: (0, xo_block), pipeline_mode=pl.Buffered(1))],
        out_specs=[blk((KV_ROWS, LANES)), blk((KV_ROWS, LANES)),
                   blk((D_MODEL // 2, X_HEADS * N_MEM)), blk((X_HEADS * N_MEM // 2, D_MODEL))],
        out_shape=[
            jax.ShapeDtypeStruct((nb, KV_ROWS, LANES), F32),
            jax.ShapeDtypeStruct((nb, KV_ROWS, LANES), F32),
            jax.ShapeDtypeStruct((nb, D_MODEL // 2, X_HEADS * N_MEM), U32),
            jax.ShapeDtypeStruct((nb, X_HEADS * N_MEM // 2, D_MODEL), U32),
        ],
        compiler_params=_params(1),
        name="kv_proj",
    )(mem, wkv, wb)


PIPE_STAGES = 3

def _prompt_kernel(x_ref, qk_ref, vo_ref, qs_ref, kc_ref, vc_ref,
                   wa_ref, w1_ref, w2_ref, gmat_ref, conv_w_ref, vec_ref,
                   y_ref, conv_new_ref, pool_new_ref, os_ref,
                   ubuf, pbuf, ybuf, dbuf, mixbuf, obuf, x1buf, x1prev, x2buf, hbuf, z2buf, *, nt):
    wa, vec = _views(wa_ref, WA_FIELDS), _views(vec_ref, VEC_FIELDS)
    w_in_ref, w_out_ref = wa["w_in"], wa["w_out"]
    b_in_ref, conv_b_ref, gn_g_ref, gn_b_ref = vec["b_in"], vec["conv_b"], vec["gn_g"], vec["gn_b"]
    ln1_g_ref, ln1_b_ref = vec["ln1_g"], vec["ln1_b"]
    ln2_g_ref, ln2_b_ref, b1_ref, b2_ref = vec["ln2_g"], vec["ln2_b"], vec["b1"], vec["b2"]
    ln3_g_ref, ln3_b_ref = vec["ln3_g"], vec["ln3_b"]
    tm = PROMPT_BLOCK
    step = pl.program_id(0)
    n_blocks = pl.num_programs(0) - (PIPE_STAGES - 1)
    t = jnp.minimum(step, n_blocks - 1) % nt

    @pl.when(t == 0)
    def _():
        ubuf[:, 0:CONV_PAD, :] = jnp.zeros((CONV_SLABS, CONV_PAD, LANES), F32)
        pbuf[:, 0:POOL_PAD, :] = jnp.zeros((POOL_SLABS, POOL_PAD, LANES), F32)

    @pl.when(t > 0)
    def _():
        ubuf[:, 0:CONV_PAD, :] = ubuf[:, tm:tm + CONV_PAD, :]
        pbuf[:, 0:POOL_PAD, :] = pbuf[:, tm:tm + POOL_PAD, :]

    val = {}

    def mix_in():
        val["x"] = x_ref[0]
        h = _bdot(val["x"], _unpack(w_in_ref[...])) + b_in_ref[...]
        u = h[:, 0:D_CONV] * jax.nn.sigmoid(h[:, D_CONV:2 * D_CONV])
        for s in range(CONV_SLABS):
            ubuf[s, CONV_PAD:CONV_PAD + tm, :] = u[:, s * LANES:(s + 1) * LANES]
        for s in range(POOL_SLABS):
            pbuf[s, POOL_PAD:POOL_PAD + tm, :] = h[:, 2 * D_CONV + s * LANES:2 * D_CONV + (s + 1) * LANES]

    def att_scores():
        x1prev[...] = _layer_norm(x1buf[...], ln1_g_ref[...], ln1_b_ref[...])
        x2buf[...] = _layer_norm(z2buf[...], ln2_g_ref[...], ln2_b_ref[...])
        val["scores"] = _bdot(x1prev[...], _unpack(qk_ref[0]))

    def dec_probs():
        val["probs_dec"] = _sample_attn_probs(qs_ref, kc_ref)

    def att_values():
        for hd in range(X_HEADS):
            cols = slice(hd * N_MEM, (hd + 1) * N_MEM)
            s = val["scores"][:, cols]
            e = jnp.exp(s - jnp.max(s, axis=-1, keepdims=True))
            obuf[:, cols] = (e / jnp.sum(e, axis=-1, keepdims=True)).astype(BF16)

    def mix_windows():
        base = CONV_PAD - CONV_CTX
        for s in range(CONV_SLABS):
            sl = slice(s * LANES, (s + 1) * LANES)
            for c in range(tm // CONV_ROWS):
                r0 = c * CONV_ROWS
                acc = jnp.broadcast_to(conv_b_ref[:, sl], (CONV_ROWS, LANES))
                for k in range(CONV_WIDTH):
                    acc = acc + ubuf[s, base + r0 + k:base + r0 + k + CONV_ROWS, :] * conv_w_ref[k, :, sl]
                ybuf[r0:r0 + CONV_ROWS, sl] = acc

        pos = t * tm + lax.broadcasted_iota(jnp.int32, (tm, POOL_GROUP), 0)
        for gi, w in enumerate(POOL_WINDOWS):
            cur = pbuf[gi, POOL_PAD:POOL_PAD + tm, :]
            ws = cur
            for j in range(1, w):
                ws = ws + pbuf[gi, POOL_PAD - j:POOL_PAD - j + tm, :]
            cnt = jnp.minimum(pos + 1, w).astype(F32)
            dbuf[:, gi * POOL_GROUP:(gi + 1) * POOL_GROUP] = ws / cnt - cur

    def att_out():
        val["attn"] = jnp.dot(obuf[...], _unpack(vo_ref[0]), preferred_element_type=F32)

    def dec_values():
        _sample_attn_values(val["probs_dec"], vc_ref, os_ref, os_ref.shape[0] // vc_ref.shape[0])

    def att_residual():
        z2buf[...] = DN_ALPHA * x1prev[...] + val["attn"]

    def ffn_up():
        hdn = jnp.maximum(_bdot(x2buf[...], _unpack(w1_ref[...])) + b1_ref[...], 0.0)
        hbuf[...] = (hdn * hdn).astype(BF16)

    def mix_acts():
        _mixer_acts(ybuf, dbuf, mixbuf, gmat_ref, gn_g_ref, gn_b_ref)

    def ffn_down():
        val["f"] = jnp.dot(hbuf[...], _unpack(w2_ref[...]), preferred_element_type=F32) + b2_ref[...]

    def mix_out():
        x1buf[...] = _mixer_out(mixbuf, val["x"], w_out_ref)

    def out_norm():
        y_ref[0] = _layer_norm(DN_ALPHA * x2buf[...] + val["f"], ln3_g_ref[...], ln3_b_ref[...])

    @pl.when(step == 0)
    def _():
        x1buf[...] = jnp.zeros((tm, D_MODEL), F32)
        z2buf[...] = jnp.zeros((tm, D_MODEL), F32)

    step_order = (mix_in, att_scores, dec_probs, att_values, mix_windows, att_out, dec_values, att_residual, ffn_up,
                  mix_acts, ffn_down, mix_out, out_norm)
    for piece in step_order:
        piece()

    @pl.when(jnp.logical_and(t == nt - 1, step < n_blocks))
    def _():
        for s in range(CONV_SLABS):
            conv_new_ref[0, :, s * LANES:(s + 1) * LANES] = ubuf[s, CONV_PAD + tm - CONV_CTX:CONV_PAD + tm, :]
        for s in range(POOL_SLABS):
            pool_new_ref[0, :, s * LANES:(s + 1) * LANES] = pbuf[s, POOL_PAD + tm - POOL_CTX:POOL_PAD + tm, :]


def _prompt_call(x, qk, vo, q_dec, k_dec, v_dec, dec_seq, wa, wb, consts):
    assert dict(WB_BLOCKS)["w1"] == 0
    nb, seq, _ = x.shape
    tm = PROMPT_BLOCK
    nt = seq // tm
    n_blocks = nb * nt
    n_dec = k_dec.shape[0]
    dec_per_step = -(-n_dec // n_blocks)
    dec_steps = n_dec // dec_per_step
    dec_rows = dec_per_step * dec_seq
    n_steps = n_blocks + PIPE_STAGES - 1
    assert dec_steps * dec_per_step == n_dec and dec_steps <= n_steps
    assert dec_rows % SUBLANES == 0 and SAMPLE_ATTN_ROWS % dec_rows == 0
    clamp = lambda i: jnp.clip(i, 0, n_blocks - 1)
    cur = lambda g: clamp(g)
    att = lambda g: clamp(g - 1)
    prev = lambda g: clamp(g - 2)
    dec = lambda g: jnp.minimum(g, dec_steps - 1)
    in_specs = [
        pl.BlockSpec((1, tm, D_MODEL), lambda g: (cur(g) // nt, cur(g) % nt, 0)),
        pl.BlockSpec((1,) + qk.shape[1:], lambda g: (att(g) // nt, 0, 0)),
        pl.BlockSpec((1,) + vo.shape[1:], lambda g: (att(g) // nt, 0, 0)),
        pl.BlockSpec((dec_rows, D_MODEL), lambda g: (dec(g), 0)),
        pl.BlockSpec((dec_per_step, KV_ROWS, LANES), lambda g: (dec(g), 0, 0)),
        pl.BlockSpec((dec_per_step, KV_ROWS, LANES), lambda g: (dec(g), 0, 0)),
        _const_spec(wa.shape),
        pl.BlockSpec((D_MODEL // 2, D_FF), lambda g: (0, 0), pipeline_mode=pl.Buffered(1)),
    ] + [_const_spec(w.shape) for w in consts]
    out_specs = [
        pl.BlockSpec((1, tm, D_MODEL), lambda g: (prev(g) // nt, prev(g) % nt, 0)),
        pl.BlockSpec((1, CONV_CTX, D_CONV), lambda g: (cur(g) // nt, 0, 0)),
        pl.BlockSpec((1, POOL_CTX, D_POOL), lambda g: (cur(g) // nt, 0, 0)),
        pl.BlockSpec((dec_rows, D_MODEL), lambda g: (dec(g), 0)),
    ]
    out_shape = [
        jax.ShapeDtypeStruct((nb, seq, D_MODEL), F32),
        jax.ShapeDtypeStruct((nb, CONV_CTX, D_CONV), F32),
        jax.ShapeDtypeStruct((nb, POOL_CTX, D_POOL), F32),
        jax.ShapeDtypeStruct(q_dec.shape, F32),
    ]
    scratch = [
        pltpu.VMEM((CONV_SLABS, CONV_PAD + tm, LANES), F32),
        pltpu.VMEM((POOL_SLABS, POOL_PAD + tm, LANES), F32),
        pltpu.VMEM((tm, D_CONV), F32),
        pltpu.VMEM((tm, D_POOL), F32),
        pltpu.VMEM((tm, D_MODEL), BF16),
        pltpu.VMEM((tm, D_MODEL), BF16),
        pltpu.VMEM((tm, D_MODEL), F32),
        pltpu.VMEM((tm, D_MODEL), F32),
        pltpu.VMEM((tm, D_MODEL), F32),
        pltpu.VMEM((tm, D_FF), BF16),
        pltpu.VMEM((tm, D_MODEL), F32),
    ]
    return pl.pallas_call(
        functools.partial(_prompt_kernel, nt=nt),
        grid=(n_steps,),
        in_specs=in_specs,
        out_specs=out_specs,
        out_shape=out_shape,
        scratch_shapes=scratch,
        compiler_params=_params(1),
        name="prompt_layer",
    )(x, qk, vo, q_dec, k_dec, v_dec, wa, wb, *consts)


def _sample_mixer_kernel(x_ref, sconv_ref, spool_ref,
                         wa_ref, xq_ref, gmat_ref, conv_w_ref, vec_ref,
                         x1_ref, q_ref, conv_new_ref, pool_new_ref,
                         hbuf, yslab, dslab, ybuf, dbuf, mixbuf, *, dec_seq):
    wa, vec = _views(wa_ref, WA_FIELDS), _views(vec_ref, VEC_FIELDS)
    w_in_ref, w_out_ref = wa["w_in"], wa["w_out"]
    b_in_ref, conv_b_ref, gn_g_ref, gn_b_ref = vec["b_in"], vec["conv_b"], vec["gn_g"], vec["gn_b"]
    ln1_g_ref, ln1_b_ref = vec["ln1_g"], vec["ln1_b"]
    gb = SAMPLE_MIX_BATCH
    ts = dec_seq
    x = x_ref[...].reshape(gb * ts, D_MODEL)
    h = _bdot(x, _unpack(w_in_ref[...])) + b_in_ref[...]
    u = h[:, 0:D_CONV] * jax.nn.sigmoid(h[:, D_CONV:2 * D_CONV])
    for s in range(CONV_SLABS):
        hbuf[s] = u[:, s * LANES:(s + 1) * LANES]
    for s in range(POOL_SLABS):
        hbuf[CONV_SLABS + s] = h[:, 2 * D_CONV + s * LANES:2 * D_CONV + (s + 1) * LANES]

    def step_rows(slab, t):
        return hbuf[slab, pl.ds(t, gb, stride=ts), :]

    for s in range(CONV_SLABS):
        sl = slice(s * LANES, (s + 1) * LANES)
        new = [step_rows(s, t) for t in range(ts)]
        ext = lambda j: sconv_ref[j, :, sl] if j < CONV_CTX else new[j - CONV_CTX]
        for j in range(CONV_CTX - ts):
            conv_new_ref[j, :, sl] = sconv_ref[j + ts, :, sl]
        for t in range(ts):
            conv_new_ref[CONV_CTX - ts + t, :, sl] = new[t]
            acc = jnp.broadcast_to(conv_b_ref[:, sl], (gb, LANES))
            for k in range(CONV_WIDTH):
                acc = acc + ext(t + k) * conv_w_ref[k, :, sl]
            yslab[s, pl.ds(t, gb, stride=ts), :] = acc

    for g, w in enumerate(POOL_WINDOWS):
        sl = slice(g * LANES, (g + 1) * LANES)
        new = [step_rows(CONV_SLABS + g, t) for t in range(ts)]
        ext = lambda j: spool_ref[j, :, sl] if j < POOL_CTX else new[j - POOL_CTX]
        for j in range(POOL_CTX - ts):
            pool_new_ref[j, :, sl] = spool_ref[j + ts, :, sl]
        for t in range(ts):
            pool_new_ref[POOL_CTX - ts + t, :, sl] = new[t]
            ws = new[t]
            for j in range(1, w):
                ws = ws + ext(POOL_CTX + t - j)
            cnt = float(min(PAST_LEN + t + 1, w))
            dslab[g, pl.ds(t, gb, stride=ts), :] = ws / cnt - new[t]

    for s in range(CONV_SLABS):
        ybuf[:, s * LANES:(s + 1) * LANES] = yslab[s]
    for s in range(POOL_SLABS):
        dbuf[:, s * LANES:(s + 1) * LANES] = dslab[s]

    _mixer_acts(ybuf, dbuf, mixbuf, gmat_ref, gn_g_ref, gn_b_ref)
    x1 = _layer_norm(_mixer_out(mixbuf, x, w_out_ref), ln1_g_ref[...], ln1_b_ref[...])
    x1_ref[...] = x1
    q_ref[...] = _bdot(x1, _unpack(xq_ref[...])) * ATTN_SCALE


def _sample_mixer_call(xs, sconv_tm, spool_tm, wa, wkv, consts, dec_seq):
    xq_block = dict(WKV_BLOCKS)["xq"]
    nb = xs.shape[0]
    n_tok = nb * dec_seq
    gb = SAMPLE_MIX_BATCH
    rows = gb * dec_seq
    row_spec = pl.BlockSpec((rows, D_MODEL), lambda i: (i, 0))
    conv_spec = pl.BlockSpec((CONV_CTX, gb, D_CONV), lambda i: (0, i, 0))
    pool_spec = pl.BlockSpec((POOL_CTX, gb, D_POOL), lambda i: (0, i, 0))
    out_shape = [
        jax.ShapeDtypeStruct((n_tok, D_MODEL), F32),
        jax.ShapeDtypeStruct((n_tok, D_MODEL), F32),
        jax.ShapeDtypeStruct((CONV_CTX, nb, D_CONV), F32),
        jax.ShapeDtypeStruct((POOL_CTX, nb, D_POOL), F32),
    ]
    scratch = [
        pltpu.VMEM((CONV_SLABS + POOL_SLABS, rows, LANES), F32),
        pltpu.VMEM((CONV_SLABS, rows, LANES), F32),
        pltpu.VMEM((POOL_SLABS, rows, LANES), F32),
        pltpu.VMEM((rows, D_CONV), F32),
        pltpu.VMEM((rows, D_POOL), F32),
        pltpu.VMEM((rows, D_MODEL), BF16),
    ]
    return pl.pallas_call(
        functools.partial(_sample_mixer_kernel, dec_seq=dec_seq),
        grid=(nb // gb,),
        in_specs=[pl.BlockSpec((gb, dec_seq, D_MODEL), lambda i: (i, 0, 0)), conv_spec, pool_spec,
                  _const_spec(wa.shape),
                  pl.BlockSpec((D_MODEL // 2, D_MODEL), lambda i: (0, xq_block), pipeline_mode=pl.Buffered(1))]
        + [_const_spec(w.shape) for w in consts],
        out_specs=[row_spec, row_spec, conv_spec, pool_spec],
        out_shape=out_shape,
        scratch_shapes=scratch,
        compiler_params=_params(1),
        name="sample_mixer",
    )(xs, sconv_tm, spool_tm, wa, wkv, *consts)


def _load_head(ref, b, hd):
    parts = [ref[b, pl.ds(c * X_HEADS + hd, N_MEM, stride=KV_ROW_STRIDE), :] for c in range(HEAD_CHUNKS)]
    return jnp.concatenate(parts, axis=1).astype(BF16)


def _sample_attn_probs(q_ref, k_ref):
    q8 = q_ref[...]
    q = jnp.concatenate([q8] * (SAMPLE_ATTN_ROWS // q8.shape[0]), axis=0).astype(BF16)
    s = jnp.concatenate(
        [lax.dot_general(q[:, hd * X_HEAD_DIM:(hd + 1) * X_HEAD_DIM], _load_head(k_ref, b, hd),
                         (((1,), (1,)), ((), ())), preferred_element_type=F32)
         for b in range(k_ref.shape[0]) for hd in range(X_HEADS)], axis=0)
    e = jnp.exp(s - jnp.max(s, axis=-1, keepdims=True))
    return (e / jnp.sum(e, axis=-1, keepdims=True)).astype(BF16)


def _sample_attn_values(a, v_ref, o_ref, ts):
    for b in range(v_ref.shape[0]):
        for hd in range(X_HEADS):
            i = b * X_HEADS + hd
            o = jnp.dot(a[i * SAMPLE_ATTN_ROWS:(i + 1) * SAMPLE_ATTN_ROWS, :], _load_head(v_ref, b, hd),
                        preferred_element_type=F32)
            o_ref[b * ts:(b + 1) * ts, hd * X_HEAD_DIM:(hd + 1) * X_HEAD_DIM] = o[b * ts:(b + 1) * ts, :]


def _sample_tail_kernel(x1_ref, o_ref, xo_ref, w1_ref, b1_ref, w2_ref, vec_ref, y_ref, x2buf, x2bf, fbuf):
    vec = _views(vec_ref, VEC_FIELDS)
    ln2_g_ref, ln2_b_ref, b2_ref, ln3_g_ref, ln3_b_ref = (
        vec["ln2_g"], vec["ln2_b"], vec["b2"], vec["ln3_g"], vec["ln3_b"])
    c = pl.program_id(0)

    @pl.when(c == 0)
    def _():
        attn = _bdot(o_ref[...], _unpack(xo_ref[...]))
        x2 = _layer_norm(DN_ALPHA * x1_ref[...] + attn, ln2_g_ref[...], ln2_b_ref[...])
        x2buf[...] = x2
        x2bf[...] = x2.astype(BF16)
        fbuf[...] = jnp.broadcast_to(b2_ref[...], fbuf.shape)

    hdn = jnp.maximum(jnp.dot(x2bf[...], _unpack(w1_ref[...]), preferred_element_type=F32) + b1_ref[...], 0.0)
    fbuf[...] += _bdot(hdn * hdn, _unpack(w2_ref[...]))

    @pl.when(c == pl.num_programs(0) - 1)
    def _():
        y = _layer_norm(DN_ALPHA * x2buf[...] + fbuf[...], ln3_g_ref[...], ln3_b_ref[...])
        y_ref[...] = y.reshape(y_ref.shape)


def _sample_tail_call(x1, o, wb, w2_p, vecs, dec_seq):
    n_tok = x1.shape[0]
    y_shape = (n_tok // dec_seq, dec_seq, D_MODEL)
    fc = TAIL_FF_CHUNK
    blocks = dict(WB_BLOCKS)
    assert fc == D_MODEL and VEC_FIELDS[0] == ("b1", D_FF) and blocks["w1"] == 0
    in_specs = [
        _const_spec(x1.shape), _const_spec(o.shape),
        pl.BlockSpec((D_MODEL // 2, D_MODEL), lambda c: (0, blocks["xo"]), pipeline_mode=pl.Buffered(1)),
        pl.BlockSpec((D_MODEL // 2, fc), lambda c: (0, c)),
        pl.BlockSpec((1, fc), lambda c: (0, c)),
        pl.BlockSpec((fc // 2, D_MODEL), lambda c: (c, 0)),
        _const_spec(vecs.shape),
    ]
    return pl.pallas_call(
        _sample_tail_kernel,
        grid=(D_FF // fc,),
        in_specs=in_specs,
        out_specs=pl.BlockSpec(y_shape, lambda c: (0, 0, 0)),
        out_shape=jax.ShapeDtypeStruct(y_shape, F32),
        scratch_shapes=[pltpu.VMEM((n_tok, D_MODEL), F32), pltpu.VMEM((n_tok, D_MODEL), BF16),
                        pltpu.VMEM((n_tok, D_MODEL), F32)],
        compiler_params=_params(1),
        name="sample_tail",
    )(x1, o, wb, wb, vecs, w2_p, vecs)


def _group_indicator():
    gid = jnp.arange(GN_HALF) // CONV_HEAD_DIM
    return (gid[:, None] == gid[None, :]).astype(F32)


def _to_head_split_view(a):
    lead = a.shape[:-3]
    n = len(lead)
    a = a.reshape(lead + (N_MEM, X_HEADS, HEAD_CHUNKS, LANES))
    a = a.transpose(tuple(range(n)) + (n, n + 2, n + 1, n + 3))
    return a.reshape(lead + (KV_ROWS, LANES))


def _from_head_split_view(a):
    lead = a.shape[:-2]
    n = len(lead)
    a = a.reshape(lead + (N_MEM, HEAD_CHUNKS, X_HEADS, LANES))
    a = a.transpose(tuple(range(n)) + (n, n + 2, n + 1, n + 3))
    return a.reshape(lead + (N_MEM, X_HEADS, X_HEAD_DIM))


def kernel(x_prompt, x_sample, mem_prompt, cache_mem_k, cache_mem_v, state_conv, state_pool, w_in, b_in, conv_w, conv_b, gn_g, gn_b, pool_w, pool_scale, w_out, ln1_g, ln1_b, xq_w, xk_w, xv_w, xo_w, ln2_g, ln2_b, w1, b1, w2, b2, ln3_g, ln3_b):
    assert w_in.shape[0] == DEPTH == 1
    n_prompt, seq, _ = x_prompt.shape
    n_dec, dec_seq, _ = x_sample.shape
    assert seq % PROMPT_BLOCK == 0 and n_dec % SAMPLE_MIX_BATCH == 0

    named = dict(w_in=w_in, w_out=w_out, xq=xq_w, xo=xo_w, w1=w1, b1=b1, b_in=b_in, conv_b=conv_b, gn_g=gn_g,
                 gn_b=gn_b, ln1_g=ln1_g, ln1_b=ln1_b, ln2_g=ln2_g, ln2_b=ln2_b, b2=b2,
                 ln3_g=ln3_g, ln3_b=ln3_b)
    named.update(xk=xk_w, xv=xv_w)
    wa, wb, wkv, gmat_p, w2_p, vecs = _cast_call(
        [[named[name][0] for name, _ in WA_FIELDS], [named[name][0] for name, _ in WB_FIELDS],
         [named[name][0] for name, _ in WKV_FIELDS], [_group_indicator()], [w2[0]]],
        [named[name] for name, _ in VEC_FIELDS],
        pool_w[0], pool_scale, fold_index=[name for name, _ in WA_FIELDS].index("w_out"))
    conv_taps = conv_w.transpose(1, 0, 2)

    x1_s, q_s, conv_s, pool_s = _sample_mixer_call(
        x_sample, state_conv[0].transpose(1, 0, 2), state_pool[0].transpose(1, 0, 2),
        wa, wkv, (gmat_p, conv_taps, vecs), dec_seq)

    k_p, v_p, qk_p, vo_p = _kv_proj_call(mem_prompt, wkv, wb)
    y_p, conv_p, pool_p, o_s = _prompt_call(
        x_prompt, qk_p, vo_p, q_s, _to_head_split_view(cache_mem_k[0]), _to_head_split_view(cache_mem_v[0]),
        dec_seq, wa, wb, (w2_p, gmat_p, conv_taps, vecs))

    y_s = _sample_tail_call(x1_s, o_s, wb, w2_p, vecs, dec_seq)

    return (y_p, y_s,
            _from_head_split_view(k_p)[None], _from_head_split_view(v_p)[None],
            conv_p[None], conv_s.transpose(1, 0, 2)[None], pool_p[None], pool_s.transpose(1, 0, 2)[None])
```
